```python
import math
import jax, jax.numpy as jnp
from jax import lax
import numpy as np


D_MODEL = 2048
BATCH = 8
SEQ = 8192
DEPTH = 4

MEM_LEN = 256
GLA_HEADS = 4
GLA_DK = 128
GLA_DV = 256
GLA_GATE_RANK = 16
GLA_GATE_NORMALIZER = 16.0
GLA_CHUNK = 64
DIL_HEADS = 4
DIL_HEAD_DIM = 128
DIL_CONFIGS = ((128, 1), (512, 4), (2048, 16))
MEM_HEADS = 4
MEM_HEAD_DIM = 128
REL_BUCKETS = 32
REL_MAX_DISTANCE = 1024
D_FF = 4 * D_MODEL
EPS = 1e-6
NEG_INF = -1e30

GLA_QK_WIDTH = GLA_HEADS * GLA_DK
GLA_V_WIDTH = GLA_HEADS * GLA_DV
DIL_WIDTH = DIL_HEADS * DIL_HEAD_DIM
MEM_WIDTH = MEM_HEADS * MEM_HEAD_DIM
MIX_WIDTH = GLA_V_WIDTH + DIL_WIDTH + MEM_WIDTH
IN_SPLITS = (GLA_QK_WIDTH, GLA_QK_WIDTH, GLA_V_WIDTH, GLA_V_WIDTH, GLA_GATE_RANK, GLA_GATE_RANK,
             DIL_WIDTH, DIL_WIDTH, DIL_WIDTH, MEM_WIDTH)
IN_WIDTH = 2 * GLA_QK_WIDTH + 2 * GLA_V_WIDTH + 2 * GLA_GATE_RANK + 3 * DIL_WIDTH + MEM_WIDTH

kernel_name = 'hymba_gla_dilated_memxattn_encoder'


def rms_norm(x, gain):
    xf = x.astype(jnp.float32)
    y = xf * lax.rsqrt(jnp.mean(xf * xf, axis=-1, keepdims=True) + EPS)
    return (y * gain.astype(jnp.float32)).astype(x.dtype)


def head_rms_norm(x, gain):
    h, e = x.shape[-2:]
    xf = x.astype(jnp.float32)
    y = xf * lax.rsqrt(jnp.mean(xf * xf, axis=-1, keepdims=True) + EPS)
    return y * gain.reshape(h, e).astype(jnp.float32)


def split_columns(t, sizes):
    outs, start = [], 0
    for s in sizes:
        outs.append(t[..., start:start + s])
        start += s
    return outs


def t5_bucket(rel):
    half = REL_BUCKETS // 2
    max_exact = half // 2
    ret = jnp.where(rel > 0, half, 0)
    n = jnp.abs(rel)
    nf = jnp.maximum(n, 1).astype(jnp.float32)
    large = max_exact + (jnp.log(nf / max_exact) / math.log(REL_MAX_DISTANCE / max_exact)
                         * (half - max_exact)).astype(jnp.int32)
    large = jnp.minimum(large, half - 1)
    return ret + jnp.where(n < max_exact, n, large)


def gla_direction(q, k, v, g):
    b_, h_, s_, dk = q.shape
    dv = v.shape[-1]
    c = GLA_CHUNK
    n = s_ // c
    q, k, g = [t.reshape(b_, h_, n, c, dk) for t in (q, k, g)]
    v = v.reshape(b_, h_, n, c, dv)
    b = jnp.cumsum(g, axis=3)
    b_last = b[:, :, :, -1:, :]
    q_dec = q * jnp.exp(b)
    k_inv = k * jnp.exp(-b)
    k_end = k * jnp.exp(b_last - b)
    causal = jnp.tril(jnp.ones((c, c), dtype=bool))
    a = jnp.where(causal, jnp.einsum('bhncd,bhnsd->bhncs', q_dec, k_inv), 0.0)
    o_intra = jnp.einsum('bhncs,bhnsv->bhncv', a, v)
    d_state = jnp.einsum('bhncd,bhncv->nbhdv', k_end, v)
    decay = jnp.moveaxis(jnp.exp(b_last[:, :, :, 0, :]), 2, 0)
    q_chunks = jnp.moveaxis(q_dec, 2, 0)

    def step(state, inp):
        ds, dec, qc = inp
        out = jnp.einsum('bhcd,bhdv->bhcv', qc, state)
        return dec[..., None] * state + ds, out

    state0 = jnp.zeros((b_, h_, dk, dv), dtype=q.dtype)
    _, o_inter = lax.scan(step, state0, (d_state, decay, q_chunks))
    o = o_intra + jnp.moveaxis(o_inter, 0, 2)
    return o.reshape(b_, h_, s_, dv)


def gla_mixer(q, k, v, r, lr_f, lr_b, up_f, bias_f, up_b, bias_b, norm_gain):
    b_, s_ = q.shape[:2]

    def heads(t, e):
        return t.reshape(b_, s_, GLA_HEADS, e).transpose(0, 2, 1, 3).astype(jnp.float32)

    def log_gate(lr, up, bias):
        logits = (jnp.einsum('bsr,rk->bsk', lr, up) + bias).astype(jnp.float32)
        return heads(jax.nn.log_sigmoid(logits) / GLA_GATE_NORMALIZER, GLA_DK)

    qh = heads(q, GLA_DK) * (GLA_DK ** -0.5)
    kh = heads(k, GLA_DK)
    vh = heads(v, GLA_DV)
    g_f = log_gate(lr_f, up_f, bias_f)
    g_b = log_gate(lr_b, up_b, bias_b)
    o_f = gla_direction(qh, kh, vh, g_f)
    flip = lambda t: jnp.flip(t, axis=2)
    o_b = flip(gla_direction(flip(qh), flip(kh), flip(vh), flip(g_b)))
    o = (o_f + o_b).transpose(0, 2, 1, 3)
    gate = jax.nn.silu(r.reshape(b_, s_, GLA_HEADS, GLA_DV).astype(jnp.float32))
    o = head_rms_norm(o, norm_gain) * gate
    return o.reshape(b_, s_, GLA_V_WIDTH)


def dilated_branch(q, k, v, rel_table, window, dilation):
    b_, s_, h_, e = q.shape
    w = window // (2 * dilation)
    l = s_ // dilation
    nb = -(-l // w)
    lp = nb * w

    def stride(t):
        return t.reshape(b_, l, dilation, h_, e).transpose(0, 2, 3, 1, 4)

    qs = jnp.pad(stride(q), ((0, 0), (0, 0), (0, 0), (0, lp - l), (0, 0))).reshape(b_, dilation, h_, nb, w, e)

    def windows(t):
        tp = jnp.pad(stride(t), ((0, 0), (0, 0), (0, 0), (w, lp - l + w), (0, 0)))
        tp = tp.reshape(b_, dilation, h_, nb + 2, w, e)
        return jnp.concatenate([tp[:, :, :, :-2], tp[:, :, :, 1:-1], tp[:, :, :, 2:]], axis=4)

    kw = windows(k)
    vw = windows(v)
    rel_sub = jnp.arange(3 * w)[None, :] - w - jnp.arange(w)[:, None]
    bias = jnp.transpose(rel_table[t5_bucket(rel_sub * dilation)], (2, 0, 1)).astype(jnp.float32)
    key_pos = jnp.arange(nb)[:, None] * w - w + jnp.arange(3 * w)[None, :]
    mask = (jnp.abs(rel_sub) <= w)[None] & ((key_pos >= 0) & (key_pos < l))[:, None, :]
    s = jnp.einsum('bdhnqe,bdhnke->bdhnqk', qs, kw, preferred_element_type=jnp.float32) * (e ** -0.5)
    s = jnp.where(mask, s + bias[None, None, :, None], NEG_INF)
    m = jnp.max(s, axis=-1, keepdims=True)
    p = jnp.exp(s - m)
    den = jnp.sum(p, axis=-1, keepdims=True)
    o = jnp.einsum('bdhnqk,bdhnke->bdhnqe', p, vw.astype(jnp.float32)) / den
    lse = m + jnp.log(den)

    def unstride(t):
        t = t.reshape(b_, dilation, h_, lp, t.shape[-1])[:, :, :, :l]
        return t.transpose(0, 3, 1, 2, 4).reshape(b_, s_, h_, t.shape[-1])

    return unstride(o), unstride(lse)[..., 0]


def dilated_mixer(q, k, v, rel_table, norm_gain):
    b_, s_ = q.shape[:2]
    qh, kh, vh = [t.reshape(b_, s_, DIL_HEADS, DIL_HEAD_DIM) for t in (q, k, v)]
    outs, lses = [], []
    for window, dilation in DIL_CONFIGS:
        o, lse = dilated_branch(qh, kh, vh, rel_table, window, dilation)
        outs.append(o)
        lses.append(lse)
    weights = jax.nn.softmax(jnp.stack(lses, axis=0), axis=0)
    o = jnp.einsum('rbsh,rbshe->bshe', weights, jnp.stack(outs, axis=0))
    return head_rms_norm(o, norm_gain).reshape(b_, s_, DIL_WIDTH)


def memory_mixer(q, mem, mem_gain, w_mem_kv, norm_gain):
    b_, s_ = q.shape[:2]
    qh = q.reshape(b_, s_, MEM_HEADS, MEM_HEAD_DIM)
    kv = jnp.einsum('bmd,dk->bmk', rms_norm(mem, mem_gain), w_mem_kv)
    km, vm = [t.reshape(b_, mem.shape[1], MEM_HEADS, MEM_HEAD_DIM) for t in split_columns(kv, (MEM_WIDTH, MEM_WIDTH))]
    s = jnp.einsum('bshe,bmhe->bhsm', qh, km, preferred_element_type=jnp.float32) * (MEM_HEAD_DIM ** -0.5)
    p = jax.nn.softmax(s, axis=-1)
    o = jnp.einsum('bhsm,bmhe->bshe', p, vm.astype(jnp.float32))
    return head_rms_norm(o, norm_gain).reshape(b_, s_, MEM_WIDTH)


def _fwd_setup_inputs(seed: int = 0) -> dict:
    key = jax.random.key(seed)
    ks = jax.random.split(key, 24)
    f32 = jnp.float32
    nrm = lambda k, shape, scale: jax.random.normal(k, shape, f32) * scale
    gain = lambda k, shape: 1.0 + 0.02 * jax.random.normal(k, shape, f32)
    return {
        'x': nrm(ks[0], (BATCH, SEQ, D_MODEL), 1.0),
        'mem': nrm(ks[1], (BATCH, MEM_LEN, D_MODEL), 1.0),
        'norm_mix': gain(ks[2], (DEPTH, D_MODEL)),
        'w_in': nrm(ks[3], (DEPTH, D_MODEL, IN_WIDTH), D_MODEL ** -0.5),
        'gla_gate_up_fwd': nrm(ks[4], (DEPTH, GLA_GATE_RANK, GLA_QK_WIDTH), GLA_GATE_RANK ** -0.5),
        'gla_gate_bias_fwd': nrm(ks[5], (DEPTH, GLA_QK_WIDTH), 0.1),
        'gla_gate_up_bwd': nrm(ks[6], (DEPTH, GLA_GATE_RANK, GLA_QK_WIDTH), GLA_GATE_RANK ** -0.5),
        'gla_gate_bias_bwd': nrm(ks[7], (DEPTH, GLA_QK_WIDTH), 0.1),
        'gla_norm': gain(ks[8], (DEPTH, GLA_V_WIDTH)),
        'rel_bias': nrm(ks[9], (REL_BUCKETS, DIL_HEADS), 0.5),
        'dil_norm': gain(ks[10], (DEPTH, DIL_WIDTH)),
        'mem_norm': gain(ks[11], (DEPTH, D_MODEL)),
        'w_mem_kv': nrm(ks[12], (DEPTH, D_MODEL, 2 * MEM_WIDTH), D_MODEL ** -0.5),
        'mem_out_norm': gain(ks[13], (DEPTH, MEM_WIDTH)),
        'w_out': nrm(ks[14], (DEPTH, MIX_WIDTH, D_MODEL), MIX_WIDTH ** -0.5),
        'norm_mlp': gain(ks[15], (DEPTH, D_MODEL)),
        'w_up': nrm(ks[16], (DEPTH, D_MODEL, D_FF), D_MODEL ** -0.5),
        'w_down': nrm(ks[17], (DEPTH, D_FF, D_MODEL), D_FF ** -0.5),
        'norm_final': gain(ks[18], (D_MODEL,)),
    }


def _fwd_reference(x, mem, norm_mix, w_in, gla_gate_up_fwd, gla_gate_bias_fwd, gla_gate_up_bwd, gla_gate_bias_bwd,
              gla_norm, rel_bias, dil_norm, mem_norm, w_mem_kv, mem_out_norm, w_out, norm_mlp, w_up, w_down,
              norm_final):
    for l in range(DEPTH):
        h = rms_norm(x, norm_mix[l])
        proj = jnp.einsum('bsd,dk->bsk', h, w_in[l])
        (g_q, g_k, g_v, g_r, lr_f, lr_b, d_q, d_k, d_v, m_q) = split_columns(proj, IN_SPLITS)
        gla_out = gla_mixer(g_q, g_k, g_v, g_r, lr_f, lr_b, gla_gate_up_fwd[l], gla_gate_bias_fwd[l],
                            gla_gate_up_bwd[l], gla_gate_bias_bwd[l], gla_norm[l])
        dil_out = dilated_mixer(d_q, d_k, d_v, rel_bias, dil_norm[l])
        mem_out = memory_mixer(m_q, mem, mem_norm[l], w_mem_kv[l], mem_out_norm[l])
        mixed = jnp.concatenate([gla_out, dil_out, mem_out], axis=-1).astype(x.dtype)
        x = x + jnp.einsum('bsk,kd->bsd', mixed, w_out[l])
        h = rms_norm(x, norm_mlp[l])
        u = jnp.square(jax.nn.relu(jnp.einsum('bsd,df->bsf', h, w_up[l])))
        x = x + jnp.einsum('bsf,fd->bsd', u, w_down[l])
    return rms_norm(x, norm_final)


import jax as _jax
import jax.numpy as _jnp

TWIN_FORMAT = 'train_step'
FWD_PARAMS = ['x', 'mem', 'norm_mix', 'w_in', 'gla_gate_up_fwd', 'gla_gate_bias_fwd', 'gla_gate_up_bwd', 'gla_gate_bias_bwd', 'gla_norm', 'rel_bias', 'dil_norm', 'mem_norm', 'w_mem_kv', 'mem_out_norm', 'w_out', 'norm_mlp', 'w_up', 'w_down', 'norm_final']
TWIN_WEIGHTS = ['norm_mix', 'w_in', 'gla_gate_up_fwd', 'gla_gate_bias_fwd', 'gla_gate_up_bwd', 'gla_gate_bias_bwd', 'gla_norm', 'rel_bias', 'dil_norm', 'mem_norm', 'w_mem_kv', 'mem_out_norm', 'w_out', 'norm_mlp', 'w_up', 'w_down', 'norm_final']
TWIN_DIFF_INPUT = 'x'
TWIN_INPUTS = ['x', 'mem', 'norm_mix', 'w_in', 'gla_gate_up_fwd', 'gla_gate_bias_fwd', 'gla_gate_up_bwd', 'gla_gate_bias_bwd', 'gla_norm', 'rel_bias', 'dil_norm', 'mem_norm', 'w_mem_kv', 'mem_out_norm', 'w_out', 'norm_mlp', 'w_up', 'w_down', 'norm_final', 'loss_target', 'm_norm_mix', 'm_w_in', 'm_gla_gate_up_fwd', 'm_gla_gate_bias_fwd', 'm_gla_gate_up_bwd', 'm_gla_gate_bias_bwd', 'm_gla_norm', 'm_rel_bias', 'm_dil_norm', 'm_mem_norm', 'm_w_mem_kv', 'm_mem_out_norm', 'm_w_out', 'm_norm_mlp', 'm_w_up', 'm_w_down', 'm_norm_final', 'v_norm_mix', 'v_w_in', 'v_gla_gate_up_fwd', 'v_gla_gate_bias_fwd', 'v_gla_gate_up_bwd', 'v_gla_gate_bias_bwd', 'v_gla_norm', 'v_rel_bias', 'v_dil_norm', 'v_mem_norm', 'v_w_mem_kv', 'v_mem_out_norm', 'v_w_out', 'v_norm_mlp', 'v_w_up', 'v_w_down', 'v_norm_final']
TWIN_OUTPUTS = ['loss', 'grad_x', 'grad_norm_mix', 'grad_w_in', 'grad_gla_gate_up_fwd', 'grad_gla_gate_bias_fwd', 'grad_gla_gate_up_bwd', 'grad_gla_gate_bias_bwd', 'grad_gla_norm', 'grad_rel_bias', 'grad_dil_norm', 'grad_mem_norm', 'grad_w_mem_kv', 'grad_mem_out_norm', 'grad_w_out', 'grad_norm_mlp', 'grad_w_up', 'grad_w_down', 'grad_norm_final', 'delta_norm_mix', 'delta_w_in', 'delta_gla_gate_up_fwd', 'delta_gla_gate_bias_fwd', 'delta_gla_gate_up_bwd', 'delta_gla_gate_bias_bwd', 'delta_gla_norm', 'delta_rel_bias', 'delta_dil_norm', 'delta_mem_norm', 'delta_w_mem_kv', 'delta_mem_out_norm', 'delta_w_out', 'delta_norm_mlp', 'delta_w_up', 'delta_w_down', 'delta_norm_final', 'new_m_norm_mix', 'new_m_w_in', 'new_m_gla_gate_up_fwd', 'new_m_gla_gate_bias_fwd', 'new_m_gla_gate_up_bwd', 'new_m_gla_gate_bias_bwd', 'new_m_gla_norm', 'new_m_rel_bias', 'new_m_dil_norm', 'new_m_mem_norm', 'new_m_w_mem_kv', 'new_m_mem_out_norm', 'new_m_w_out', 'new_m_norm_mlp', 'new_m_w_up', 'new_m_w_down', 'new_m_norm_final', 'new_v_norm_mix', 'new_v_w_in', 'new_v_gla_gate_up_fwd', 'new_v_gla_gate_bias_fwd', 'new_v_gla_gate_up_bwd', 'new_v_gla_gate_bias_bwd', 'new_v_gla_norm', 'new_v_rel_bias', 'new_v_dil_norm', 'new_v_mem_norm', 'new_v_w_mem_kv', 'new_v_mem_out_norm', 'new_v_w_out', 'new_v_norm_mlp', 'new_v_w_up', 'new_v_w_down', 'new_v_norm_final']
TWIN_LEAF_KINDS = {'loss': 'loss', 'grad_x': 'grad_x', 'grad_norm_mix': 'grad_w', 'grad_w_in': 'grad_w', 'grad_gla_gate_up_fwd': 'grad_w', 'grad_gla_gate_bias_fwd': 'grad_w', 'grad_gla_gate_up_bwd': 'grad_w', 'grad_gla_gate_bias_bwd': 'grad_w', 'grad_gla_norm': 'grad_w', 'grad_rel_bias': 'grad_w', 'grad_dil_norm': 'grad_w', 'grad_mem_norm': 'grad_w', 'grad_w_mem_kv': 'grad_w', 'grad_mem_out_norm': 'grad_w', 'grad_w_out': 'grad_w', 'grad_norm_mlp': 'grad_w', 'grad_w_up': 'grad_w', 'grad_w_down': 'grad_w', 'grad_norm_final': 'grad_w', 'delta_norm_mix': 'delta_w', 'delta_w_in': 'delta_w', 'delta_gla_gate_up_fwd': 'delta_w', 'delta_gla_gate_bias_fwd': 'delta_w', 'delta_gla_gate_up_bwd': 'delta_w', 'delta_gla_gate_bias_bwd': 'delta_w', 'delta_gla_norm': 'delta_w', 'delta_rel_bias': 'delta_w', 'delta_dil_norm': 'delta_w', 'delta_mem_norm': 'delta_w', 'delta_w_mem_kv': 'delta_w', 'delta_mem_out_norm': 'delta_w', 'delta_w_out': 'delta_w', 'delta_norm_mlp': 'delta_w', 'delta_w_up': 'delta_w', 'delta_w_down': 'delta_w', 'delta_norm_final': 'delta_w', 'new_m_norm_mix': 'new_m', 'new_m_w_in': 'new_m', 'new_m_gla_gate_up_fwd': 'new_m', 'new_m_gla_gate_bias_fwd': 'new_m', 'new_m_gla_gate_up_bwd': 'new_m', 'new_m_gla_gate_bias_bwd': 'new_m', 'new_m_gla_norm': 'new_m', 'new_m_rel_bias': 'new_m', 'new_m_dil_norm': 'new_m', 'new_m_mem_norm': 'new_m', 'new_m_w_mem_kv': 'new_m', 'new_m_mem_out_norm': 'new_m', 'new_m_w_out': 'new_m', 'new_m_norm_mlp': 'new_m', 'new_m_w_up': 'new_m', 'new_m_w_down': 'new_m', 'new_m_norm_final': 'new_m', 'new_v_norm_mix': 'new_v', 'new_v_w_in': 'new_v', 'new_v_gla_gate_up_fwd': 'new_v', 'new_v_gla_gate_bias_fwd': 'new_v', 'new_v_gla_gate_up_bwd': 'new_v', 'new_v_gla_gate_bias_bwd': 'new_v', 'new_v_gla_norm': 'new_v', 'new_v_rel_bias': 'new_v', 'new_v_dil_norm': 'new_v', 'new_v_mem_norm': 'new_v', 'new_v_w_mem_kv': 'new_v', 'new_v_mem_out_norm': 'new_v', 'new_v_w_out': 'new_v', 'new_v_norm_mlp': 'new_v', 'new_v_w_up': 'new_v', 'new_v_w_down': 'new_v', 'new_v_norm_final': 'new_v'}


def _forward(args):
    return _fwd_reference(*[args[k] for k in FWD_PARAMS])


def _output_shape():
    def fwd():
        inp = _fwd_setup_inputs(0)
        return _fwd_reference(*[inp[k] for k in FWD_PARAMS])
    out = _jax.eval_shape(fwd)
    return out.shape, out.dtype

N_MICROBATCH = 1
ADAM_LR = 0.001
ADAM_B1 = 0.9
ADAM_B2 = 0.999
ADAM_EPS = 1e-08
ADAM_WD = 0.01
ADAM_STEP = 10
PER_EXAMPLE_BATCH_AXIS = {'x': 0, 'mem': 0, 'loss_target': 0}
SHARED_INPUTS = []
_WEIGHT_DTYPES = {'norm_mix': _jnp.float32, 'w_in': _jnp.float32, 'gla_gate_up_fwd': _jnp.float32, 'gla_gate_bias_fwd': _jnp.float32, 'gla_gate_up_bwd': _jnp.float32, 'gla_gate_bias_bwd': _jnp.float32, 'gla_norm': _jnp.float32, 'rel_bias': _jnp.float32, 'dil_norm': _jnp.float32, 'mem_norm': _jnp.float32, 'w_mem_kv': _jnp.float32, 'mem_out_norm': _jnp.float32, 'w_out': _jnp.float32, 'norm_mlp': _jnp.float32, 'w_up': _jnp.float32, 'w_down': _jnp.float32, 'norm_final': _jnp.float32}
MOMENT_SCALE = {'norm_mix': 1.109426e-01, 'w_in': 6.990535e-02, 'gla_gate_up_fwd': 9.963620e-03, 'gla_gate_bias_fwd': 2.440409e-02, 'gla_gate_up_bwd': 8.638774e-03, 'gla_gate_bias_bwd': 2.394924e-02, 'gla_norm': 4.818326e-02, 'rel_bias': 2.360981e-01, 'dil_norm': 1.203910e-01, 'mem_norm': 6.680946e-02, 'w_mem_kv': 9.022164e-02, 'mem_out_norm': 1.008415e-01, 'w_out': 8.427347e-02, 'norm_mlp': 9.169387e-02, 'w_up': 4.612488e-02, 'w_down': 1.071211e-01, 'norm_final': 3.331609e+01}


def _to_microbatches(a, axis):
    t = _jnp.moveaxis(a, axis, 0)
    t = t.reshape((N_MICROBATCH, t.shape[0] // N_MICROBATCH) + t.shape[1:])
    return _jnp.moveaxis(t, 1, axis + 1)


def setup_inputs(seed: int = 0) -> dict:
    inp = _fwd_setup_inputs(seed)
    key = _jax.random.fold_in(_jax.random.key(seed), 7919)
    shape, _ = _output_shape()
    out = dict(inp)
    out["loss_target"] = _jax.random.normal(_jax.random.fold_in(key, 0), shape, _jnp.float32)
    for i, name in enumerate(TWIN_WEIGHTS):
        w = inp[name].astype(_jnp.float32)
        if MOMENT_SCALE is None:
            s = _jnp.sqrt(_jnp.mean(_jnp.square(w)) + 1e-30)
        else:
            s = MOMENT_SCALE[name]
        km, kv = _jax.random.split(_jax.random.fold_in(key, i + 1))
        out[name] = w
        out["m_" + name] = s * _jax.random.normal(km, w.shape, _jnp.float32)
        out["v_" + name] = (s * s) * _jax.random.uniform(kv, w.shape, _jnp.float32, 0.5, 1.5)
    if N_MICROBATCH > 1:
        for name, axis in PER_EXAMPLE_BATCH_AXIS.items():
            out[name] = _to_microbatches(out[name], axis)
    return {'x': out['x'], 'mem': out['mem'], 'norm_mix': out['norm_mix'], 'w_in': out['w_in'], 'gla_gate_up_fwd': out['gla_gate_up_fwd'], 'gla_gate_bias_fwd': out['gla_gate_bias_fwd'], 'gla_gate_up_bwd': out['gla_gate_up_bwd'], 'gla_gate_bias_bwd': out['gla_gate_bias_bwd'], 'gla_norm': out['gla_norm'], 'rel_bias': out['rel_bias'], 'dil_norm': out['dil_norm'], 'mem_norm': out['mem_norm'], 'w_mem_kv': out['w_mem_kv'], 'mem_out_norm': out['mem_out_norm'], 'w_out': out['w_out'], 'norm_mlp': out['norm_mlp'], 'w_up': out['w_up'], 'w_down': out['w_down'], 'norm_final': out['norm_final'], 'loss_target': out['loss_target'], 'm_norm_mix': out['m_norm_mix'], 'm_w_in': out['m_w_in'], 'm_gla_gate_up_fwd': out['m_gla_gate_up_fwd'], 'm_gla_gate_bias_fwd': out['m_gla_gate_bias_fwd'], 'm_gla_gate_up_bwd': out['m_gla_gate_up_bwd'], 'm_gla_gate_bias_bwd': out['m_gla_gate_bias_bwd'], 'm_gla_norm': out['m_gla_norm'], 'm_rel_bias': out['m_rel_bias'], 'm_dil_norm': out['m_dil_norm'], 'm_mem_norm': out['m_mem_norm'], 'm_w_mem_kv': out['m_w_mem_kv'], 'm_mem_out_norm': out['m_mem_out_norm'], 'm_w_out': out['m_w_out'], 'm_norm_mlp': out['m_norm_mlp'], 'm_w_up': out['m_w_up'], 'm_w_down': out['m_w_down'], 'm_norm_final': out['m_norm_final'], 'v_norm_mix': out['v_norm_mix'], 'v_w_in': out['v_w_in'], 'v_gla_gate_up_fwd': out['v_gla_gate_up_fwd'], 'v_gla_gate_bias_fwd': out['v_gla_gate_bias_fwd'], 'v_gla_gate_up_bwd': out['v_gla_gate_up_bwd'], 'v_gla_gate_bias_bwd': out['v_gla_gate_bias_bwd'], 'v_gla_norm': out['v_gla_norm'], 'v_rel_bias': out['v_rel_bias'], 'v_dil_norm': out['v_dil_norm'], 'v_mem_norm': out['v_mem_norm'], 'v_w_mem_kv': out['v_w_mem_kv'], 'v_mem_out_norm': out['v_mem_out_norm'], 'v_w_out': out['v_w_out'], 'v_norm_mlp': out['v_norm_mlp'], 'v_w_up': out['v_w_up'], 'v_w_down': out['v_w_down'], 'v_norm_final': out['v_norm_final']}


def _loss(weights, diff, rest, loss_target):
    with _jax.named_scope("forward"):
        args = {**rest, TWIN_DIFF_INPUT: diff, **{k: w.astype(_WEIGHT_DTYPES[k]) for k, w in weights.items()}}
        y = _forward(args)
    with _jax.named_scope("loss_head"):
        err = _jnp.square(y.astype(_jnp.float32) - loss_target)
        return 0.5 * _jnp.sum(_jnp.mean(err, axis=-1)) if err.ndim else 0.5 * err


def _adamw(w, g, m, v):
    m = ADAM_B1 * m + (1.0 - ADAM_B1) * g
    v = ADAM_B2 * v + (1.0 - ADAM_B2) * _jnp.square(g)
    m_hat = m / (1.0 - ADAM_B1 ** ADAM_STEP)
    v_hat = v / (1.0 - ADAM_B2 ** ADAM_STEP)
    delta = -ADAM_LR * (m_hat / (_jnp.sqrt(v_hat) + ADAM_EPS) + ADAM_WD * w)
    return delta, m, v


def reference(x, mem, norm_mix, w_in, gla_gate_up_fwd, gla_gate_bias_fwd, gla_gate_up_bwd, gla_gate_bias_bwd, gla_norm, rel_bias, dil_norm, mem_norm, w_mem_kv, mem_out_norm, w_out, norm_mlp, w_up, w_down, norm_final, loss_target, m_norm_mix, m_w_in, m_gla_gate_up_fwd, m_gla_gate_bias_fwd, m_gla_gate_up_bwd, m_gla_gate_bias_bwd, m_gla_norm, m_rel_bias, m_dil_norm, m_mem_norm, m_w_mem_kv, m_mem_out_norm, m_w_out, m_norm_mlp, m_w_up, m_w_down, m_norm_final, v_norm_mix, v_w_in, v_gla_gate_up_fwd, v_gla_gate_bias_fwd, v_gla_gate_up_bwd, v_gla_gate_bias_bwd, v_gla_norm, v_rel_bias, v_dil_norm, v_mem_norm, v_w_mem_kv, v_mem_out_norm, v_w_out, v_norm_mlp, v_w_up, v_w_down, v_norm_final):
    given = dict(x=x, mem=mem, norm_mix=norm_mix, w_in=w_in, gla_gate_up_fwd=gla_gate_up_fwd, gla_gate_bias_fwd=gla_gate_bias_fwd, gla_gate_up_bwd=gla_gate_up_bwd, gla_gate_bias_bwd=gla_gate_bias_bwd, gla_norm=gla_norm, rel_bias=rel_bias, dil_norm=dil_norm, mem_norm=mem_norm, w_mem_kv=w_mem_kv, mem_out_norm=mem_out_norm, w_out=w_out, norm_mlp=norm_mlp, w_up=w_up, w_down=w_down, norm_final=norm_final, loss_target=loss_target, m_norm_mix=m_norm_mix, m_w_in=m_w_in, m_gla_gate_up_fwd=m_gla_gate_up_fwd, m_gla_gate_bias_fwd=m_gla_gate_bias_fwd, m_gla_gate_up_bwd=m_gla_gate_up_bwd, m_gla_gate_bias_bwd=m_gla_gate_bias_bwd, m_gla_norm=m_gla_norm, m_rel_bias=m_rel_bias, m_dil_norm=m_dil_norm, m_mem_norm=m_mem_norm, m_w_mem_kv=m_w_mem_kv, m_mem_out_norm=m_mem_out_norm, m_w_out=m_w_out, m_norm_mlp=m_norm_mlp, m_w_up=m_w_up, m_w_down=m_w_down, m_norm_final=m_norm_final, v_norm_mix=v_norm_mix, v_w_in=v_w_in, v_gla_gate_up_fwd=v_gla_gate_up_fwd, v_gla_gate_bias_fwd=v_gla_gate_bias_fwd, v_gla_gate_up_bwd=v_gla_gate_up_bwd, v_gla_gate_bias_bwd=v_gla_gate_bias_bwd, v_gla_norm=v_gla_norm, v_rel_bias=v_rel_bias, v_dil_norm=v_dil_norm, v_mem_norm=v_mem_norm, v_w_mem_kv=v_w_mem_kv, v_mem_out_norm=v_mem_out_norm, v_w_out=v_w_out, v_norm_mlp=v_norm_mlp, v_w_up=v_w_up, v_w_down=v_w_down, v_norm_final=v_norm_final)
    weights = {n: given[n] for n in TWIN_WEIGHTS}
    shared = {n: given[n] for n in SHARED_INPUTS}
    per_example = {n: given[n] for n in ['x', 'mem']}
    grad_fn = _jax.value_and_grad(_loss, argnums=(0, 1))

    def one_microbatch(ex, loss_target):
        ex = dict(ex)
        diff = ex.pop(TWIN_DIFF_INPUT)
        return grad_fn(weights, diff, {**shared, **ex}, loss_target)

    if N_MICROBATCH == 1:
        loss, (grad_w, grad_x) = one_microbatch(per_example, given["loss_target"])
    else:
        def body(carry, xs):
            loss_sum, grad_sum = carry
            l_k, (gw_k, gx_k) = one_microbatch(xs[0], xs[1])
            with _jax.named_scope("update"):
                return (loss_sum + l_k, _jax.tree.map(_jnp.add, grad_sum, gw_k)), gx_k

        init = (_jnp.zeros((), _jnp.float32), _jax.tree.map(_jnp.zeros_like, weights))
        (loss, grad_w), grad_x = _jax.lax.scan(body, init, (per_example, given["loss_target"]))
    with _jax.named_scope("update"):
        delta_w, new_m, new_v = {}, {}, {}
        for n in TWIN_WEIGHTS:
            delta_w[n], new_m[n], new_v[n] = _adamw(weights[n], grad_w[n], given["m_" + n], given["v_" + n])
    return (loss, grad_x, *[grad_w[n] for n in TWIN_WEIGHTS], *[delta_w[n] for n in TWIN_WEIGHTS],
            *[new_m[n] for n in TWIN_WEIGHTS], *[new_v[n] for n in TWIN_WEIGHTS])
```

```python
import functools
import math

import numpy as np
import jax
import jax.numpy as jnp
from jax import lax
from jax.experimental import pallas as pl
from jax.experimental.pallas import tpu as pltpu

F32 = jnp.float32
BF16 = jnp.bfloat16

N_DEV = 8
DEPTH = 4
HEADS = 4
GLA_DK = 128
GLA_DV = 256
GLA_RANK = 16
GLA_GATE_NORMALIZER = 16.0
GLA_CHUNK = 64
HEAD_DIM = 128
DIL_CONFIGS = ((128, 1), (512, 4), (2048, 16))
DIL_HALF = 64
REL_BUCKETS = 32
REL_MAX_DISTANCE = 1024
EPS = 1e-6
NEG_INF = -1e30
IN_SPLITS = (512, 512, 1024, 1024, 16, 16, 512, 512, 512, 512)
IN_WIDTH = sum(IN_SPLITS)
MAIN_WIDTH = IN_WIDTH - 2 * GLA_RANK
LR_PAD = 128
OFF_GQ, OFF_GK, OFF_GV, OFF_GR, OFF_DQ, OFF_DK, OFF_DV, OFF_MQ = 0, 512, 1024, 2048, 3072, 3584, 4096, 4608

ADAM_LR = 0.001
ADAM_B1 = 0.9
ADAM_B2 = 0.999
ADAM_EPS = 1e-08
ADAM_WD = 0.01
ADAM_STEP = 10

VMEM_LIMIT = 56 * 1024 * 1024
ADAMW_TILE_ELEMS = 128 * 1024
MESH = pl.DeviceIdType.MESH

NN = ((1,), (0,))
NT = ((1,), (1,))
TN = ((0,), (0,))


def _dot(a, b, dims, precision=None):
    return lax.dot_general(a, b, (dims, ((), ())), preferred_element_type=F32, precision=precision)


def _tile(n, pref, mult=128):
    if n <= pref:
        return n
    t = (pref // mult) * mult
    while t >= mult:
        if n % t == 0:
            return t
        t -= mult
    return n


def _params(sem, **kw):
    return pltpu.CompilerParams(dimension_semantics=sem, vmem_limit_bytes=VMEM_LIMIT, **kw)


def _mm(a, b, mode, outs, name, epilogue=None, extras=(), tm=1024, tn=1024, tk=512):
    if mode == "nn":
        (m, k), (k2, n) = a.shape, b.shape
    elif mode == "nt":
        (m, k), (n, k2) = a.shape, b.shape
    else:
        (k, m), (k2, n) = a.shape, b.shape
    assert k == k2, (a.shape, b.shape, mode)
    tm, tn, tk = _tile(m, tm), _tile(n, tn), _tile(k, tk)
    nk = k // tk
    if mode == "nn":
        a_spec = pl.BlockSpec((tm, tk), lambda i, j, kk: (i, kk))
        b_spec = pl.BlockSpec((tk, tn), lambda i, j, kk: (kk, j))
        dims = NN
    elif mode == "nt":
        a_spec = pl.BlockSpec((tm, tk), lambda i, j, kk: (i, kk))
        b_spec = pl.BlockSpec((tn, tk), lambda i, j, kk: (j, kk))
        dims = NT
    else:
        a_spec = pl.BlockSpec((tk, tm), lambda i, j, kk: (kk, i))
        b_spec = pl.BlockSpec((tk, tn), lambda i, j, kk: (kk, j))
        dims = TN
    tile_spec = pl.BlockSpec((tm, tn), lambda i, j, kk: (i, j))
    n_extra, n_out = len(extras), len(outs)
    if epilogue is None:
        epilogue = lambda acc: (acc,)

    def body(a_ref, b_ref, *rest):
        extra_refs = rest[:n_extra]
        out_refs = rest[n_extra:n_extra + n_out]
        acc = rest[-1]
        kk = pl.program_id(2)

        @pl.when(kk == 0)
        def _():
            acc[...] = jnp.zeros_like(acc)

        acc[...] += _dot(a_ref[...].astype(BF16), b_ref[...].astype(BF16), dims)

        @pl.when(kk == nk - 1)
        def _():
            res = epilogue(acc[...], *[e[...] for e in extra_refs])
            for o_ref, r in zip(out_refs, res):
                o_ref[...] = r.astype(o_ref.dtype)

    res = pl.pallas_call(
        body,
        name=name,
        grid=(m // tm, n // tn, nk),
        in_specs=[a_spec, b_spec] + [tile_spec] * n_extra,
        out_specs=[tile_spec] * n_out,
        out_shape=[jax.ShapeDtypeStruct((m, n), d) for d in outs],
        scratch_shapes=[pltpu.VMEM((tm, tn), F32)],
        compiler_params=_params(("parallel", "parallel", "arbitrary")),
    )(a, b, *extras)
    return res


def _rmsnorm_fwd(x, gain, name, rows=256):
    s, d = x.shape
    tr = _tile(s, rows, 8)

    def body(x_ref, g_ref, h_ref):
        xv = x_ref[...]
        r = lax.rsqrt(jnp.mean(xv * xv, axis=-1, keepdims=True) + EPS)
        h_ref[...] = (xv * r * g_ref[...]).astype(h_ref.dtype)

    return pl.pallas_call(
        body, name=name, grid=(s // tr,),
        in_specs=[pl.BlockSpec((tr, d), lambda i: (i, 0)), pl.BlockSpec((1, d), lambda i: (0, 0))],
        out_specs=pl.BlockSpec((tr, d), lambda i: (i, 0)),
        out_shape=jax.ShapeDtypeStruct((s, d), BF16),
        compiler_params=_params(("parallel",)),
    )(x, gain)


def _rmsnorm_bwd(x, gain, dh, dres, name, rows=256):
    s, d = x.shape
    tr = _tile(s, rows, 8)
    has_res = dres is not None

    def body(*refs):
        if has_res:
            x_ref, g_ref, dh_ref, dres_ref, dx_ref, dg_ref = refs
        else:
            x_ref, g_ref, dh_ref, dx_ref, dg_ref = refs
        i = pl.program_id(0)
        xv = x_ref[...]
        dy = dh_ref[...].astype(F32)
        r = lax.rsqrt(jnp.mean(xv * xv, axis=-1, keepdims=True) + EPS)
        xn = xv * r
        t = dy * g_ref[...]
        dx = r * (t - xn * jnp.mean(t * xn, axis=-1, keepdims=True))
        if has_res:
            dx = dx + dres_ref[...]
        dx_ref[...] = dx

        @pl.when(i == 0)
        def _():
            dg_ref[...] = jnp.zeros_like(dg_ref)

        dg_ref[...] += jnp.sum(dy * xn, axis=0, keepdims=True)

    row_spec = pl.BlockSpec((tr, d), lambda i: (i, 0))
    vec_spec = pl.BlockSpec((1, d), lambda i: (0, 0))
    args = [x, gain, dh] + ([dres] if has_res else [])
    return pl.pallas_call(
        body, name=name, grid=(s // tr,),
        in_specs=[row_spec, vec_spec, row_spec] + ([row_spec] if has_res else []),
        out_specs=[row_spec, vec_spec],
        out_shape=[jax.ShapeDtypeStruct((s, d), F32), jax.ShapeDtypeStruct((1, d), F32)],
        compiler_params=_params(("arbitrary",)),
    )(*args)


def _loss_head(x, gain, target, name, rows=256):
    s, d = x.shape
    tr = _tile(s, rows, 8)

    def body(x_ref, g_ref, t_ref, dx_ref, dg_ref, loss_ref):
        i = pl.program_id(0)
        xv = x_ref[...]
        g = g_ref[...]
        r = lax.rsqrt(jnp.mean(xv * xv, axis=-1, keepdims=True) + EPS)
        xn = xv * r
        err = xn * g - t_ref[...]
        dy = err * (1.0 / d)
        t = dy * g
        dx_ref[...] = r * (t - xn * jnp.mean(t * xn, axis=-1, keepdims=True))

        @pl.when(i == 0)
        def _():
            dg_ref[...] = jnp.zeros_like(dg_ref)
            loss_ref[...] = jnp.zeros_like(loss_ref)

        dg_ref[...] += jnp.sum(dy * xn, axis=0, keepdims=True)
        part = 0.5 * jnp.sum(jnp.mean(err * err, axis=-1, keepdims=True), axis=0, keepdims=True)
        loss_ref[...] += jnp.broadcast_to(part, loss_ref.shape)

    row_spec = pl.BlockSpec((tr, d), lambda i: (i, 0))
    vec_spec = pl.BlockSpec((1, d), lambda i: (0, 0))
    return pl.pallas_call(
        body, name=name, grid=(s // tr,),
        in_specs=[row_spec, vec_spec, row_spec],
        out_specs=[row_spec, vec_spec, pl.BlockSpec((1, 128), lambda i: (0, 0))],
        out_shape=[jax.ShapeDtypeStruct((s, d), F32), jax.ShapeDtypeStruct((1, d), F32),
                   jax.ShapeDtypeStruct((1, 128), F32)],
        compiler_params=_params(("arbitrary",)),
    )(x, gain, target)


def _log_sigmoid(z):
    return jnp.minimum(z, 0.0) - jnp.log1p(jnp.exp(-jnp.abs(z)))


def _gla_tri(reverse):
    row = lax.broadcasted_iota(jnp.int32, (GLA_CHUNK, GLA_CHUNK), 0)
    col = lax.broadcasted_iota(jnp.int32, (GLA_CHUNK, GLA_CHUNK), 1)
    return (col >= row) if reverse else (col <= row)


def _gla_rows(s):
    return _tile(s, 256, GLA_CHUNK)


def _gla_chunk_terms(q, k, g, tri, reverse):
    b = _dot(tri.astype(F32), g, NN, precision=lax.Precision.HIGHEST)
    bl = b[0:1] if reverse else b[GLA_CHUNK - 1:GLA_CHUNK]
    qd = q * jnp.exp(b)
    ki = k * jnp.exp(-b)
    ke = k * jnp.exp(bl - b)
    return b, bl, qd, ki, ke


def _gla_fwd(proj, lr, up_pad, bias, reverse, name):
    s = proj.shape[0]
    ts = _gla_rows(s)
    nblk, cpb = s // ts, ts // GLA_CHUNK
    scale = GLA_DK ** -0.5

    def blk(i):
        return (nblk - 1 - i) if reverse else i

    def body(q_ref, k_ref, v_ref, lr_ref, up_ref, b_ref, o_ref, st_ref, state):
        @pl.when(pl.program_id(1) == 0)
        def _():
            state[...] = jnp.zeros_like(state)

        tri = _gla_tri(reverse)
        z = _dot(lr_ref[...].astype(BF16), up_ref[...], NN) + b_ref[...]
        g_all = _log_sigmoid(z) * (1.0 / GLA_GATE_NORMALIZER)
        order = range(cpb - 1, -1, -1) if reverse else range(cpb)
        for c in order:
            sl = slice(c * GLA_CHUNK, (c + 1) * GLA_CHUNK)
            _, bl, qd, ki, ke = _gla_chunk_terms(q_ref[sl, :] * scale, k_ref[sl, :], g_all[sl, :], tri, reverse)
            qdb = qd.astype(BF16)
            a = jnp.where(tri, _dot(qdb, ki.astype(BF16), NT), 0.0)
            vb = v_ref[sl, :].astype(BF16)
            st = state[...]
            o_ref[sl, :] = _dot(a.astype(BF16), vb, NN) + _dot(qdb, st.astype(BF16), NT)
            st_ref[0, c] = st
            state[...] = st * jnp.exp(bl) + _dot(vb, ke.astype(BF16), TN)

    qk = lambda off: pl.BlockSpec((ts, GLA_DK), lambda h, i: (blk(i), off // GLA_DK + h))
    return pl.pallas_call(
        body, name=name, grid=(HEADS, nblk),
        in_specs=[qk(OFF_GQ), qk(OFF_GK),
                  pl.BlockSpec((ts, GLA_DV), lambda h, i: (blk(i), OFF_GV // GLA_DV + h)),
                  pl.BlockSpec((ts, LR_PAD), lambda h, i: (blk(i), 0)),
                  pl.BlockSpec((LR_PAD, GLA_DK), lambda h, i: (0, h)),
                  pl.BlockSpec((1, GLA_DK), lambda h, i: (0, h))],
        out_specs=[pl.BlockSpec((ts, GLA_DV), lambda h, i: (blk(i), h)),
                   pl.BlockSpec((1, cpb, GLA_DV, GLA_DK), lambda h, i: (h, blk(i), 0, 0))],
        out_shape=[jax.ShapeDtypeStruct((s, HEADS * GLA_DV), F32),
                   jax.ShapeDtypeStruct((HEADS, s // GLA_CHUNK, GLA_DV, GLA_DK), F32)],
        scratch_shapes=[pltpu.VMEM((GLA_DV, GLA_DK), F32)],
        compiler_params=_params(("parallel", "arbitrary")),
    )(proj, proj, proj, lr, up_pad, bias)


def _gla_bwd(proj, lr, up_pad, bias, states, d_o, prev, reverse, name):
    s = proj.shape[0]
    ts = _gla_rows(s)
    nblk, cpb = s // ts, ts // GLA_CHUNK
    scale = GLA_DK ** -0.5
    has_prev = prev is not None

    def blk(i):
        return i if reverse else (nblk - 1 - i)

    def body(*refs):
        q_ref, k_ref, v_ref, lr_ref, up_ref, b_ref, st_ref, do_ref = refs[:8]
        refs = refs[8:]
        if has_prev:
            pq_ref, pk_ref, pv_ref = refs[:3]
            refs = refs[3:]
        dq_ref, dk_ref, dv_ref, dz_ref, dstate = refs

        @pl.when(pl.program_id(1) == 0)
        def _():
            dstate[...] = jnp.zeros_like(dstate)

        tri = _gla_tri(reverse)
        tri_t = _gla_tri(not reverse)
        row = lax.broadcasted_iota(jnp.int32, (GLA_CHUNK, GLA_DK), 0)
        last_row = (row == 0) if reverse else (row == GLA_CHUNK - 1)
        z = _dot(lr_ref[...].astype(BF16), up_ref[...], NN) + b_ref[...]
        g_all = _log_sigmoid(z) * (1.0 / GLA_GATE_NORMALIZER)
        dgate = (1.0 / GLA_GATE_NORMALIZER) * (1.0 - jax.nn.sigmoid(z))
        order = range(cpb) if reverse else range(cpb - 1, -1, -1)
        for c in order:
            sl = slice(c * GLA_CHUNK, (c + 1) * GLA_CHUNK)
            b, bl, qd, ki, ke = _gla_chunk_terms(q_ref[sl, :] * scale, k_ref[sl, :], g_all[sl, :], tri, reverse)
            qdb, kib, keb = qd.astype(BF16), ki.astype(BF16), ke.astype(BF16)
            a = jnp.where(tri, _dot(qdb, kib, NT), 0.0)
            vb = v_ref[sl, :].astype(BF16)
            dob = do_ref[sl, :].astype(BF16)
            st = st_ref[0, c]
            dst = dstate[...]
            dstb = dst.astype(BF16)
            da = jnp.where(tri, _dot(dob, vb, NT), 0.0).astype(BF16)
            dv = _dot(a.astype(BF16), dob, TN) + _dot(keb, dstb, NT)
            dqd = _dot(da, kib, NN) + _dot(dob, st.astype(BF16), NN)
            dki = _dot(da, qdb, TN)
            dke = _dot(vb, dstb, NN)
            decay = jnp.exp(bl)
            dbl = decay * jnp.sum(dst * st, axis=0, keepdims=True) + jnp.sum(dke * ke, axis=0, keepdims=True)
            dstate[...] = dst * decay + _dot(dob, qdb, TN)
            db = dqd * qd - dki * ki - dke * ke + jnp.where(last_row, dbl, 0.0)
            dg = _dot(tri_t.astype(F32), db, NN, precision=lax.Precision.HIGHEST)
            dq = dqd * jnp.exp(b) * scale
            dk = dki * jnp.exp(-b) + dke * jnp.exp(bl - b)
            if has_prev:
                dq = dq + pq_ref[sl, :]
                dk = dk + pk_ref[sl, :]
                dv = dv + pv_ref[sl, :]
            dq_ref[sl, :] = dq
            dk_ref[sl, :] = dk
            dv_ref[sl, :] = dv
            dz_ref[sl, :] = dg * dgate[sl, :]

    qk = lambda off: pl.BlockSpec((ts, GLA_DK), lambda h, i: (blk(i), off // GLA_DK + h))
    hk = pl.BlockSpec((ts, GLA_DK), lambda h, i: (blk(i), h))
    hv = pl.BlockSpec((ts, GLA_DV), lambda h, i: (blk(i), h))
    in_specs = [qk(OFF_GQ), qk(OFF_GK),
                pl.BlockSpec((ts, GLA_DV), lambda h, i: (blk(i), OFF_GV // GLA_DV + h)),
                pl.BlockSpec((ts, LR_PAD), lambda h, i: (blk(i), 0)),
                pl.BlockSpec((LR_PAD, GLA_DK), lambda h, i: (0, h)),
                pl.BlockSpec((1, GLA_DK), lambda h, i: (0, h)),
                pl.BlockSpec((1, cpb, GLA_DV, GLA_DK), lambda h, i: (h, blk(i), 0, 0)),
                hv]
    args = [proj, proj, proj, lr, up_pad, bias, states, d_o]
    if has_prev:
        in_specs += [hk, hk, hv]
        args += list(prev)
    return pl.pallas_call(
        body, name=name, grid=(HEADS, nblk),
        in_specs=in_specs,
        out_specs=[hk, hk, hv, hk],
        out_shape=[jax.ShapeDtypeStruct((s, HEADS * GLA_DK), F32), jax.ShapeDtypeStruct((s, HEADS * GLA_DK), F32),
                   jax.ShapeDtypeStruct((s, HEADS * GLA_DV), F32), jax.ShapeDtypeStruct((s, HEADS * GLA_DK), F32)],
        scratch_shapes=[pltpu.VMEM((GLA_DV, GLA_DK), F32)],
        compiler_params=_params(("parallel", "arbitrary")),
    )(*args)


def _dil_tile(l):
    return _tile(l, 256, DIL_HALF)


def t5_bucket(rel):
    half = REL_BUCKETS // 2
    max_exact = half // 2
    ret = jnp.where(rel > 0, half, 0)
    n = jnp.abs(rel)
    nf = jnp.maximum(n, 1).astype(jnp.float32)
    large = max_exact + (jnp.log(nf / max_exact) / math.log(REL_MAX_DISTANCE / max_exact)
                         * (half - max_exact)).astype(jnp.int32)
    large = jnp.minimum(large, half - 1)
    return ret + jnp.where(n < max_exact, n, large)


def _band_buckets(dilation):
    w = DIL_HALF
    rel_sub = jnp.arange(3 * w)[None, :] - w - jnp.arange(w)[:, None]
    return t5_bucket(rel_sub * dilation)[0, :2 * w + 1].astype(jnp.int32)


def _band_offsets(tq):
    tk = tq + 2 * DIL_HALF
    da = lax.broadcasted_iota(jnp.int32, (tq, tk), 1) - lax.broadcasted_iota(jnp.int32, (tq, tk), 0)
    db = (lax.broadcasted_iota(jnp.int32, (tk, tq), 1) - lax.broadcasted_iota(jnp.int32, (tk, tq), 0)
          + 2 * DIL_HALF)
    return jnp.clip(da, 0, 2 * DIL_HALF), jnp.clip(db, 0, 2 * DIL_HALF)


def _bias_tiles(band, table, tq, name):
    tk = tq + 2 * DIL_HALF

    def body(band_ref, t_ref, oa_ref, ob_ref):
        h = pl.program_id(0)
        off_a, off_b = _band_offsets(tq)

        def step(t, carry):
            bkt_a, bkt_b = carry
            bkt = band_ref[t]
            return jnp.where(off_a == t, bkt, bkt_a), jnp.where(off_b == t, bkt, bkt_b)

        bkt_a, bkt_b = lax.fori_loop(0, 2 * DIL_HALF + 1, step,
                                     (jnp.zeros((tq, tk), jnp.int32), jnp.zeros((tk, tq), jnp.int32)))
        acc_a, acc_b = jnp.zeros((tq, tk), F32), jnp.zeros((tk, tq), F32)
        for bkt in range(REL_BUCKETS):
            val = t_ref[bkt, h]
            acc_a = jnp.where(bkt_a == bkt, val, acc_a)
            acc_b = jnp.where(bkt_b == bkt, val, acc_b)
        oa_ref[0] = acc_a
        ob_ref[0] = acc_b

    smem = pl.BlockSpec(memory_space=pltpu.SMEM)
    return pl.pallas_call(
        body, name=name, grid=(HEADS,),
        in_specs=[smem, smem],
        out_specs=[pl.BlockSpec((1, tq, tk), lambda h: (h, 0, 0)), pl.BlockSpec((1, tk, tq), lambda h: (h, 0, 0))],
        out_shape=[jax.ShapeDtypeStruct((HEADS, tq, tk), F32), jax.ShapeDtypeStruct((HEADS, tk, tq), F32)],
        compiler_params=_params(("arbitrary",)),
    )(band, table)


def _bias_grad(bands, dbias_list, name):
    n = len(bands)

    def body(*refs):
        band_refs, db_refs, out_ref = refs[:n], refs[n:2 * n], refs[2 * n]
        row = lax.broadcasted_iota(jnp.int32, (REL_BUCKETS, 128), 0)
        lane = lax.broadcasted_iota(jnp.int32, (REL_BUCKETS, 128), 1)
        acc = jnp.zeros((REL_BUCKETS, 128), F32)
        for band_ref, d_ref in zip(band_refs, db_refs):
            off_a, _ = _band_offsets(d_ref.shape[1])
            for h in range(HEADS):
                def step(t, acc, band_ref=band_ref, d_ref=d_ref, h=h, off_a=off_a):
                    tot = jnp.sum(jnp.where(off_a == t, d_ref[h], 0.0))
                    return acc + jnp.where((row == band_ref[t]) & (lane == h), tot, 0.0)

                acc = lax.fori_loop(0, 2 * DIL_HALF + 1, step, acc)
        out_ref[...] = acc

    vm = pl.BlockSpec(memory_space=pltpu.VMEM)
    smem = pl.BlockSpec(memory_space=pltpu.SMEM)
    return pl.pallas_call(
        body, name=name,
        in_specs=[smem] * n + [vm] * n, out_specs=vm,
        out_shape=jax.ShapeDtypeStruct((REL_BUCKETS, 128), F32),
        compiler_params=pltpu.CompilerParams(vmem_limit_bytes=VMEM_LIMIT),
    )(*bands, *dbias_list)


def _dil_specs(l, tq, dilation, width, off, by_head_first):
    nb64 = l // DIL_HALF
    per = tq // DIL_HALF

    def col(h, r):
        return (r * width + off) // HEAD_DIM + h

    def wrap(f):
        if by_head_first:
            return lambda h, r, n: f(h, r, n)
        return lambda r, h, n: f(h, r, n)

    prev = pl.BlockSpec((DIL_HALF, HEAD_DIM), wrap(lambda h, r, n: (jnp.maximum(n * per - 1, 0), col(h, r))))
    main = pl.BlockSpec((tq, HEAD_DIM), wrap(lambda h, r, n: (n, col(h, r))))
    nxt = pl.BlockSpec((DIL_HALF, HEAD_DIM), wrap(lambda h, r, n: (jnp.minimum((n + 1) * per, nb64 - 1), col(h, r))))
    return prev, main, nxt


def _dil_fwd(proj, bias_a, dilation, name):
    s = proj.shape[0]
    l = s // dilation
    tq = _dil_tile(l)
    tk = tq + 2 * DIL_HALF
    nq = l // tq
    scale = HEAD_DIM ** -0.5
    view = proj.reshape(l, dilation * MAIN_WIDTH)

    def body(q_ref, kp_ref, km_ref, kn_ref, vp_ref, vm_ref, vn_ref, b_ref, o_ref, lse_ref):
        n = pl.program_id(2)
        q = q_ref[...].astype(BF16)
        kc = jnp.concatenate([kp_ref[...], km_ref[...], kn_ref[...]], axis=0).astype(BF16)
        vc = jnp.concatenate([vp_ref[...], vm_ref[...], vn_ref[...]], axis=0).astype(BF16)
        sc = _dot(q, kc, NT) * scale + b_ref[0]
        qpos = n * tq + lax.broadcasted_iota(jnp.int32, (tq, tk), 0)
        kpos = n * tq - DIL_HALF + lax.broadcasted_iota(jnp.int32, (tq, tk), 1)
        mask = (jnp.abs(kpos - qpos) <= DIL_HALF) & (kpos >= 0) & (kpos < l)
        sc = jnp.where(mask, sc, NEG_INF)
        m = jnp.max(sc, axis=-1, keepdims=True)
        p = jnp.exp(sc - m)
        den = jnp.sum(p, axis=-1, keepdims=True)
        o_ref[...] = _dot(p.astype(BF16), vc, NN) / den
        lse_ref[...] = jnp.broadcast_to(m + jnp.log(den), (tq, HEAD_DIM))

    _, q_main, _ = _dil_specs(l, tq, dilation, MAIN_WIDTH, OFF_DQ, False)
    k_specs = _dil_specs(l, tq, dilation, MAIN_WIDTH, OFF_DK, False)
    v_specs = _dil_specs(l, tq, dilation, MAIN_WIDTH, OFF_DV, False)
    _, o_main, _ = _dil_specs(l, tq, dilation, HEADS * HEAD_DIM, 0, False)
    o, lse = pl.pallas_call(
        body, name=name, grid=(dilation, HEADS, nq),
        in_specs=[q_main, *k_specs, *v_specs, pl.BlockSpec((1, tq, tk), lambda r, h, n: (h, 0, 0))],
        out_specs=[o_main, o_main],
        out_shape=[jax.ShapeDtypeStruct((l, dilation * HEADS * HEAD_DIM), F32)] * 2,
        compiler_params=_params(("parallel", "parallel", "parallel")),
    )(view, view, view, view, view, view, view, bias_a)
    return o.reshape(s, HEADS * HEAD_DIM), lse.reshape(s, HEADS * HEAD_DIM)


def _dil_bwd_q(proj, d_o, lse, delta, bias_a, prev_dq, dilation, name):
    s = proj.shape[0]
    l = s // dilation
    tq = _dil_tile(l)
    tk = tq + 2 * DIL_HALF
    nq = l // tq
    scale = HEAD_DIM ** -0.5
    w4 = HEADS * HEAD_DIM
    view = proj.reshape(l, dilation * MAIN_WIDTH)
    small = lambda t: t.reshape(l, dilation * w4)
    has_prev = prev_dq is not None

    def body(*refs):
        q_ref, kp_ref, km_ref, kn_ref, vp_ref, vm_ref, vn_ref, b_ref, do_ref, lse_ref, dl_ref = refs[:11]
        refs = refs[11:]
        if has_prev:
            pq_ref, refs = refs[0], refs[1:]
        dq_ref, db_ref = refs
        r, n = pl.program_id(1), pl.program_id(2)
        q = q_ref[...].astype(BF16)
        kc = jnp.concatenate([kp_ref[...], km_ref[...], kn_ref[...]], axis=0).astype(BF16)
        vc = jnp.concatenate([vp_ref[...], vm_ref[...], vn_ref[...]], axis=0).astype(BF16)
        sc = _dot(q, kc, NT) * scale + b_ref[0]
        qpos = n * tq + lax.broadcasted_iota(jnp.int32, (tq, tk), 0)
        kpos = n * tq - DIL_HALF + lax.broadcasted_iota(jnp.int32, (tq, tk), 1)
        mask = (jnp.abs(kpos - qpos) <= DIL_HALF) & (kpos >= 0) & (kpos < l)
        p = jnp.where(mask, jnp.exp(sc - lse_ref[:, 0:1]), 0.0)
        dp = _dot(do_ref[...].astype(BF16), vc, NT)
        ds = p * (dp - dl_ref[:, 0:1])
        dq = _dot(ds.astype(BF16), kc, NN) * scale
        if has_prev:
            dq = dq + pq_ref[...]
        dq_ref[...] = dq

        @pl.when((r == 0) & (n == 0))
        def _():
            db_ref[...] = jnp.zeros_like(db_ref)

        db_ref[0] += ds

    _, q_main, _ = _dil_specs(l, tq, dilation, MAIN_WIDTH, OFF_DQ, True)
    k_specs = _dil_specs(l, tq, dilation, MAIN_WIDTH, OFF_DK, True)
    v_specs = _dil_specs(l, tq, dilation, MAIN_WIDTH, OFF_DV, True)
    _, o_main, _ = _dil_specs(l, tq, dilation, w4, 0, True)
    bias_spec = pl.BlockSpec((1, tq, tk), lambda h, r, n: (h, 0, 0))
    in_specs = [q_main, *k_specs, *v_specs, bias_spec, o_main, o_main, o_main] + ([o_main] if has_prev else [])
    args = [view] * 7 + [bias_a, small(d_o), small(lse), small(delta)] + ([small(prev_dq)] if has_prev else [])
    dq, dbias = pl.pallas_call(
        body, name=name, grid=(HEADS, dilation, nq),
        in_specs=in_specs,
        out_specs=[o_main, bias_spec],
        out_shape=[jax.ShapeDtypeStruct((l, dilation * w4), F32), jax.ShapeDtypeStruct((HEADS, tq, tk), F32)],
        compiler_params=_params(("arbitrary", "arbitrary", "arbitrary")),
    )(*args)
    return dq.reshape(s, w4), dbias


def _dil_bwd_kv(proj, d_o, lse, delta, bias_b, prev, dilation, name):
    s = proj.shape[0]
    l = s // dilation
    tq = _dil_tile(l)
    tw = tq + 2 * DIL_HALF
    nq = l // tq
    scale = HEAD_DIM ** -0.5
    w4 = HEADS * HEAD_DIM
    view = proj.reshape(l, dilation * MAIN_WIDTH)
    small = lambda t: t.reshape(l, dilation * w4)
    has_prev = prev is not None

    def body(*refs):
        (qp_ref, qm_ref, qn_ref, k_ref, v_ref, b_ref, dop_ref, dom_ref, don_ref,
         lp_ref, lm_ref, ln_ref, dp_ref, dm_ref, dn_ref) = refs[:15]
        refs = refs[15:]
        if has_prev:
            pk_ref, pv_ref = refs[:2]
            refs = refs[2:]
        dk_ref, dv_ref = refs
        n = pl.program_id(2)
        cat = lambda a, b_, c: jnp.concatenate([a[...], b_[...], c[...]], axis=0)
        qc = cat(qp_ref, qm_ref, qn_ref).astype(BF16)
        doc = cat(dop_ref, dom_ref, don_ref).astype(BF16)
        lsec = cat(lp_ref, lm_ref, ln_ref)[:, 0:1]
        dlc = cat(dp_ref, dm_ref, dn_ref)[:, 0:1]
        kb = k_ref[...].astype(BF16)
        vb = v_ref[...].astype(BF16)
        sc = _dot(qc, kb, NT) * scale + b_ref[0]
        qpos = n * tq - DIL_HALF + lax.broadcasted_iota(jnp.int32, (tw, tq), 0)
        kpos = n * tq + lax.broadcasted_iota(jnp.int32, (tw, tq), 1)
        mask = (jnp.abs(kpos - qpos) <= DIL_HALF) & (qpos >= 0) & (qpos < l)
        p = jnp.where(mask, jnp.exp(sc - lsec), 0.0)
        dv = _dot(p.astype(BF16), doc, TN)
        dp = _dot(doc, vb, NT)
        ds = p * (dp - dlc)
        dk = _dot(ds.astype(BF16), qc, TN) * scale
        if has_prev:
            dk = dk + pk_ref[...]
            dv = dv + pv_ref[...]
        dk_ref[...] = dk
        dv_ref[...] = dv

    q_specs = _dil_specs(l, tq, dilation, MAIN_WIDTH, OFF_DQ, False)
    _, k_main, _ = _dil_specs(l, tq, dilation, MAIN_WIDTH, OFF_DK, False)
    _, v_main, _ = _dil_specs(l, tq, dilation, MAIN_WIDTH, OFF_DV, False)
    o_specs = _dil_specs(l, tq, dilation, w4, 0, False)
    o_main = o_specs[1]
    in_specs = [*q_specs, k_main, v_main, pl.BlockSpec((1, tw, tq), lambda r, h, n: (h, 0, 0)),
                *o_specs, *o_specs, *o_specs] + ([o_main, o_main] if has_prev else [])
    args = ([view] * 5 + [bias_b] + [small(d_o)] * 3 + [small(lse)] * 3 + [small(delta)] * 3
            + ([small(prev[0]), small(prev[1])] if has_prev else []))
    dk, dv = pl.pallas_call(
        body, name=name, grid=(dilation, HEADS, nq),
        in_specs=in_specs,
        out_specs=[o_main, o_main],
        out_shape=[jax.ShapeDtypeStruct((l, dilation * w4), F32)] * 2,
        compiler_params=_params(("parallel", "parallel", "parallel")),
    )(*args)
    return dk.reshape(s, w4), dv.reshape(s, w4)


def _mem_fwd(proj, kv, name):
    s = proj.shape[0]
    mlen = kv.shape[0]
    tq = _tile(s, 512, 8)
    scale = HEAD_DIM ** -0.5
    w4 = HEADS * HEAD_DIM

    def body(q_ref, k_ref, v_ref, o_ref, lse_ref):
        sc = _dot(q_ref[...].astype(BF16), k_ref[...].astype(BF16), NT) * scale
        m = jnp.max(sc, axis=-1, keepdims=True)
        e = jnp.exp(sc - m)
        den = jnp.sum(e, axis=-1, keepdims=True)
        o_ref[...] = _dot((e / den).astype(BF16), v_ref[...].astype(BF16), NN)
        lse_ref[...] = jnp.broadcast_to(m + jnp.log(den), (tq, HEAD_DIM))

    o_spec = pl.BlockSpec((tq, HEAD_DIM), lambda h, n: (n, h))
    return pl.pallas_call(
        body, name=name, grid=(HEADS, s // tq),
        in_specs=[pl.BlockSpec((tq, HEAD_DIM), lambda h, n: (n, OFF_MQ // HEAD_DIM + h)),
                  pl.BlockSpec((mlen, HEAD_DIM), lambda h, n: (0, h)),
                  pl.BlockSpec((mlen, HEAD_DIM), lambda h, n: (0, HEADS + h))],
        out_specs=[o_spec, o_spec],
        out_shape=[jax.ShapeDtypeStruct((s, w4), F32)] * 2,
        compiler_params=_params(("parallel", "parallel")),
    )(proj, kv, kv)


def _mem_bwd(proj, kv, d_o, lse, delta, name):
    s = proj.shape[0]
    mlen = kv.shape[0]
    tq = _tile(s, 512, 8)
    scale = HEAD_DIM ** -0.5
    w4 = HEADS * HEAD_DIM

    def body(q_ref, k_ref, v_ref, do_ref, lse_ref, dl_ref, dq_ref, dk_ref, dv_ref):
        qb = q_ref[...].astype(BF16)
        kb = k_ref[...].astype(BF16)
        dob = do_ref[...].astype(BF16)
        sc = _dot(qb, kb, NT) * scale
        p = jnp.exp(sc - lse_ref[:, 0:1])
        dp = _dot(dob, v_ref[...].astype(BF16), NT)
        ds = (p * (dp - dl_ref[:, 0:1])).astype(BF16)
        dq_ref[...] = _dot(ds, kb, NN) * scale

        @pl.when(pl.program_id(1) == 0)
        def _():
            dk_ref[...] = jnp.zeros_like(dk_ref)
            dv_ref[...] = jnp.zeros_like(dv_ref)

        dk_ref[...] += _dot(ds, qb, TN) * scale
        dv_ref[...] += _dot(p.astype(BF16), dob, TN)

    o_spec = pl.BlockSpec((tq, HEAD_DIM), lambda h, n: (n, h))
    k_spec = pl.BlockSpec((mlen, HEAD_DIM), lambda h, n: (0, h))
    v_spec = pl.BlockSpec((mlen, HEAD_DIM), lambda h, n: (0, HEADS + h))
    dq, dkv, dkv2 = pl.pallas_call(
        body, name=name, grid=(HEADS, s // tq),
        in_specs=[pl.BlockSpec((tq, HEAD_DIM), lambda h, n: (n, OFF_MQ // HEAD_DIM + h)),
                  k_spec, v_spec, o_spec, o_spec, o_spec],
        out_specs=[o_spec, k_spec, k_spec],
        out_shape=[jax.ShapeDtypeStruct((s, w4), F32), jax.ShapeDtypeStruct((mlen, w4), F32),
                   jax.ShapeDtypeStruct((mlen, w4), F32)],
        compiler_params=_params(("parallel", "arbitrary")),
    )(proj, kv, kv, d_o, lse, delta)
    return dq, dkv, dkv2


def _head_norm(o, gain, width):
    outs, xns = [], []
    for h in range(HEADS):
        oh = o[:, h * width:(h + 1) * width]
        r = lax.rsqrt(jnp.mean(oh * oh, axis=-1, keepdims=True) + EPS)
        xn = oh * r
        xns.append(xn)
        outs.append(xn * gain[:, h * width:(h + 1) * width])
    return outs, xns


def _head_norm_bwd(o, gain, dy, width):
    dos, dgs = [], []
    for h in range(HEADS):
        sl = slice(h * width, (h + 1) * width)
        oh = o[:, sl]
        r = lax.rsqrt(jnp.mean(oh * oh, axis=-1, keepdims=True) + EPS)
        xn = oh * r
        t = dy[:, sl] * gain[:, sl]
        dos.append(r * (t - xn * jnp.mean(t * xn, axis=-1, keepdims=True)))
        dgs.append(jnp.sum(dy[:, sl] * xn, axis=0, keepdims=True))
    return dos, dgs


def _mix_fwd(o_f, o_b, proj, dil_os, dil_lses, mem_o, g_gla, g_dil, g_mem, name):
    s = o_f.shape[0]
    tr = _tile(s, 256, 8)
    w4 = HEADS * HEAD_DIM
    wv = HEADS * GLA_DV

    def body(of_ref, ob_ref, r_ref, o1_ref, o2_ref, o3_ref, l1_ref, l2_ref, l3_ref, mo_ref,
             gg_ref, gd_ref, gm_ref, mix_ref, do_ref, dl_ref):
        o = of_ref[...] + ob_ref[...]
        normed, _ = _head_norm(o, gg_ref[...], GLA_DV)
        rv = r_ref[...]
        gate = rv * jax.nn.sigmoid(rv)
        for h in range(HEADS):
            mix_ref[:, h * GLA_DV:(h + 1) * GLA_DV] = (normed[h] * gate[:, h * GLA_DV:(h + 1) * GLA_DV]).astype(BF16)
        l1, l2, l3 = l1_ref[...], l2_ref[...], l3_ref[...]
        m = jnp.maximum(jnp.maximum(l1, l2), l3)
        e1, e2, e3 = jnp.exp(l1 - m), jnp.exp(l2 - m), jnp.exp(l3 - m)
        den = e1 + e2 + e3
        od = (e1 * o1_ref[...] + e2 * o2_ref[...] + e3 * o3_ref[...]) / den
        do_ref[...] = od
        dl_ref[...] = m + jnp.log(den)
        nd, _ = _head_norm(od, gd_ref[...], HEAD_DIM)
        nm, _ = _head_norm(mo_ref[...], gm_ref[...], HEAD_DIM)
        for h in range(HEADS):
            mix_ref[:, wv + h * HEAD_DIM:wv + (h + 1) * HEAD_DIM] = nd[h].astype(BF16)
            mix_ref[:, wv + w4 + h * HEAD_DIM:wv + w4 + (h + 1) * HEAD_DIM] = nm[h].astype(BF16)

    rows = lambda w, c=0: pl.BlockSpec((tr, w), lambda i: (i, c))
    vec = lambda w: pl.BlockSpec((1, w), lambda i: (0, 0))
    return pl.pallas_call(
        body, name=name, grid=(s // tr,),
        in_specs=[rows(wv), rows(wv), rows(wv, OFF_GR // wv)] + [rows(w4)] * 7 + [vec(wv), vec(w4), vec(w4)],
        out_specs=[rows(wv + 2 * w4), rows(w4), rows(w4)],
        out_shape=[jax.ShapeDtypeStruct((s, wv + 2 * w4), BF16), jax.ShapeDtypeStruct((s, w4), F32),
                   jax.ShapeDtypeStruct((s, w4), F32)],
        compiler_params=_params(("parallel",)),
    )(o_f, o_b, proj, *dil_os, *dil_lses, mem_o, g_gla, g_dil, g_mem)


def _mix_bwd(dmixed, o_f, o_b, proj, dil_o, mem_o, g_gla, g_dil, g_mem, name):
    s = o_f.shape[0]
    tr = _tile(s, 256, 8)
    w4 = HEADS * HEAD_DIM
    wv = HEADS * GLA_DV

    def body(dm_ref, of_ref, ob_ref, r_ref, od_ref, mo_ref, gg_ref, gd_ref, gm_ref,
             dog_ref, dr_ref, dod_ref, dld_ref, dom_ref, dlm_ref, dgg_ref, dgd_ref, dgm_ref):
        i = pl.program_id(0)

        @pl.when(i == 0)
        def _():
            dgg_ref[...] = jnp.zeros_like(dgg_ref)
            dgd_ref[...] = jnp.zeros_like(dgd_ref)
            dgm_ref[...] = jnp.zeros_like(dgm_ref)

        dm = dm_ref[...]
        o = of_ref[...] + ob_ref[...]
        normed, _ = _head_norm(o, gg_ref[...], GLA_DV)
        rv = r_ref[...]
        sg = jax.nn.sigmoid(rv)
        gate = rv * sg
        dgate = sg * (1.0 + rv * (1.0 - sg))
        d_gla = dm[:, :wv]
        for h in range(HEADS):
            sl = slice(h * GLA_DV, (h + 1) * GLA_DV)
            dr_ref[:, sl] = d_gla[:, sl] * normed[h] * dgate[:, sl]
        dos, dgs = _head_norm_bwd(o, gg_ref[...], d_gla * gate, GLA_DV)
        for h in range(HEADS):
            sl = slice(h * GLA_DV, (h + 1) * GLA_DV)
            dog_ref[:, sl] = dos[h]
            dgg_ref[:, sl] += dgs[h]
        for src_ref, g_ref, off, do_out, dl_out, dg_out in (
                (od_ref, gd_ref, wv, dod_ref, dld_ref, dgd_ref),
                (mo_ref, gm_ref, wv + w4, dom_ref, dlm_ref, dgm_ref)):
            src = src_ref[...]
            dos, dgs = _head_norm_bwd(src, g_ref[...], dm[:, off:off + w4], HEAD_DIM)
            for h in range(HEADS):
                sl = slice(h * HEAD_DIM, (h + 1) * HEAD_DIM)
                do_out[:, sl] = dos[h]
                dl_out[:, sl] = jnp.broadcast_to(
                    jnp.sum(dos[h] * src[:, sl], axis=-1, keepdims=True), (tr, HEAD_DIM))
                dg_out[:, sl] += dgs[h]

    rows = lambda w, c=0: pl.BlockSpec((tr, w), lambda i: (i, c))
    vec = lambda w: pl.BlockSpec((1, w), lambda i: (0, 0))
    sds = lambda w: jax.ShapeDtypeStruct((s, w), F32)
    vds = lambda w: jax.ShapeDtypeStruct((1, w), F32)
    return pl.pallas_call(
        body, name=name, grid=(s // tr,),
        in_specs=[rows(wv + 2 * w4), rows(wv), rows(wv), rows(wv, OFF_GR // wv), rows(w4), rows(w4),
                  vec(wv), vec(w4), vec(w4)],
        out_specs=[rows(wv), rows(wv), rows(w4), rows(w4), rows(w4), rows(w4), vec(wv), vec(w4), vec(w4)],
        out_shape=[sds(wv), sds(wv), sds(w4), sds(w4), sds(w4), sds(w4), vds(wv), vds(w4), vds(w4)],
        compiler_params=_params(("arbitrary",)),
    )(dmixed, o_f, o_b, proj, dil_o, mem_o, g_gla, g_dil, g_mem)


def _colsum(x, name, rows=512):
    s, w = x.shape
    tr = _tile(s, rows, 8)

    def body(x_ref, o_ref):
        @pl.when(pl.program_id(0) == 0)
        def _():
            o_ref[...] = jnp.zeros_like(o_ref)

        o_ref[...] += jnp.sum(x_ref[...], axis=0, keepdims=True)

    return pl.pallas_call(
        body, name=name, grid=(s // tr,),
        in_specs=[pl.BlockSpec((tr, w), lambda i: (i, 0))],
        out_specs=pl.BlockSpec((1, w), lambda i: (0, 0)),
        out_shape=jax.ShapeDtypeStruct((1, w), F32),
        compiler_params=_params(("arbitrary",)),
    )(x)


def _peer(k):
    x, y, c = lax.axis_index("x"), lax.axis_index("y"), lax.axis_index("c")
    kx, ky, kc = (k >> 2) & 1, (k >> 1) & 1, k & 1
    return (x ^ kx if kx else x, y ^ ky if ky else y, c ^ kc if kc else c)


def _my_index():
    return 4 * lax.axis_index("x") + 2 * lax.axis_index("y") + lax.axis_index("c")


def _exchange(xs, scatter, name):
    n = len(xs)
    out_shapes = [jax.ShapeDtypeStruct(x.shape if scatter else (N_DEV,) + x.shape, x.dtype) for x in xs]

    def body(*refs):
        in_refs, out_refs = refs[:n], refs[n:2 * n]
        send_sems, recv_sems, local_sems = refs[2 * n:]
        me = _my_index()
        copies = []
        for i in range(n):
            src_own = in_refs[i].at[me] if scatter else in_refs[i]
            own = pltpu.make_async_copy(src_own, out_refs[i].at[me], local_sems.at[i])
            own.start()
            copies.append(own)
        remote = []
        for i in range(n):
            for k in range(1, N_DEV):
                peer = _peer(k)
                peer_idx = 4 * peer[0] + 2 * peer[1] + peer[2]
                src = in_refs[i].at[peer_idx] if scatter else in_refs[i]
                cp = pltpu.make_async_remote_copy(
                    src_ref=src, dst_ref=out_refs[i].at[me],
                    send_sem=send_sems.at[i * (N_DEV - 1) + k - 1], recv_sem=recv_sems.at[i * (N_DEV - 1) + k - 1],
                    device_id=peer, device_id_type=MESH)
                cp.start()
                remote.append((cp, i, k, peer_idx))
        for cp, i, k, peer_idx in remote:
            src = in_refs[i].at[peer_idx] if scatter else in_refs[i]
            pltpu.make_async_remote_copy(
                src_ref=src, dst_ref=out_refs[i].at[peer_idx],
                send_sem=send_sems.at[i * (N_DEV - 1) + k - 1], recv_sem=recv_sems.at[i * (N_DEV - 1) + k - 1],
                device_id=_peer(k), device_id_type=MESH).wait_recv()
        for cp, _, _, _ in remote:
            cp.wait_send()
        for own in copies:
            own.wait()

    any_spec = pl.BlockSpec(memory_space=pl.ANY)
    return pl.pallas_call(
        body, name=name,
        in_specs=[any_spec] * n, out_specs=[any_spec] * n, out_shape=out_shapes,
        scratch_shapes=[pltpu.SemaphoreType.DMA((n * (N_DEV - 1),)), pltpu.SemaphoreType.DMA((n * (N_DEV - 1),)),
                        pltpu.SemaphoreType.DMA((n,))],
        compiler_params=pltpu.CompilerParams(has_side_effects=True),
    )(*xs)


def _adamw(parts, w, m, v, name, rows=256):
    r, c = w.shape
    tr = _tile(r, max(8, min(rows, ADAMW_TILE_ELEMS // c)), 8)
    c1 = 1.0 - ADAM_B1 ** ADAM_STEP
    c2 = 1.0 - ADAM_B2 ** ADAM_STEP

    def body(p_ref, w_ref, m_ref, v_ref, g_ref, d_ref, nm_ref, nv_ref):
        g = p_ref[0].astype(F32)
        for d in range(1, N_DEV):
            g = g + p_ref[d].astype(F32)
        nm = ADAM_B1 * m_ref[...] + (1.0 - ADAM_B1) * g
        nv = ADAM_B2 * v_ref[...] + (1.0 - ADAM_B2) * (g * g)
        m_hat = nm / c1
        v_hat = nv / c2
        g_ref[...] = g
        d_ref[...] = -ADAM_LR * (m_hat / (jnp.sqrt(v_hat) + ADAM_EPS) + ADAM_WD * w_ref[...])
        nm_ref[...] = nm
        nv_ref[...] = nv

    spec = pl.BlockSpec((tr, c), lambda i: (i, 0))
    return pl.pallas_call(
        body, name=name, grid=(r // tr,),
        in_specs=[pl.BlockSpec((N_DEV, tr, c), lambda i: (0, i, 0)), spec, spec, spec],
        out_specs=[spec] * 4,
        out_shape=[jax.ShapeDtypeStruct((r, c), F32)] * 4,
        compiler_params=_params(("parallel",)),
    )(parts, w, m, v)


SMALL = ("norm_mix", "gla_gate_bias_fwd", "gla_gate_bias_bwd", "gla_norm", "rel_bias", "dil_norm", "mem_norm",
         "mem_out_norm", "norm_mlp", "norm_final")


def _pack(arrs, rows):
    flat = jnp.concatenate([a.reshape(-1) for a in arrs])
    return jnp.pad(flat, (0, rows * 128 - flat.shape[0])).reshape(rows, 128)


def _unpack(buf, shapes):
    flat = buf.reshape(-1)
    out, off = [], 0
    for shp in shapes:
        n = int(np.prod(shp))
        out.append(flat[off:off + n].reshape(shp))
        off += n
    return out


def _split_in(w):
    main = jnp.concatenate([w[..., :3072], w[..., 3104:]], axis=-1)
    lr = w[..., 3072:3104]
    pad = [(0, 0)] * (w.ndim - 1) + [(0, LR_PAD - 2 * GLA_RANK)]
    return main, jnp.pad(lr, pad)


def _join_in(main, lr):
    return jnp.concatenate([main[..., :3072], lr[..., :2 * GLA_RANK], main[..., 3072:]], axis=-1)


def kernel(x, mem, norm_mix, w_in, gla_gate_up_fwd, gla_gate_bias_fwd, gla_gate_up_bwd, gla_gate_bias_bwd, gla_norm, rel_bias, dil_norm, mem_norm, w_mem_kv, mem_out_norm, w_out, norm_mlp, w_up, w_down, norm_final, loss_target, m_norm_mix, m_w_in, m_gla_gate_up_fwd, m_gla_gate_bias_fwd, m_gla_gate_up_bwd, m_gla_gate_bias_bwd, m_gla_norm, m_rel_bias, m_dil_norm, m_mem_norm, m_w_mem_kv, m_mem_out_norm, m_w_out, m_norm_mlp, m_w_up, m_w_down, m_norm_final, v_norm_mix, v_w_in, v_gla_gate_up_fwd, v_gla_gate_bias_fwd, v_gla_gate_up_bwd, v_gla_gate_bias_bwd, v_gla_norm, v_rel_bias, v_dil_norm, v_mem_norm, v_w_mem_kv, v_mem_out_norm, v_w_out, v_norm_mlp, v_w_up, v_w_down, v_norm_final):
    weights = dict(norm_mix=norm_mix, w_in=w_in, gla_gate_up_fwd=gla_gate_up_fwd, gla_gate_bias_fwd=gla_gate_bias_fwd,
                   gla_gate_up_bwd=gla_gate_up_bwd, gla_gate_bias_bwd=gla_gate_bias_bwd, gla_norm=gla_norm,
                   rel_bias=rel_bias, dil_norm=dil_norm, mem_norm=mem_norm, w_mem_kv=w_mem_kv,
                   mem_out_norm=mem_out_norm, w_out=w_out, norm_mlp=norm_mlp, w_up=w_up, w_down=w_down,
                   norm_final=norm_final)
    mom1 = dict(norm_mix=m_norm_mix, w_in=m_w_in, gla_gate_up_fwd=m_gla_gate_up_fwd,
                gla_gate_bias_fwd=m_gla_gate_bias_fwd, gla_gate_up_bwd=m_gla_gate_up_bwd,
                gla_gate_bias_bwd=m_gla_gate_bias_bwd, gla_norm=m_gla_norm, rel_bias=m_rel_bias, dil_norm=m_dil_norm,
                mem_norm=m_mem_norm, w_mem_kv=m_w_mem_kv, mem_out_norm=m_mem_out_norm, w_out=m_w_out,
                norm_mlp=m_norm_mlp, w_up=m_w_up, w_down=m_w_down, norm_final=m_norm_final)
    mom2 = dict(norm_mix=v_norm_mix, w_in=v_w_in, gla_gate_up_fwd=v_gla_gate_up_fwd,
                gla_gate_bias_fwd=v_gla_gate_bias_fwd, gla_gate_up_bwd=v_gla_gate_up_bwd,
                gla_gate_bias_bwd=v_gla_gate_bias_bwd, gla_norm=v_gla_norm, rel_bias=v_rel_bias, dil_norm=v_dil_norm,
                mem_norm=v_mem_norm, w_mem_kv=v_w_mem_kv, mem_out_norm=v_mem_out_norm, w_out=v_w_out,
                norm_mlp=v_norm_mlp, w_up=v_w_up, w_down=v_w_down, norm_final=v_norm_final)

    s, d = x.shape[1], x.shape[2]
    xs = x.reshape(s, d)
    mems = mem.reshape(mem.shape[1], d)
    target = loss_target.reshape(s, d)
    me = _my_index()
    n_layers = w_in.shape[0]
    gate_w = gla_gate_up_fwd.shape[2]

    to_bf = lambda t: t.astype(BF16)
    g_in, g_upf, g_upb, g_kv, g_out, g_up, g_down = _exchange(
        [to_bf(w_in), to_bf(gla_gate_up_fwd), to_bf(gla_gate_up_bwd), to_bf(w_mem_kv), to_bf(w_out), to_bf(w_up),
         to_bf(w_down)], False, "ag_weights")
    cols = lambda t: jnp.moveaxis(t, 0, 2).reshape(t.shape[1], t.shape[2], N_DEV * t.shape[3])
    rows = lambda t: jnp.moveaxis(t, 0, 1).reshape(t.shape[1], N_DEV * t.shape[2], t.shape[3])
    w_main, w_lr = _split_in(cols(g_in))
    up_f, up_b = cols(g_upf), cols(g_upb)
    zeros_up = jnp.zeros((n_layers, GLA_RANK, HEADS * GLA_DK), BF16)
    pad_rows = jnp.zeros((n_layers, LR_PAD - 2 * GLA_RANK, HEADS * GLA_DK), BF16)
    up_pad_f = jnp.concatenate([up_f, zeros_up, pad_rows], axis=1)
    up_pad_b = jnp.concatenate([zeros_up, up_b, pad_rows], axis=1)
    up_cat = jnp.concatenate([up_pad_f, up_pad_b], axis=2)
    wkv_full, wout_full, wdown_full = rows(g_kv), rows(g_out), rows(g_down)
    wup_full = cols(g_up)

    row2 = lambda t: t.reshape(1, -1)

    bands, bias_a, bias_b = [], [], []
    for bi, (window, dilation) in enumerate(DIL_CONFIGS):
        band = _band_buckets(dilation)
        ba, bb = _bias_tiles(band, rel_bias, _dil_tile(s // dilation), f"bias_tiles_{bi}")
        bands.append(band), bias_a.append(ba), bias_b.append(bb)

    saved = []
    xl = xs
    for l in range(n_layers):
        h = _rmsnorm_fwd(xl, row2(norm_mix[l]), f"norm_mix_{l}")
        (proj,) = _mm(h, w_main[l], "nn", [F32], f"proj_{l}")
        (lr,) = _mm(h, w_lr[l], "nn", [F32], f"proj_lr_{l}")
        bias_f, bias_b_ = row2(gla_gate_bias_fwd[l]), row2(gla_gate_bias_bwd[l])
        o_f, st_f = _gla_fwd(proj, lr, up_pad_f[l], bias_f, False, f"gla_fwd_f_{l}")
        o_b, st_b = _gla_fwd(proj, lr, up_pad_b[l], bias_b_, True, f"gla_fwd_b_{l}")
        dil = [_dil_fwd(proj, bias_a[bi], dil_cfg[1], f"dil_fwd_{bi}_{l}") for bi, dil_cfg in enumerate(DIL_CONFIGS)]
        hm = _rmsnorm_fwd(mems, row2(mem_norm[l]), f"norm_mem_{l}")
        (kv,) = _mm(hm, wkv_full[l], "nn", [F32], f"mem_kv_{l}")
        mem_o, mem_lse = _mem_fwd(proj, kv, f"mem_fwd_{l}")
        mixed, dil_o, dil_lse = _mix_fwd(o_f, o_b, proj, [t[0] for t in dil], [t[1] for t in dil], mem_o,
                                         row2(gla_norm[l]), row2(dil_norm[l]), row2(mem_out_norm[l]), f"mix_fwd_{l}")
        (x1,) = _mm(mixed, wout_full[l], "nn", [F32], f"out_proj_{l}",
                    epilogue=lambda acc, res: (acc + res,), extras=(xl,))
        h2 = _rmsnorm_fwd(x1, row2(norm_mlp[l]), f"norm_mlp_{l}")
        a, u = _mm(h2, wup_full[l], "nn", [F32, BF16], f"mlp_up_{l}",
                   epilogue=lambda acc: (acc, jnp.square(jnp.maximum(acc, 0.0))))
        (x2,) = _mm(u, wdown_full[l], "nn", [F32], f"mlp_down_{l}",
                    epilogue=lambda acc, res: (acc + res,), extras=(x1,))
        saved.append(dict(x0=xl, h=h, proj=proj, lr=lr, o_f=o_f, o_b=o_b, st_f=st_f, st_b=st_b, hm=hm, kv=kv,
                          mem_o=mem_o, mem_lse=mem_lse, mixed=mixed, dil_o=dil_o, dil_lse=dil_lse, x1=x1, h2=h2,
                          a=a, u=u))
        xl = x2

    dx, dg_final, loss_part = _loss_head(xl, row2(norm_final), target, "loss_head")

    grads_small = {k: [None] * n_layers for k in SMALL}
    g_win, g_wlr, g_wkv, g_wout, g_wup, g_wdown, g_upcat = [], [], [], [], [], [], []
    dbias_sum = [None] * len(DIL_CONFIGS)
    for l in range(n_layers - 1, -1, -1):
        sv = saved[l]
        (dw_down,) = _mm(sv["u"], dx, "tn", [F32], f"dw_down_{l}")
        (da,) = _mm(dx, wdown_full[l], "nt", [BF16], f"d_mlp_act_{l}",
                    epilogue=lambda acc, a_: (acc * (2.0 * jnp.maximum(a_, 0.0)),), extras=(sv["a"],))
        (dw_up,) = _mm(sv["h2"], da, "tn", [F32], f"dw_up_{l}")
        (dh2,) = _mm(da, wup_full[l], "nt", [F32], f"d_norm_mlp_in_{l}")
        dx1, dg_mlp = _rmsnorm_bwd(sv["x1"], row2(norm_mlp[l]), dh2, dx, f"norm_mlp_bwd_{l}")
        (dw_out,) = _mm(sv["mixed"], dx1, "tn", [F32], f"dw_out_{l}")
        (dmixed,) = _mm(dx1, wout_full[l], "nt", [F32], f"d_mixed_{l}")
        (d_og, d_r, d_od, dl_d, d_om, dl_m, dg_gla, dg_dil, dg_memo) = _mix_bwd(
            dmixed, sv["o_f"], sv["o_b"], sv["proj"], sv["dil_o"], sv["mem_o"],
            row2(gla_norm[l]), row2(dil_norm[l]), row2(mem_out_norm[l]), f"mix_bwd_{l}")
        bias_f, bias_b_ = row2(gla_gate_bias_fwd[l]), row2(gla_gate_bias_bwd[l])
        dq1, dk1, dv1, dz_f = _gla_bwd(sv["proj"], sv["lr"], up_pad_f[l], bias_f, sv["st_f"], d_og, None, False,
                                       f"gla_bwd_f_{l}")
        dq_g, dk_g, dv_g, dz_b = _gla_bwd(sv["proj"], sv["lr"], up_pad_b[l], bias_b_, sv["st_b"], d_og,
                                          (dq1, dk1, dv1), True, f"gla_bwd_b_{l}")
        dz = jnp.concatenate([dz_f, dz_b], axis=1)
        (d_lr,) = _mm(dz, up_cat[l], "nt", [BF16], f"d_lowrank_{l}")
        (d_upcat,) = _mm(sv["lr"], dz, "tn", [F32], f"dw_gate_up_{l}")
        dzsum = _colsum(dz, f"d_gate_bias_{l}")
        dq_d = dk_d = dv_d = None
        for bi, (window, dilation) in enumerate(DIL_CONFIGS):
            dq_d, dbias = _dil_bwd_q(sv["proj"], d_od, sv["dil_lse"], dl_d, bias_a[bi], dq_d, dilation,
                                     f"dil_bwd_q_{bi}_{l}")
            prev = None if dk_d is None else (dk_d, dv_d)
            dk_d, dv_d = _dil_bwd_kv(sv["proj"], d_od, sv["dil_lse"], dl_d, bias_b[bi], prev, dilation,
                                     f"dil_bwd_kv_{bi}_{l}")
            dbias_sum[bi] = dbias if dbias_sum[bi] is None else dbias_sum[bi] + dbias
        dq_m, dkm, dvm = _mem_bwd(sv["proj"], sv["kv"], d_om, sv["mem_lse"], dl_m, f"mem_bwd_{l}")
        dkv = jnp.concatenate([dkm, dvm], axis=1).astype(BF16)
        (dw_kv,) = _mm(sv["hm"], dkv, "tn", [F32], f"dw_mem_kv_{l}")
        (dhm,) = _mm(dkv, wkv_full[l], "nt", [F32], f"d_mem_norm_in_{l}")
        _, dg_mem = _rmsnorm_bwd(mems, row2(mem_norm[l]), dhm, None, f"norm_mem_bwd_{l}")
        dproj = jnp.concatenate([t.astype(BF16) for t in (dq_g, dk_g, dv_g, d_r, dq_d, dk_d, dv_d, dq_m)], axis=1)
        (dw_main,) = _mm(sv["h"], dproj, "tn", [F32], f"dw_in_{l}")
        (dw_lr,) = _mm(sv["h"], d_lr, "tn", [F32], f"dw_in_lr_{l}")
        (dh_lr,) = _mm(d_lr, w_lr[l], "nt", [F32], f"d_norm_mix_in_lr_{l}")
        (dh,) = _mm(dproj, w_main[l], "nt", [F32], f"d_norm_mix_in_{l}",
                    epilogue=lambda acc, other: (acc + other,), extras=(dh_lr,))
        dx, dg_mix = _rmsnorm_bwd(sv["x0"], row2(norm_mix[l]), dh, dx1, f"norm_mix_bwd_{l}")

        grads_small["norm_mix"][l] = dg_mix
        grads_small["gla_gate_bias_fwd"][l] = dzsum[:, :HEADS * GLA_DK]
        grads_small["gla_gate_bias_bwd"][l] = dzsum[:, HEADS * GLA_DK:]
        grads_small["gla_norm"][l] = dg_gla
        grads_small["dil_norm"][l] = dg_dil
        grads_small["mem_norm"][l] = dg_mem
        grads_small["mem_out_norm"][l] = dg_memo
        grads_small["norm_mlp"][l] = dg_mlp
        g_win.append(dw_main), g_wlr.append(dw_lr), g_wkv.append(dw_kv), g_wout.append(dw_out)
        g_wup.append(dw_up), g_wdown.append(dw_down), g_upcat.append(d_upcat)

    stack = lambda lst: jnp.stack(lst[::-1])
    d_table = _bias_grad(bands, dbias_sum, "bias_grad")[:, :HEADS]

    to_cols = lambda t: jnp.moveaxis(t.reshape(t.shape[0], t.shape[1], N_DEV, -1), 2, 0)
    to_rows = lambda t: jnp.moveaxis(t.reshape(t.shape[0], N_DEV, -1, t.shape[2]), 1, 0)
    upcat = stack(g_upcat)
    d_up_f = upcat[:, :GLA_RANK, :HEADS * GLA_DK]
    d_up_b = upcat[:, GLA_RANK:2 * GLA_RANK, HEADS * GLA_DK:]
    send = [to_cols(_join_in(stack(g_win), stack(g_wlr))), to_cols(d_up_f), to_cols(d_up_b), to_rows(stack(g_wkv)),
            to_rows(stack(g_wout)), to_cols(stack(g_wup)), to_rows(stack(g_wdown))]
    r_in, r_upf, r_upb, r_kv, r_out, r_up, r_down = _exchange(send, True, "rs_grads")

    small_shapes = [weights[k].shape for k in SMALL]
    small_grads = []
    for k in SMALL:
        if k == "rel_bias":
            small_grads.append(d_table)
        elif k == "norm_final":
            small_grads.append(dg_final)
        else:
            small_grads.append(jnp.concatenate(grads_small[k], axis=0))
    n_small = sum(int(np.prod(shp)) for shp in small_shapes)
    small_rows = -(-(n_small + 128) // (8 * 128)) * 8
    pack = lambda arrs, extra: _pack(list(arrs) + [extra], small_rows)
    zeros_tail = jnp.zeros((128,), F32)
    (small_parts,) = _exchange([pack(small_grads, loss_part.reshape(-1))], False, "ag_small")
    sg, sd, sm, sv_ = _adamw(small_parts, pack([weights[k] for k in SMALL], zeros_tail),
                             pack([mom1[k] for k in SMALL], zeros_tail),
                             pack([mom2[k] for k in SMALL], zeros_tail), "adamw_small")
    loss = sg.reshape(-1)[n_small]
    small_out = [dict(zip(SMALL, _unpack(buf, small_shapes))) for buf in (sg, sd, sm, sv_)]

    def shard_update(parts, name):
        w = weights[name]
        shp = w.shape
        flat = lambda t: t.reshape(-1, shp[-1])
        res = _adamw(parts.reshape(N_DEV, -1, shp[-1]), flat(w), flat(mom1[name]), flat(mom2[name]), f"adamw_{name}")
        return [t.reshape(shp) for t in res]

    big = dict(w_in=shard_update(r_in, "w_in"), gla_gate_up_fwd=shard_update(r_upf, "gla_gate_up_fwd"),
               gla_gate_up_bwd=shard_update(r_upb, "gla_gate_up_bwd"), w_mem_kv=shard_update(r_kv, "w_mem_kv"),
               w_out=shard_update(r_out, "w_out"), w_up=shard_update(r_up, "w_up"),
               w_down=shard_update(r_down, "w_down"))

    order = ("norm_mix", "w_in", "gla_gate_up_fwd", "gla_gate_bias_fwd", "gla_gate_up_bwd", "gla_gate_bias_bwd",
             "gla_norm", "rel_bias", "dil_norm", "mem_norm", "w_mem_kv", "mem_out_norm", "w_out", "norm_mlp", "w_up",
             "w_down", "norm_final")
    outs = [loss, dx.reshape(x.shape)]
    for which in range(4):
        for name in order:
            outs.append(big[name][which] if name in big else small_out[which][name])
    return tuple(outs)
```

```python
import functools
import math

import numpy as np
import jax
import jax.numpy as jnp
from jax import lax
from jax.experimental import pallas as pl
from jax.experimental.pallas import tpu as pltpu

F32 = jnp.float32
BF16 = jnp.bfloat16

N_DEV = 8
DEPTH = 4
HEADS = 4
GLA_DK = 128
GLA_DV = 256
GLA_RANK = 16
GLA_GATE_NORMALIZER = 16.0
GLA_CHUNK = 64
HEAD_DIM = 128
DIL_CONFIGS = ((128, 1), (512, 4), (2048, 16))
DIL_HALF = 64
REL_BUCKETS = 32
REL_MAX_DISTANCE = 1024
EPS = 1e-6
NEG_INF = -1e30
IN_SPLITS = (512, 512, 1024, 1024, 16, 16, 512, 512, 512, 512)
IN_WIDTH = sum(IN_SPLITS)
MAIN_WIDTH = IN_WIDTH - 2 * GLA_RANK
LR_PAD = 128
OFF_GQ, OFF_GK, OFF_GV, OFF_GR, OFF_DQ, OFF_DK, OFF_DV, OFF_MQ = 0, 512, 1024, 2048, 3072, 3584, 4096, 4608

ADAM_LR = 0.001
ADAM_B1 = 0.9
ADAM_B2 = 0.999
ADAM_EPS = 1e-08
ADAM_WD = 0.01
ADAM_STEP = 10

VMEM_LIMIT = 56 * 1024 * 1024
ADAMW_TILE_ELEMS = 128 * 1024
MESH = pl.DeviceIdType.MESH

NN = ((1,), (0,))
NT = ((1,), (1,))
TN = ((0,), (0,))


def _dot(a, b, dims, precision=None):
    return lax.dot_general(a, b, (dims, ((), ())), preferred_element_type=F32, precision=precision)


def _tile(n, pref, mult=128):
    if n <= pref:
        return n
    t = (pref // mult) * mult
    while t >= mult:
        if n % t == 0:
            return t
        t -= mult
    return n


def _params(sem, **kw):
    return pltpu.CompilerParams(dimension_semantics=sem, vmem_limit_bytes=VMEM_LIMIT, **kw)


def _mm(a, b, mode, outs, name, epilogue=None, extras=(), tm=1024, tn=1024, tk=2048, cargo=()):
    if mode == "nn":
        (m, k), (k2, n) = a.shape, b.shape
    elif mode == "nt":
        (m, k), (n, k2) = a.shape, b.shape
    else:
        (k, m), (k2, n) = a.shape, b.shape
    assert k == k2, (a.shape, b.shape, mode)
    tm, tn, tk = _tile(m, tm), _tile(n, tn), _tile(k, tk)
    gi, gj, nk = m // tm, n // tn, k // tk
    if mode == "nn":
        a_spec = pl.BlockSpec((tm, tk), lambda i, j, kk: (i, kk))
        b_spec = pl.BlockSpec((tk, tn), lambda i, j, kk: (kk, j))
        dims = NN
    elif mode == "nt":
        a_spec = pl.BlockSpec((tm, tk), lambda i, j, kk: (i, kk))
        b_spec = pl.BlockSpec((tn, tk), lambda i, j, kk: (j, kk))
        dims = NT
    else:
        a_spec = pl.BlockSpec((tk, tm), lambda i, j, kk: (kk, i))
        b_spec = pl.BlockSpec((tk, tn), lambda i, j, kk: (kk, j))
        dims = TN
    tile_spec = pl.BlockSpec((tm, tn), lambda i, j, kk: (i, j))
    any_spec = pl.BlockSpec(memory_space=pl.ANY)
    n_extra, n_out, n_cargo = len(extras), len(outs), len(cargo)
    scatter = [sc for _, sc in cargo]
    if epilogue is None:
        epilogue = lambda acc: (acc,)

    def body(a_ref, b_ref, *rest):
        extra_refs, rest = rest[:n_extra], rest[n_extra:]
        cargo_in, rest = rest[:n_cargo], rest[n_cargo:]
        out_refs, rest = rest[:n_out], rest[n_out:]
        cargo_out, rest = rest[:n_cargo], rest[n_cargo:]
        i, j, kk = pl.program_id(0), pl.program_id(1), pl.program_id(2)
        if n_cargo:
            sems = rest[-3:]

            @pl.when((i == 0) & (j == 0) & (kk == 0))
            def _():
                _cargo_start(cargo_in, cargo_out, sems, scatter)

        def finish(total):
            res = epilogue(total, *[e[...] for e in extra_refs])
            for o_ref, r in zip(out_refs, res):
                o_ref[...] = r.astype(o_ref.dtype)

        part = _dot(a_ref[...].astype(BF16), b_ref[...].astype(BF16), dims)
        if nk == 1:
            finish(part)
        else:
            acc = rest[0]

            @pl.when(kk == 0)
            def _():
                acc[...] = part

            @pl.when((kk > 0) & (kk < nk - 1))
            def _():
                acc[...] += part

            @pl.when(kk == nk - 1)
            def _():
                finish(acc[...] + part)

        if n_cargo:
            @pl.when((i == gi - 1) & (j == gj - 1) & (kk == nk - 1))
            def _():
                _cargo_wait(cargo_in, cargo_out, sems, scatter)

    scratch = [pltpu.VMEM((tm, tn), F32)] if nk > 1 else []
    if n_cargo:
        scratch += _cargo_sems(n_cargo)
    sem = ("arbitrary",) * 3 if n_cargo else ("parallel", "parallel", "arbitrary")
    return pl.pallas_call(
        body,
        name=name,
        grid=(gi, gj, nk),
        in_specs=[a_spec, b_spec] + [tile_spec] * n_extra + [any_spec] * n_cargo,
        out_specs=[tile_spec] * n_out + [any_spec] * n_cargo,
        out_shape=[jax.ShapeDtypeStruct((m, n), d) for d in outs] + _cargo_shapes(cargo),
        scratch_shapes=scratch,
        compiler_params=_params(sem),
    )(a, b, *extras, *[x for x, _ in cargo])


def _rmsnorm_fwd(x, gain, name, rows=256):
    s, d = x.shape
    tr = _tile(s, rows, 8)

    def body(x_ref, g_ref, h_ref):
        xv = x_ref[...]
        r = lax.rsqrt(jnp.mean(xv * xv, axis=-1, keepdims=True) + EPS)
        h_ref[...] = (xv * r * g_ref[...]).astype(h_ref.dtype)

    return pl.pallas_call(
        body, name=name, grid=(s // tr,),
        in_specs=[pl.BlockSpec((tr, d), lambda i: (i, 0)), pl.BlockSpec((1, d), lambda i: (0, 0))],
        out_specs=pl.BlockSpec((tr, d), lambda i: (i, 0)),
        out_shape=jax.ShapeDtypeStruct((s, d), BF16),
        compiler_params=_params(("parallel",)),
    )(x, gain)


def _rmsnorm_bwd(x, gain, dh, dres, name, rows=256):
    s, d = x.shape
    tr = _tile(s, rows, 8)
    has_res = dres is not None

    def body(*refs):
        if has_res:
            x_ref, g_ref, dh_ref, dres_ref, dx_ref, dxb_ref, dg_ref = refs
        else:
            x_ref, g_ref, dh_ref, dx_ref, dxb_ref, dg_ref = refs
        i = pl.program_id(0)
        xv = x_ref[...]
        dy = dh_ref[...].astype(F32)
        r = lax.rsqrt(jnp.mean(xv * xv, axis=-1, keepdims=True) + EPS)
        xn = xv * r
        t = dy * g_ref[...]
        dx = r * (t - xn * jnp.mean(t * xn, axis=-1, keepdims=True))
        if has_res:
            dx = dx + dres_ref[...]
        dx_ref[...] = dx
        dxb_ref[...] = dx.astype(BF16)

        @pl.when(i == 0)
        def _():
            dg_ref[...] = jnp.zeros_like(dg_ref)

        dg_ref[...] += jnp.sum(dy * xn, axis=0, keepdims=True)

    row_spec = pl.BlockSpec((tr, d), lambda i: (i, 0))
    vec_spec = pl.BlockSpec((1, d), lambda i: (0, 0))
    args = [x, gain, dh] + ([dres] if has_res else [])
    return pl.pallas_call(
        body, name=name, grid=(s // tr,),
        in_specs=[row_spec, vec_spec, row_spec] + ([row_spec] if has_res else []),
        out_specs=[row_spec, row_spec, vec_spec],
        out_shape=[jax.ShapeDtypeStruct((s, d), F32), jax.ShapeDtypeStruct((s, d), BF16),
                   jax.ShapeDtypeStruct((1, d), F32)],
        compiler_params=_params(("arbitrary",)),
    )(*args)


def _loss_head(x, gain, target, name, rows=256):
    s, d = x.shape
    tr = _tile(s, rows, 8)

    def body(x_ref, g_ref, t_ref, dx_ref, dxb_ref, dg_ref, loss_ref):
        i = pl.program_id(0)
        xv = x_ref[...]
        g = g_ref[...]
        r = lax.rsqrt(jnp.mean(xv * xv, axis=-1, keepdims=True) + EPS)
        xn = xv * r
        err = xn * g - t_ref[...]
        dy = err * (1.0 / d)
        t = dy * g
        dx = r * (t - xn * jnp.mean(t * xn, axis=-1, keepdims=True))
        dx_ref[...] = dx
        dxb_ref[...] = dx.astype(BF16)

        @pl.when(i == 0)
        def _():
            dg_ref[...] = jnp.zeros_like(dg_ref)
            loss_ref[...] = jnp.zeros_like(loss_ref)

        dg_ref[...] += jnp.sum(dy * xn, axis=0, keepdims=True)
        part = 0.5 * jnp.sum(jnp.mean(err * err, axis=-1, keepdims=True), axis=0, keepdims=True)
        loss_ref[...] += jnp.broadcast_to(part, loss_ref.shape)

    row_spec = pl.BlockSpec((tr, d), lambda i: (i, 0))
    vec_spec = pl.BlockSpec((1, d), lambda i: (0, 0))
    return pl.pallas_call(
        body, name=name, grid=(s // tr,),
        in_specs=[row_spec, vec_spec, row_spec],
        out_specs=[row_spec, row_spec, vec_spec, pl.BlockSpec((1, 128), lambda i: (0, 0))],
        out_shape=[jax.ShapeDtypeStruct((s, d), F32), jax.ShapeDtypeStruct((s, d), BF16),
                   jax.ShapeDtypeStruct((1, d), F32), jax.ShapeDtypeStruct((1, 128), F32)],
        compiler_params=_params(("arbitrary",)),
    )(x, gain, target)


def _log_sigmoid(z):
    return jnp.minimum(z, 0.0) - jnp.log1p(jnp.exp(-jnp.abs(z)))


def _gla_tri(reverse):
    row = lax.broadcasted_iota(jnp.int32, (GLA_CHUNK, GLA_CHUNK), 0)
    col = lax.broadcasted_iota(jnp.int32, (GLA_CHUNK, GLA_CHUNK), 1)
    return (col >= row) if reverse else (col <= row)


def _gla_rows(s):
    return _tile(s, 256, GLA_CHUNK)


def _gla_chunk_terms(q, k, g, tri, reverse):
    b = _dot(tri.astype(F32), g, NN, precision=lax.Precision.HIGHEST)
    bl = b[0:1] if reverse else b[GLA_CHUNK - 1:GLA_CHUNK]
    qd = q * jnp.exp(b)
    ki = k * jnp.exp(-b)
    ke = k * jnp.exp(bl - b)
    return b, bl, qd, ki, ke


def _gla_fwd(proj, lr, up_pad, bias, reverse, name):
    s = proj.shape[0]
    ts = _gla_rows(s)
    nblk, cpb = s // ts, ts // GLA_CHUNK
    scale = GLA_DK ** -0.5

    def blk(i):
        return (nblk - 1 - i) if reverse else i

    def body(q_ref, k_ref, v_ref, lr_ref, up_ref, b_ref, o_ref, st_ref, state):
        @pl.when(pl.program_id(1) == 0)
        def _():
            state[...] = jnp.zeros_like(state)

        tri = _gla_tri(reverse)
        z = _dot(lr_ref[...].astype(BF16), up_ref[...], NN) + b_ref[...]
        g_all = _log_sigmoid(z) * (1.0 / GLA_GATE_NORMALIZER)
        order = range(cpb - 1, -1, -1) if reverse else range(cpb)
        for c in order:
            sl = slice(c * GLA_CHUNK, (c + 1) * GLA_CHUNK)
            _, bl, qd, ki, ke = _gla_chunk_terms(q_ref[sl, :] * scale, k_ref[sl, :], g_all[sl, :], tri, reverse)
            qdb = qd.astype(BF16)
            a = jnp.where(tri, _dot(qdb, ki.astype(BF16), NT), 0.0)
            vb = v_ref[sl, :].astype(BF16)
            st = state[...]
            o_ref[sl, :] = _dot(a.astype(BF16), vb, NN) + _dot(qdb, st.astype(BF16), NT)
            st_ref[0, c] = st
            state[...] = st * jnp.exp(bl) + _dot(vb, ke.astype(BF16), TN)

    qk = lambda off: pl.BlockSpec((ts, GLA_DK), lambda h, i: (blk(i), off // GLA_DK + h))
    return pl.pallas_call(
        body, name=name, grid=(HEADS, nblk),
        in_specs=[qk(OFF_GQ), qk(OFF_GK),
                  pl.BlockSpec((ts, GLA_DV), lambda h, i: (blk(i), OFF_GV // GLA_DV + h)),
                  pl.BlockSpec((ts, LR_PAD), lambda h, i: (blk(i), 0)),
                  pl.BlockSpec((LR_PAD, GLA_DK), lambda h, i: (0, h)),
                  pl.BlockSpec((1, GLA_DK), lambda h, i: (0, h))],
        out_specs=[pl.BlockSpec((ts, GLA_DV), lambda h, i: (blk(i), h)),
                   pl.BlockSpec((1, cpb, GLA_DV, GLA_DK), lambda h, i: (h, blk(i), 0, 0))],
        out_shape=[jax.ShapeDtypeStruct((s, HEADS * GLA_DV), F32),
                   jax.ShapeDtypeStruct((HEADS, s // GLA_CHUNK, GLA_DV, GLA_DK), F32)],
        scratch_shapes=[pltpu.VMEM((GLA_DV, GLA_DK), F32)],
        compiler_params=_params(("parallel", "arbitrary")),
    )(proj, proj, proj, lr, up_pad, bias)


def _gla_bwd(proj, lr, up_pad, bias, states, d_o, prev, reverse, name):
    s = proj.shape[0]
    ts = _gla_rows(s)
    nblk, cpb = s // ts, ts // GLA_CHUNK
    scale = GLA_DK ** -0.5
    has_prev = prev is not None

    def blk(i):
        return i if reverse else (nblk - 1 - i)

    def body(*refs):
        q_ref, k_ref, v_ref, lr_ref, up_ref, b_ref, st_ref, do_ref = refs[:8]
        refs = refs[8:]
        if has_prev:
            pq_ref, pk_ref, pv_ref = refs[:3]
            refs = refs[3:]
        dq_ref, dk_ref, dv_ref, dz_ref, dstate = refs

        @pl.when(pl.program_id(1) == 0)
        def _():
            dstate[...] = jnp.zeros_like(dstate)

        tri = _gla_tri(reverse)
        tri_t = _gla_tri(not reverse)
        row = lax.broadcasted_iota(jnp.int32, (GLA_CHUNK, GLA_DK), 0)
        last_row = (row == 0) if reverse else (row == GLA_CHUNK - 1)
        z = _dot(lr_ref[...].astype(BF16), up_ref[...], NN) + b_ref[...]
        g_all = _log_sigmoid(z) * (1.0 / GLA_GATE_NORMALIZER)
        dgate = (1.0 / GLA_GATE_NORMALIZER) * (1.0 - jax.nn.sigmoid(z))
        order = range(cpb) if reverse else range(cpb - 1, -1, -1)
        for c in order:
            sl = slice(c * GLA_CHUNK, (c + 1) * GLA_CHUNK)
            b, bl, qd, ki, ke = _gla_chunk_terms(q_ref[sl, :] * scale, k_ref[sl, :], g_all[sl, :], tri, reverse)
            qdb, kib, keb = qd.astype(BF16), ki.astype(BF16), ke.astype(BF16)
            a = jnp.where(tri, _dot(qdb, kib, NT), 0.0)
            vb = v_ref[sl, :].astype(BF16)
            dob = do_ref[sl, :].astype(BF16)
            st = st_ref[0, c]
            dst = dstate[...]
            dstb = dst.astype(BF16)
            da = jnp.where(tri, _dot(dob, vb, NT), 0.0).astype(BF16)
            dv = _dot(a.astype(BF16), dob, TN) + _dot(keb, dstb, NT)
            dqd = _dot(da, kib, NN) + _dot(dob, st.astype(BF16), NN)
            dki = _dot(da, qdb, TN)
            dke = _dot(vb, dstb, NN)
            decay = jnp.exp(bl)
            dbl = decay * jnp.sum(dst * st, axis=0, keepdims=True) + jnp.sum(dke * ke, axis=0, keepdims=True)
            dstate[...] = dst * decay + _dot(dob, qdb, TN)
            db = dqd * qd - dki * ki - dke * ke + jnp.where(last_row, dbl, 0.0)
            dg = _dot(tri_t.astype(F32), db, NN, precision=lax.Precision.HIGHEST)
            dq = dqd * jnp.exp(b) * scale
            dk = dki * jnp.exp(-b) + dke * jnp.exp(bl - b)
            if has_prev:
                dq = dq + pq_ref[sl, :]
                dk = dk + pk_ref[sl, :]
                dv = dv + pv_ref[sl, :]
            dq_ref[sl, :] = dq
            dk_ref[sl, :] = dk
            dv_ref[sl, :] = dv
            dz_ref[sl, :] = dg * dgate[sl, :]

    qk = lambda off: pl.BlockSpec((ts, GLA_DK), lambda h, i: (blk(i), off // GLA_DK + h))
    hk = pl.BlockSpec((ts, GLA_DK), lambda h, i: (blk(i), h))
    hv = pl.BlockSpec((ts, GLA_DV), lambda h, i: (blk(i), h))
    in_specs = [qk(OFF_GQ), qk(OFF_GK),
                pl.BlockSpec((ts, GLA_DV), lambda h, i: (blk(i), OFF_GV // GLA_DV + h)),
                pl.BlockSpec((ts, LR_PAD), lambda h, i: (blk(i), 0)),
                pl.BlockSpec((LR_PAD, GLA_DK), lambda h, i: (0, h)),
                pl.BlockSpec((1, GLA_DK), lambda h, i: (0, h)),
                pl.BlockSpec((1, cpb, GLA_DV, GLA_DK), lambda h, i: (h, blk(i), 0, 0)),
                hv]
    args = [proj, proj, proj, lr, up_pad, bias, states, d_o]
    if has_prev:
        in_specs += [hk, hk, hv]
        args += list(prev)
    return pl.pallas_call(
        body, name=name, grid=(HEADS, nblk),
        in_specs=in_specs,
        out_specs=[hk, hk, hv, hk],
        out_shape=[jax.ShapeDtypeStruct((s, HEADS * GLA_DK), F32), jax.ShapeDtypeStruct((s, HEADS * GLA_DK), F32),
                   jax.ShapeDtypeStruct((s, HEADS * GLA_DV), F32), jax.ShapeDtypeStruct((s, HEADS * GLA_DK), F32)],
        scratch_shapes=[pltpu.VMEM((GLA_DV, GLA_DK), F32)],
        compiler_params=_params(("parallel", "arbitrary")),
    )(*args)


def _dil_tile(l):
    return _tile(l, 256, DIL_HALF)


def t5_bucket(rel):
    half = REL_BUCKETS // 2
    max_exact = half // 2
    ret = jnp.where(rel > 0, half, 0)
    n = jnp.abs(rel)
    nf = jnp.maximum(n, 1).astype(jnp.float32)
    large = max_exact + (jnp.log(nf / max_exact) / math.log(REL_MAX_DISTANCE / max_exact)
                         * (half - max_exact)).astype(jnp.int32)
    large = jnp.minimum(large, half - 1)
    return ret + jnp.where(n < max_exact, n, large)


def _band_buckets(dilation):
    w = DIL_HALF
    rel_sub = jnp.arange(3 * w)[None, :] - w - jnp.arange(w)[:, None]
    return t5_bucket(rel_sub * dilation)[0, :2 * w + 1].astype(jnp.int32)


def _band_offsets(tq):
    tk = tq + 2 * DIL_HALF
    da = lax.broadcasted_iota(jnp.int32, (tq, tk), 1) - lax.broadcasted_iota(jnp.int32, (tq, tk), 0)
    db = (lax.broadcasted_iota(jnp.int32, (tk, tq), 1) - lax.broadcasted_iota(jnp.int32, (tk, tq), 0)
          + 2 * DIL_HALF)
    return jnp.clip(da, 0, 2 * DIL_HALF), jnp.clip(db, 0, 2 * DIL_HALF)


def _bias_tiles(band, table, tq, name):
    tk = tq + 2 * DIL_HALF

    def body(band_ref, t_ref, oa_ref, ob_ref):
        h = pl.program_id(0)
        off_a, off_b = _band_offsets(tq)

        def step(t, carry):
            bkt_a, bkt_b = carry
            bkt = band_ref[t]
            return jnp.where(off_a == t, bkt, bkt_a), jnp.where(off_b == t, bkt, bkt_b)

        bkt_a, bkt_b = lax.fori_loop(0, 2 * DIL_HALF + 1, step,
                                     (jnp.zeros((tq, tk), jnp.int32), jnp.zeros((tk, tq), jnp.int32)))
        acc_a, acc_b = jnp.zeros((tq, tk), F32), jnp.zeros((tk, tq), F32)
        for bkt in range(REL_BUCKETS):
            val = t_ref[bkt, h]
            acc_a = jnp.where(bkt_a == bkt, val, acc_a)
            acc_b = jnp.where(bkt_b == bkt, val, acc_b)
        oa_ref[0] = acc_a
        ob_ref[0] = acc_b

    smem = pl.BlockSpec(memory_space=pltpu.SMEM)
    return pl.pallas_call(
        body, name=name, grid=(HEADS,),
        in_specs=[smem, smem],
        out_specs=[pl.BlockSpec((1, tq, tk), lambda h: (h, 0, 0)), pl.BlockSpec((1, tk, tq), lambda h: (h, 0, 0))],
        out_shape=[jax.ShapeDtypeStruct((HEADS, tq, tk), F32), jax.ShapeDtypeStruct((HEADS, tk, tq), F32)],
        compiler_params=_params(("arbitrary",)),
    )(band, table)


def _bias_grad(bands, dbias_list, name):
    n = len(bands)

    def body(*refs):
        band_refs, db_refs, out_ref = refs[:n], refs[n:2 * n], refs[2 * n]
        row = lax.broadcasted_iota(jnp.int32, (REL_BUCKETS, 128), 0)
        lane = lax.broadcasted_iota(jnp.int32, (REL_BUCKETS, 128), 1)
        acc = jnp.zeros((REL_BUCKETS, 128), F32)
        for band_ref, d_ref in zip(band_refs, db_refs):
            off_a, _ = _band_offsets(d_ref.shape[1])
            for h in range(HEADS):
                def step(t, acc, band_ref=band_ref, d_ref=d_ref, h=h, off_a=off_a):
                    tot = jnp.sum(jnp.where(off_a == t, d_ref[h], 0.0))
                    return acc + jnp.where((row == band_ref[t]) & (lane == h), tot, 0.0)

                acc = lax.fori_loop(0, 2 * DIL_HALF + 1, step, acc)
        out_ref[...] = acc

    vm = pl.BlockSpec(memory_space=pltpu.VMEM)
    smem = pl.BlockSpec(memory_space=pltpu.SMEM)
    return pl.pallas_call(
        body, name=name,
        in_specs=[smem] * n + [vm] * n, out_specs=vm,
        out_shape=jax.ShapeDtypeStruct((REL_BUCKETS, 128), F32),
        compiler_params=pltpu.CompilerParams(vmem_limit_bytes=VMEM_LIMIT),
    )(*bands, *dbias_list)


def _dil_specs(l, tq, dilation, width, off, by_head_first):
    nb64 = l // DIL_HALF
    per = tq // DIL_HALF

    def col(h, r):
        return (r * width + off) // HEAD_DIM + h

    def wrap(f):
        if by_head_first:
            return lambda h, r, n: f(h, r, n)
        return lambda r, h, n: f(h, r, n)

    prev = pl.BlockSpec((DIL_HALF, HEAD_DIM), wrap(lambda h, r, n: (jnp.maximum(n * per - 1, 0), col(h, r))))
    main = pl.BlockSpec((tq, HEAD_DIM), wrap(lambda h, r, n: (n, col(h, r))))
    nxt = pl.BlockSpec((DIL_HALF, HEAD_DIM), wrap(lambda h, r, n: (jnp.minimum((n + 1) * per, nb64 - 1), col(h, r))))
    return prev, main, nxt


def _dil_fwd(proj, bias_a, dilation, name):
    s = proj.shape[0]
    l = s // dilation
    tq = _dil_tile(l)
    tk = tq + 2 * DIL_HALF
    nq = l // tq
    scale = HEAD_DIM ** -0.5
    view = proj.reshape(l, dilation * MAIN_WIDTH)

    def body(q_ref, kp_ref, km_ref, kn_ref, vp_ref, vm_ref, vn_ref, b_ref, o_ref, lse_ref):
        n = pl.program_id(2)
        q = q_ref[...].astype(BF16)
        kc = jnp.concatenate([kp_ref[...], km_ref[...], kn_ref[...]], axis=0).astype(BF16)
        vc = jnp.concatenate([vp_ref[...], vm_ref[...], vn_ref[...]], axis=0).astype(BF16)
        sc = _dot(q, kc, NT) * scale + b_ref[0]
        qpos = n * tq + lax.broadcasted_iota(jnp.int32, (tq, tk), 0)
        kpos = n * tq - DIL_HALF + lax.broadcasted_iota(jnp.int32, (tq, tk), 1)
        mask = (jnp.abs(kpos - qpos) <= DIL_HALF) & (kpos >= 0) & (kpos < l)
        sc = jnp.where(mask, sc, NEG_INF)
        m = jnp.max(sc, axis=-1, keepdims=True)
        p = jnp.exp(sc - m)
        den = jnp.sum(p, axis=-1, keepdims=True)
        o_ref[...] = _dot(p.astype(BF16), vc, NN) / den
        lse_ref[...] = jnp.broadcast_to(m + jnp.log(den), (tq, HEAD_DIM))

    _, q_main, _ = _dil_specs(l, tq, dilation, MAIN_WIDTH, OFF_DQ, False)
    k_specs = _dil_specs(l, tq, dilation, MAIN_WIDTH, OFF_DK, False)
    v_specs = _dil_specs(l, tq, dilation, MAIN_WIDTH, OFF_DV, False)
    _, o_main, _ = _dil_specs(l, tq, dilation, HEADS * HEAD_DIM, 0, False)
    o, lse = pl.pallas_call(
        body, name=name, grid=(dilation, HEADS, nq),
        in_specs=[q_main, *k_specs, *v_specs, pl.BlockSpec((1, tq, tk), lambda r, h, n: (h, 0, 0))],
        out_specs=[o_main, o_main],
        out_shape=[jax.ShapeDtypeStruct((l, dilation * HEADS * HEAD_DIM), F32)] * 2,
        compiler_params=_params(("parallel", "parallel", "parallel")),
    )(view, view, view, view, view, view, view, bias_a)
    return o.reshape(s, HEADS * HEAD_DIM), lse.reshape(s, HEADS * HEAD_DIM)


def _dil_bwd_q(proj, d_o, lse, delta, bias_a, prev_dq, dilation, name):
    s = proj.shape[0]
    l = s // dilation
    tq = _dil_tile(l)
    tk = tq + 2 * DIL_HALF
    nq = l // tq
    scale = HEAD_DIM ** -0.5
    w4 = HEADS * HEAD_DIM
    view = proj.reshape(l, dilation * MAIN_WIDTH)
    small = lambda t: t.reshape(l, dilation * w4)
    has_prev = prev_dq is not None

    def body(*refs):
        q_ref, kp_ref, km_ref, kn_ref, vp_ref, vm_ref, vn_ref, b_ref, do_ref, lse_ref, dl_ref = refs[:11]
        refs = refs[11:]
        if has_prev:
            pq_ref, refs = refs[0], refs[1:]
        dq_ref, db_ref = refs
        r, n = pl.program_id(1), pl.program_id(2)
        q = q_ref[...].astype(BF16)
        kc = jnp.concatenate([kp_ref[...], km_ref[...], kn_ref[...]], axis=0).astype(BF16)
        vc = jnp.concatenate([vp_ref[...], vm_ref[...], vn_ref[...]], axis=0).astype(BF16)
        sc = _dot(q, kc, NT) * scale + b_ref[0]
        qpos = n * tq + lax.broadcasted_iota(jnp.int32, (tq, tk), 0)
        kpos = n * tq - DIL_HALF + lax.broadcasted_iota(jnp.int32, (tq, tk), 1)
        mask = (jnp.abs(kpos - qpos) <= DIL_HALF) & (kpos >= 0) & (kpos < l)
        p = jnp.where(mask, jnp.exp(sc - lse_ref[:, 0:1]), 0.0)
        dp = _dot(do_ref[...].astype(BF16), vc, NT)
        ds = p * (dp - dl_ref[:, 0:1])
        dq = _dot(ds.astype(BF16), kc, NN) * scale
        if has_prev:
            dq = dq + pq_ref[...]
        dq_ref[...] = dq

        @pl.when((r == 0) & (n == 0))
        def _():
            db_ref[...] = jnp.zeros_like(db_ref)

        db_ref[0] += ds

    _, q_main, _ = _dil_specs(l, tq, dilation, MAIN_WIDTH, OFF_DQ, True)
    k_specs = _dil_specs(l, tq, dilation, MAIN_WIDTH, OFF_DK, True)
    v_specs = _dil_specs(l, tq, dilation, MAIN_WIDTH, OFF_DV, True)
    _, o_main, _ = _dil_specs(l, tq, dilation, w4, 0, True)
    bias_spec = pl.BlockSpec((1, tq, tk), lambda h, r, n: (h, 0, 0))
    in_specs = [q_main, *k_specs, *v_specs, bias_spec, o_main, o_main, o_main] + ([o_main] if has_prev else [])
    args = [view] * 7 + [bias_a, small(d_o), small(lse), small(delta)] + ([small(prev_dq)] if has_prev else [])
    dq, dbias = pl.pallas_call(
        body, name=name, grid=(HEADS, dilation, nq),
        in_specs=in_specs,
        out_specs=[o_main, bias_spec],
        out_shape=[jax.ShapeDtypeStruct((l, dilation * w4), F32), jax.ShapeDtypeStruct((HEADS, tq, tk), F32)],
        compiler_params=_params(("arbitrary", "arbitrary", "arbitrary")),
    )(*args)
    return dq.reshape(s, w4), dbias


def _dil_bwd_kv(proj, d_o, lse, delta, bias_b, prev, dilation, name):
    s = proj.shape[0]
    l = s // dilation
    tq = _dil_tile(l)
    tw = tq + 2 * DIL_HALF
    nq = l // tq
    scale = HEAD_DIM ** -0.5
    w4 = HEADS * HEAD_DIM
    view = proj.reshape(l, dilation * MAIN_WIDTH)
    small = lambda t: t.reshape(l, dilation * w4)
    has_prev = prev is not None

    def body(*refs):
        (qp_ref, qm_ref, qn_ref, k_ref, v_ref, b_ref, dop_ref, dom_ref, don_ref,
         lp_ref, lm_ref, ln_ref, dp_ref, dm_ref, dn_ref) = refs[:15]
        refs = refs[15:]
        if has_prev:
            pk_ref, pv_ref = refs[:2]
            refs = refs[2:]
        dk_ref, dv_ref = refs
        n = pl.program_id(2)
        cat = lambda a, b_, c: jnp.concatenate([a[...], b_[...], c[...]], axis=0)
        qc = cat(qp_ref, qm_ref, qn_ref).astype(BF16)
        doc = cat(dop_ref, dom_ref, don_ref).astype(BF16)
        lsec = cat(lp_ref, lm_ref, ln_ref)[:, 0:1]
        dlc = cat(dp_ref, dm_ref, dn_ref)[:, 0:1]
        kb = k_ref[...].astype(BF16)
        vb = v_ref[...].astype(BF16)
        sc = _dot(qc, kb, NT) * scale + b_ref[0]
        qpos = n * tq - DIL_HALF + lax.broadcasted_iota(jnp.int32, (tw, tq), 0)
        kpos = n * tq + lax.broadcasted_iota(jnp.int32, (tw, tq), 1)
        mask = (jnp.abs(kpos - qpos) <= DIL_HALF) & (qpos >= 0) & (qpos < l)
        p = jnp.where(mask, jnp.exp(sc - lsec), 0.0)
        dv = _dot(p.astype(BF16), doc, TN)
        dp = _dot(doc, vb, NT)
        ds = p * (dp - dlc)
        dk = _dot(ds.astype(BF16), qc, TN) * scale
        if has_prev:
            dk = dk + pk_ref[...]
            dv = dv + pv_ref[...]
        dk_ref[...] = dk
        dv_ref[...] = dv

    q_specs = _dil_specs(l, tq, dilation, MAIN_WIDTH, OFF_DQ, False)
    _, k_main, _ = _dil_specs(l, tq, dilation, MAIN_WIDTH, OFF_DK, False)
    _, v_main, _ = _dil_specs(l, tq, dilation, MAIN_WIDTH, OFF_DV, False)
    o_specs = _dil_specs(l, tq, dilation, w4, 0, False)
    o_main = o_specs[1]
    in_specs = [*q_specs, k_main, v_main, pl.BlockSpec((1, tw, tq), lambda r, h, n: (h, 0, 0)),
                *o_specs, *o_specs, *o_specs] + ([o_main, o_main] if has_prev else [])
    args = ([view] * 5 + [bias_b] + [small(d_o)] * 3 + [small(lse)] * 3 + [small(delta)] * 3
            + ([small(prev[0]), small(prev[1])] if has_prev else []))
    dk, dv = pl.pallas_call(
        body, name=name, grid=(dilation, HEADS, nq),
        in_specs=in_specs,
        out_specs=[o_main, o_main],
        out_shape=[jax.ShapeDtypeStruct((l, dilation * w4), F32)] * 2,
        compiler_params=_params(("parallel", "parallel", "parallel")),
    )(*args)
    return dk.reshape(s, w4), dv.reshape(s, w4)


def _mem_fwd(proj, kv, name):
    s = proj.shape[0]
    mlen = kv.shape[0]
    tq = _tile(s, 512, 8)
    scale = HEAD_DIM ** -0.5
    w4 = HEADS * HEAD_DIM

    def body(q_ref, k_ref, v_ref, o_ref, lse_ref):
        sc = _dot(q_ref[...].astype(BF16), k_ref[...].astype(BF16), NT) * scale
        m = jnp.max(sc, axis=-1, keepdims=True)
        e = jnp.exp(sc - m)
        den = jnp.sum(e, axis=-1, keepdims=True)
        o_ref[...] = _dot((e / den).astype(BF16), v_ref[...].astype(BF16), NN)
        lse_ref[...] = jnp.broadcast_to(m + jnp.log(den), (tq, HEAD_DIM))

    o_spec = pl.BlockSpec((tq, HEAD_DIM), lambda h, n: (n, h))
    return pl.pallas_call(
        body, name=name, grid=(HEADS, s // tq),
        in_specs=[pl.BlockSpec((tq, HEAD_DIM), lambda h, n: (n, OFF_MQ // HEAD_DIM + h)),
                  pl.BlockSpec((mlen, HEAD_DIM), lambda h, n: (0, h)),
                  pl.BlockSpec((mlen, HEAD_DIM), lambda h, n: (0, HEADS + h))],
        out_specs=[o_spec, o_spec],
        out_shape=[jax.ShapeDtypeStruct((s, w4), F32)] * 2,
        compiler_params=_params(("parallel", "parallel")),
    )(proj, kv, kv)


def _mem_bwd(proj, kv, d_o, lse, delta, name):
    s = proj.shape[0]
    mlen = kv.shape[0]
    tq = _tile(s, 512, 8)
    scale = HEAD_DIM ** -0.5
    w4 = HEADS * HEAD_DIM

    def body(q_ref, k_ref, v_ref, do_ref, lse_ref, dl_ref, dq_ref, dk_ref, dv_ref):
        qb = q_ref[...].astype(BF16)
        kb = k_ref[...].astype(BF16)
        dob = do_ref[...].astype(BF16)
        sc = _dot(qb, kb, NT) * scale
        p = jnp.exp(sc - lse_ref[:, 0:1])
        dp = _dot(dob, v_ref[...].astype(BF16), NT)
        ds = (p * (dp - dl_ref[:, 0:1])).astype(BF16)
        dq_ref[...] = _dot(ds, kb, NN) * scale

        @pl.when(pl.program_id(1) == 0)
        def _():
            dk_ref[...] = jnp.zeros_like(dk_ref)
            dv_ref[...] = jnp.zeros_like(dv_ref)

        dk_ref[...] += _dot(ds, qb, TN) * scale
        dv_ref[...] += _dot(p.astype(BF16), dob, TN)

    o_spec = pl.BlockSpec((tq, HEAD_DIM), lambda h, n: (n, h))
    k_spec = pl.BlockSpec((mlen, HEAD_DIM), lambda h, n: (0, h))
    v_spec = pl.BlockSpec((mlen, HEAD_DIM), lambda h, n: (0, HEADS + h))
    dq, dkv, dkv2 = pl.pallas_call(
        body, name=name, grid=(HEADS, s // tq),
        in_specs=[pl.BlockSpec((tq, HEAD_DIM), lambda h, n: (n, OFF_MQ // HEAD_DIM + h)),
                  k_spec, v_spec, o_spec, o_spec, o_spec],
        out_specs=[o_spec, k_spec, k_spec],
        out_shape=[jax.ShapeDtypeStruct((s, w4), F32), jax.ShapeDtypeStruct((mlen, w4), F32),
                   jax.ShapeDtypeStruct((mlen, w4), F32)],
        compiler_params=_params(("parallel", "arbitrary")),
    )(proj, kv, kv, d_o, lse, delta)
    return dq, dkv, dkv2


def _head_norm(o, gain, width):
    outs, xns = [], []
    for h in range(HEADS):
        oh = o[:, h * width:(h + 1) * width]
        r = lax.rsqrt(jnp.mean(oh * oh, axis=-1, keepdims=True) + EPS)
        xn = oh * r
        xns.append(xn)
        outs.append(xn * gain[:, h * width:(h + 1) * width])
    return outs, xns


def _head_norm_bwd(o, gain, dy, width):
    dos, dgs = [], []
    for h in range(HEADS):
        sl = slice(h * width, (h + 1) * width)
        oh = o[:, sl]
        r = lax.rsqrt(jnp.mean(oh * oh, axis=-1, keepdims=True) + EPS)
        xn = oh * r
        t = dy[:, sl] * gain[:, sl]
        dos.append(r * (t - xn * jnp.mean(t * xn, axis=-1, keepdims=True)))
        dgs.append(jnp.sum(dy[:, sl] * xn, axis=0, keepdims=True))
    return dos, dgs


def _mix_fwd(o_f, o_b, proj, dil_os, dil_lses, mem_o, g_gla, g_dil, g_mem, name):
    s = o_f.shape[0]
    tr = _tile(s, 256, 8)
    w4 = HEADS * HEAD_DIM
    wv = HEADS * GLA_DV

    def body(of_ref, ob_ref, r_ref, o1_ref, o2_ref, o3_ref, l1_ref, l2_ref, l3_ref, mo_ref,
             gg_ref, gd_ref, gm_ref, mix_ref, do_ref, dl_ref):
        o = of_ref[...] + ob_ref[...]
        normed, _ = _head_norm(o, gg_ref[...], GLA_DV)
        rv = r_ref[...]
        gate = rv * jax.nn.sigmoid(rv)
        for h in range(HEADS):
            mix_ref[:, h * GLA_DV:(h + 1) * GLA_DV] = (normed[h] * gate[:, h * GLA_DV:(h + 1) * GLA_DV]).astype(BF16)
        l1, l2, l3 = l1_ref[...], l2_ref[...], l3_ref[...]
        m = jnp.maximum(jnp.maximum(l1, l2), l3)
        e1, e2, e3 = jnp.exp(l1 - m), jnp.exp(l2 - m), jnp.exp(l3 - m)
        den = e1 + e2 + e3
        od = (e1 * o1_ref[...] + e2 * o2_ref[...] + e3 * o3_ref[...]) / den
        do_ref[...] = od
        dl_ref[...] = m + jnp.log(den)
        nd, _ = _head_norm(od, gd_ref[...], HEAD_DIM)
        nm, _ = _head_norm(mo_ref[...], gm_ref[...], HEAD_DIM)
        for h in range(HEADS):
            mix_ref[:, wv + h * HEAD_DIM:wv + (h + 1) * HEAD_DIM] = nd[h].astype(BF16)
            mix_ref[:, wv + w4 + h * HEAD_DIM:wv + w4 + (h + 1) * HEAD_DIM] = nm[h].astype(BF16)

    rows = lambda w, c=0: pl.BlockSpec((tr, w), lambda i: (i, c))
    vec = lambda w: pl.BlockSpec((1, w), lambda i: (0, 0))
    return pl.pallas_call(
        body, name=name, grid=(s // tr,),
        in_specs=[rows(wv), rows(wv), rows(wv, OFF_GR // wv)] + [rows(w4)] * 7 + [vec(wv), vec(w4), vec(w4)],
        out_specs=[rows(wv + 2 * w4), rows(w4), rows(w4)],
        out_shape=[jax.ShapeDtypeStruct((s, wv + 2 * w4), BF16), jax.ShapeDtypeStruct((s, w4), F32),
                   jax.ShapeDtypeStruct((s, w4), F32)],
        compiler_params=_params(("parallel",)),
    )(o_f, o_b, proj, *dil_os, *dil_lses, mem_o, g_gla, g_dil, g_mem)


def _mix_bwd(dmixed, o_f, o_b, proj, dil_o, mem_o, g_gla, g_dil, g_mem, name):
    s = o_f.shape[0]
    tr = _tile(s, 256, 8)
    w4 = HEADS * HEAD_DIM
    wv = HEADS * GLA_DV

    def body(dm_ref, of_ref, ob_ref, r_ref, od_ref, mo_ref, gg_ref, gd_ref, gm_ref,
             dog_ref, dr_ref, dod_ref, dld_ref, dom_ref, dlm_ref, dgg_ref, dgd_ref, dgm_ref):
        i = pl.program_id(0)

        @pl.when(i == 0)
        def _():
            dgg_ref[...] = jnp.zeros_like(dgg_ref)
            dgd_ref[...] = jnp.zeros_like(dgd_ref)
            dgm_ref[...] = jnp.zeros_like(dgm_ref)

        dm = dm_ref[...]
        o = of_ref[...] + ob_ref[...]
        normed, _ = _head_norm(o, gg_ref[...], GLA_DV)
        rv = r_ref[...]
        sg = jax.nn.sigmoid(rv)
        gate = rv * sg
        dgate = sg * (1.0 + rv * (1.0 - sg))
        d_gla = dm[:, :wv]
        for h in range(HEADS):
            sl = slice(h * GLA_DV, (h + 1) * GLA_DV)
            dr_ref[:, sl] = d_gla[:, sl] * normed[h] * dgate[:, sl]
        dos, dgs = _head_norm_bwd(o, gg_ref[...], d_gla * gate, GLA_DV)
        for h in range(HEADS):
            sl = slice(h * GLA_DV, (h + 1) * GLA_DV)
            dog_ref[:, sl] = dos[h]
            dgg_ref[:, sl] += dgs[h]
        for src_ref, g_ref, off, do_out, dl_out, dg_out in (
                (od_ref, gd_ref, wv, dod_ref, dld_ref, dgd_ref),
                (mo_ref, gm_ref, wv + w4, dom_ref, dlm_ref, dgm_ref)):
            src = src_ref[...]
            dos, dgs = _head_norm_bwd(src, g_ref[...], dm[:, off:off + w4], HEAD_DIM)
            for h in range(HEADS):
                sl = slice(h * HEAD_DIM, (h + 1) * HEAD_DIM)
                do_out[:, sl] = dos[h]
                dl_out[:, sl] = jnp.broadcast_to(
                    jnp.sum(dos[h] * src[:, sl], axis=-1, keepdims=True), (tr, HEAD_DIM))
                dg_out[:, sl] += dgs[h]

    rows = lambda w, c=0: pl.BlockSpec((tr, w), lambda i: (i, c))
    vec = lambda w: pl.BlockSpec((1, w), lambda i: (0, 0))
    sds = lambda w: jax.ShapeDtypeStruct((s, w), F32)
    vds = lambda w: jax.ShapeDtypeStruct((1, w), F32)
    return pl.pallas_call(
        body, name=name, grid=(s // tr,),
        in_specs=[rows(wv + 2 * w4), rows(wv), rows(wv), rows(wv, OFF_GR // wv), rows(w4), rows(w4),
                  vec(wv), vec(w4), vec(w4)],
        out_specs=[rows(wv), rows(wv), rows(w4), rows(w4), rows(w4), rows(w4), vec(wv), vec(w4), vec(w4)],
        out_shape=[sds(wv), sds(wv), sds(w4), sds(w4), sds(w4), sds(w4), vds(wv), vds(w4), vds(w4)],
        compiler_params=_params(("arbitrary",)),
    )(dmixed, o_f, o_b, proj, dil_o, mem_o, g_gla, g_dil, g_mem)


def _colsum(x, name, rows=512):
    s, w = x.shape
    tr = _tile(s, rows, 8)

    def body(x_ref, o_ref):
        @pl.when(pl.program_id(0) == 0)
        def _():
            o_ref[...] = jnp.zeros_like(o_ref)

        o_ref[...] += jnp.sum(x_ref[...], axis=0, keepdims=True)

    return pl.pallas_call(
        body, name=name, grid=(s // tr,),
        in_specs=[pl.BlockSpec((tr, w), lambda i: (i, 0))],
        out_specs=pl.BlockSpec((1, w), lambda i: (0, 0)),
        out_shape=jax.ShapeDtypeStruct((1, w), F32),
        compiler_params=_params(("arbitrary",)),
    )(x)


def _peer(k):
    x, y, c = lax.axis_index("x"), lax.axis_index("y"), lax.axis_index("c")
    kx, ky, kc = (k >> 2) & 1, (k >> 1) & 1, k & 1
    return (x ^ kx if kx else x, y ^ ky if ky else y, c ^ kc if kc else c)


def _my_index():
    return 4 * lax.axis_index("x") + 2 * lax.axis_index("y") + lax.axis_index("c")


def _cargo_shapes(cargo):
    return [jax.ShapeDtypeStruct(x.shape if sc else (N_DEV,) + x.shape, x.dtype) for x, sc in cargo]


def _cargo_sems(n):
    return [pltpu.SemaphoreType.DMA((n * (N_DEV - 1),)), pltpu.SemaphoreType.DMA((n * (N_DEV - 1),)),
            pltpu.SemaphoreType.DMA((n,))]


def _cargo_copies(in_refs, out_refs, sems, scatter, with_arrivals=True):
    send_sems, recv_sems, local_sems = sems
    me = _my_index()
    own, sends, arrivals = [], [], []
    for i, (src_ref, dst_ref) in enumerate(zip(in_refs, out_refs)):
        own.append(pltpu.make_async_copy(src_ref.at[me] if scatter[i] else src_ref, dst_ref.at[me], local_sems.at[i]))
        for k in range(1, N_DEV):
            peer = _peer(k)
            peer_idx = 4 * peer[0] + 2 * peer[1] + peer[2]
            src = src_ref.at[peer_idx] if scatter[i] else src_ref
            sem = i * (N_DEV - 1) + k - 1
            sends.append(pltpu.make_async_remote_copy(
                src_ref=src, dst_ref=dst_ref.at[me], send_sem=send_sems.at[sem], recv_sem=recv_sems.at[sem],
                device_id=peer, device_id_type=MESH))
            if with_arrivals:
                arrivals.append(pltpu.make_async_remote_copy(
                    src_ref=src, dst_ref=dst_ref.at[peer_idx], send_sem=send_sems.at[sem], recv_sem=recv_sems.at[sem],
                    device_id=peer, device_id_type=MESH))
    return own, sends, arrivals


def _cargo_start(in_refs, out_refs, sems, scatter):
    own, sends, _ = _cargo_copies(in_refs, out_refs, sems, scatter, with_arrivals=False)
    for cp in own + sends:
        cp.start()


def _cargo_wait(in_refs, out_refs, sems, scatter):
    own, sends, arrivals = _cargo_copies(in_refs, out_refs, sems, scatter)
    for cp in arrivals:
        cp.wait_recv()
    for cp in sends:
        cp.wait_send()
    for cp in own:
        cp.wait()


def _exchange(cargo, name):
    n = len(cargo)
    scatter = [sc for _, sc in cargo]

    def body(*refs):
        in_refs, out_refs, sems = refs[:n], refs[n:2 * n], refs[2 * n:]
        _cargo_start(in_refs, out_refs, sems, scatter)
        _cargo_wait(in_refs, out_refs, sems, scatter)

    any_spec = pl.BlockSpec(memory_space=pl.ANY)
    return pl.pallas_call(
        body, name=name,
        in_specs=[any_spec] * n, out_specs=[any_spec] * n, out_shape=_cargo_shapes(cargo),
        scratch_shapes=_cargo_sems(n),
        compiler_params=pltpu.CompilerParams(has_side_effects=True),
    )(*[x for x, _ in cargo])


def _adamw(parts, w, m, v, name, rows=256):
    r, c = w.shape
    tr = _tile(r, max(8, min(rows, ADAMW_TILE_ELEMS // c)), 8)
    c1 = 1.0 - ADAM_B1 ** ADAM_STEP
    c2 = 1.0 - ADAM_B2 ** ADAM_STEP

    def body(p_ref, w_ref, m_ref, v_ref, g_ref, d_ref, nm_ref, nv_ref):
        g = p_ref[0].astype(F32)
        for d in range(1, N_DEV):
            g = g + p_ref[d].astype(F32)
        nm = ADAM_B1 * m_ref[...] + (1.0 - ADAM_B1) * g
        nv = ADAM_B2 * v_ref[...] + (1.0 - ADAM_B2) * (g * g)
        m_hat = nm / c1
        v_hat = nv / c2
        g_ref[...] = g
        d_ref[...] = -ADAM_LR * (m_hat / (jnp.sqrt(v_hat) + ADAM_EPS) + ADAM_WD * w_ref[...])
        nm_ref[...] = nm
        nv_ref[...] = nv

    spec = pl.BlockSpec((tr, c), lambda i: (i, 0))
    return pl.pallas_call(
        body, name=name, grid=(r // tr,),
        in_specs=[pl.BlockSpec((N_DEV, tr, c), lambda i: (0, i, 0)), spec, spec, spec],
        out_specs=[spec] * 4,
        out_shape=[jax.ShapeDtypeStruct((r, c), F32)] * 4,
        compiler_params=_params(("parallel",)),
    )(parts, w, m, v)


SMALL = ("norm_mix", "gla_gate_bias_fwd", "gla_gate_bias_bwd", "gla_norm", "rel_bias", "dil_norm", "mem_norm",
         "mem_out_norm", "norm_mlp", "norm_final")


def _pack(arrs, rows):
    flat = jnp.concatenate([a.reshape(-1) for a in arrs])
    return jnp.pad(flat, (0, rows * 128 - flat.shape[0])).reshape(rows, 128)


def _unpack(buf, shapes):
    flat = buf.reshape(-1)
    out, off = [], 0
    for shp in shapes:
        n = int(np.prod(shp))
        out.append(flat[off:off + n].reshape(shp))
        off += n
    return out


def _split_in(w):
    main = jnp.concatenate([w[..., :3072], w[..., 3104:]], axis=-1)
    lr = w[..., 3072:3104]
    pad = [(0, 0)] * (w.ndim - 1) + [(0, LR_PAD - 2 * GLA_RANK)]
    return main, jnp.pad(lr, pad)


def _join_in(main, lr):
    return jnp.concatenate([main[..., :3072], lr[..., :2 * GLA_RANK], main[..., 3072:]], axis=-1)


def kernel(x, mem, norm_mix, w_in, gla_gate_up_fwd, gla_gate_bias_fwd, gla_gate_up_bwd, gla_gate_bias_bwd, gla_norm, rel_bias, dil_norm, mem_norm, w_mem_kv, mem_out_norm, w_out, norm_mlp, w_up, w_down, norm_final, loss_target, m_norm_mix, m_w_in, m_gla_gate_up_fwd, m_gla_gate_bias_fwd, m_gla_gate_up_bwd, m_gla_gate_bias_bwd, m_gla_norm, m_rel_bias, m_dil_norm, m_mem_norm, m_w_mem_kv, m_mem_out_norm, m_w_out, m_norm_mlp, m_w_up, m_w_down, m_norm_final, v_norm_mix, v_w_in, v_gla_gate_up_fwd, v_gla_gate_bias_fwd, v_gla_gate_up_bwd, v_gla_gate_bias_bwd, v_gla_norm, v_rel_bias, v_dil_norm, v_mem_norm, v_w_mem_kv, v_mem_out_norm, v_w_out, v_norm_mlp, v_w_up, v_w_down, v_norm_final):
    weights = dict(norm_mix=norm_mix, w_in=w_in, gla_gate_up_fwd=gla_gate_up_fwd, gla_gate_bias_fwd=gla_gate_bias_fwd,
                   gla_gate_up_bwd=gla_gate_up_bwd, gla_gate_bias_bwd=gla_gate_bias_bwd, gla_norm=gla_norm,
                   rel_bias=rel_bias, dil_norm=dil_norm, mem_norm=mem_norm, w_mem_kv=w_mem_kv,
                   mem_out_norm=mem_out_norm, w_out=w_out, norm_mlp=norm_mlp, w_up=w_up, w_down=w_down,
                   norm_final=norm_final)
    mom1 = dict(norm_mix=m_norm_mix, w_in=m_w_in, gla_gate_up_fwd=m_gla_gate_up_fwd,
                gla_gate_bias_fwd=m_gla_gate_bias_fwd, gla_gate_up_bwd=m_gla_gate_up_bwd,
                gla_gate_bias_bwd=m_gla_gate_bias_bwd, gla_norm=m_gla_norm, rel_bias=m_rel_bias, dil_norm=m_dil_norm,
                mem_norm=m_mem_norm, w_mem_kv=m_w_mem_kv, mem_out_norm=m_mem_out_norm, w_out=m_w_out,
                norm_mlp=m_norm_mlp, w_up=m_w_up, w_down=m_w_down, norm_final=m_norm_final)
    mom2 = dict(norm_mix=v_norm_mix, w_in=v_w_in, gla_gate_up_fwd=v_gla_gate_up_fwd,
                gla_gate_bias_fwd=v_gla_gate_bias_fwd, gla_gate_up_bwd=v_gla_gate_up_bwd,
                gla_gate_bias_bwd=v_gla_gate_bias_bwd, gla_norm=v_gla_norm, rel_bias=v_rel_bias, dil_norm=v_dil_norm,
                mem_norm=v_mem_norm, w_mem_kv=v_w_mem_kv, mem_out_norm=v_mem_out_norm, w_out=v_w_out,
                norm_mlp=v_norm_mlp, w_up=v_w_up, w_down=v_w_down, norm_final=v_norm_final)

    s, d = x.shape[1], x.shape[2]
    xs = x.reshape(s, d)
    mems = mem.reshape(mem.shape[1], d)
    target = loss_target.reshape(s, d)
    me = _my_index()
    n_layers = w_in.shape[0]
    gate_w = gla_gate_up_fwd.shape[2]

    shard = lambda name, l: (weights[name][l].astype(BF16), False)
    cols = lambda t: jnp.moveaxis(t, 0, 1).reshape(t.shape[1], N_DEV * t.shape[2])
    rows = lambda t: t.reshape(N_DEV * t.shape[1], t.shape[2])
    in_names = ("w_in", "gla_gate_up_fwd", "gla_gate_up_bwd")

    def in_mats(g_in, g_upf, g_upb):
        w_main, w_lr = _split_in(cols(g_in))
        up_f, up_b = cols(g_upf), cols(g_upb)
        zeros_up = jnp.zeros((GLA_RANK, HEADS * GLA_DK), BF16)
        pad_rows = jnp.zeros((LR_PAD - 2 * GLA_RANK, HEADS * GLA_DK), BF16)
        up_pad_f = jnp.concatenate([up_f, zeros_up, pad_rows], axis=0)
        up_pad_b = jnp.concatenate([zeros_up, up_b, pad_rows], axis=0)
        return dict(w_main=w_main, w_lr=w_lr, up_pad_f=up_pad_f, up_pad_b=up_pad_b,
                    up_cat=jnp.concatenate([up_pad_f, up_pad_b], axis=1))

    g0 = _exchange([shard(nm, 0) for nm in in_names + ("w_mem_kv", "w_out", "w_up")], "ag_weights_0")
    wts = [dict() for _ in range(n_layers)]
    wts[0].update(in_mats(*g0[:3]), wkv=rows(g0[3]), wout=rows(g0[4]), wup=cols(g0[5]))

    row2 = lambda t: t.reshape(1, -1)

    bands, bias_a, bias_b = [], [], []
    for bi, (window, dilation) in enumerate(DIL_CONFIGS):
        band = _band_buckets(dilation)
        ba, bb = _bias_tiles(band, rel_bias, _dil_tile(s // dilation), f"bias_tiles_{bi}")
        bands.append(band), bias_a.append(ba), bias_b.append(bb)

    saved = []
    xl = xs
    for l in range(n_layers):
        wl = wts[l]
        nxt = l + 1 < n_layers
        h = _rmsnorm_fwd(xl, row2(norm_mix[l]), f"norm_mix_{l}")
        proj, g_down, *g_nxt = _mm(h, wl["w_main"], "nn", [F32], f"proj_{l}", cargo=[shard("w_down", l)] + (
            [shard("w_mem_kv", l + 1), shard("w_out", l + 1)] if nxt else []))
        wl["wdown"] = rows(g_down)
        if nxt:
            wts[l + 1].update(wkv=rows(g_nxt[0]), wout=rows(g_nxt[1]))
        (lr,) = _mm(h, wl["w_lr"], "nn", [F32], f"proj_lr_{l}")
        bias_f, bias_b_ = row2(gla_gate_bias_fwd[l]), row2(gla_gate_bias_bwd[l])
        o_f, st_f = _gla_fwd(proj, lr, wl["up_pad_f"], bias_f, False, f"gla_fwd_f_{l}")
        o_b, st_b = _gla_fwd(proj, lr, wl["up_pad_b"], bias_b_, True, f"gla_fwd_b_{l}")
        dil = [_dil_fwd(proj, bias_a[bi], dil_cfg[1], f"dil_fwd_{bi}_{l}") for bi, dil_cfg in enumerate(DIL_CONFIGS)]
        hm = _rmsnorm_fwd(mems, row2(mem_norm[l]), f"norm_mem_{l}")
        (kv,) = _mm(hm, wl["wkv"], "nn", [F32], f"mem_kv_{l}")
        mem_o, mem_lse = _mem_fwd(proj, kv, f"mem_fwd_{l}")
        mixed, dil_o, dil_lse = _mix_fwd(o_f, o_b, proj, [t[0] for t in dil], [t[1] for t in dil], mem_o,
                                         row2(gla_norm[l]), row2(dil_norm[l]), row2(mem_out_norm[l]), f"mix_fwd_{l}")
        (x1,) = _mm(mixed, wl["wout"], "nn", [F32], f"out_proj_{l}",
                    epilogue=lambda acc, res: (acc + res,), extras=(xl,))
        h2 = _rmsnorm_fwd(x1, row2(norm_mlp[l]), f"norm_mlp_{l}")
        a, u, *g_nxt = _mm(h2, wl["wup"], "nn", [F32, BF16], f"mlp_up_{l}",
                           epilogue=lambda acc: (acc, jnp.square(jnp.maximum(acc, 0.0))),
                           cargo=[shard(nm, l + 1) for nm in in_names] if nxt else [])
        if nxt:
            wts[l + 1].update(in_mats(*g_nxt))
        x2, *g_nxt = _mm(u, wl["wdown"], "nn", [F32], f"mlp_down_{l}",
                         epilogue=lambda acc, res: (acc + res,), extras=(x1,),
                         cargo=[shard("w_up", l + 1)] if nxt else [])
        if nxt:
            wts[l + 1]["wup"] = cols(g_nxt[0])
        saved.append(dict(x0=xl, h=h, proj=proj, lr=lr, o_f=o_f, o_b=o_b, st_f=st_f, st_b=st_b, hm=hm, kv=kv,
                          mem_o=mem_o, mem_lse=mem_lse, mixed=mixed, dil_o=dil_o, dil_lse=dil_lse, x1=x1, h2=h2,
                          a=a, u=u))
        xl = x2

    dx, dx_bf, dg_final, loss_part = _loss_head(xl, row2(norm_final), target, "loss_head")

    to_cols = lambda t: jnp.moveaxis(t.reshape(t.shape[0], N_DEV, -1), 1, 0)
    to_rows = lambda t: t.reshape(N_DEV, -1, t.shape[1])
    tail_names = ("w_mem_kv", "w_in", "gla_gate_up_fwd", "gla_gate_up_bwd")
    recv = {nm: [None] * n_layers for nm in tail_names + ("w_out", "w_up", "w_down")}
    tail = None
    grads_small = {k: [None] * n_layers for k in SMALL}
    dbias_sum = [None] * len(DIL_CONFIGS)
    for l in range(n_layers - 1, -1, -1):
        sv = saved[l]
        wl = wts[l]
        dw_down, *got = _mm(sv["u"], dx_bf, "tn", [BF16], f"dw_down_{l}", cargo=tail or [])
        if tail:
            for nm, part in zip(tail_names, got):
                recv[nm][l + 1] = part
        (da,) = _mm(dx_bf, wl["wdown"], "nt", [BF16], f"d_mlp_act_{l}",
                    epilogue=lambda acc, a_: (acc * (2.0 * jnp.maximum(a_, 0.0)),), extras=(sv["a"],))
        dw_up, recv["w_down"][l] = _mm(sv["h2"], da, "tn", [BF16], f"dw_up_{l}", cargo=[(to_rows(dw_down), True)])
        dh2, recv["w_up"][l] = _mm(da, wl["wup"], "nt", [F32], f"d_norm_mlp_in_{l}", cargo=[(to_cols(dw_up), True)])
        dx1, dx1_bf, dg_mlp = _rmsnorm_bwd(sv["x1"], row2(norm_mlp[l]), dh2, dx, f"norm_mlp_bwd_{l}")
        (dw_out,) = _mm(sv["mixed"], dx1_bf, "tn", [BF16], f"dw_out_{l}")
        dmixed, recv["w_out"][l] = _mm(dx1_bf, wl["wout"], "nt", [F32], f"d_mixed_{l}",
                                       cargo=[(to_rows(dw_out), True)])
        (d_og, d_r, d_od, dl_d, d_om, dl_m, dg_gla, dg_dil, dg_memo) = _mix_bwd(
            dmixed, sv["o_f"], sv["o_b"], sv["proj"], sv["dil_o"], sv["mem_o"],
            row2(gla_norm[l]), row2(dil_norm[l]), row2(mem_out_norm[l]), f"mix_bwd_{l}")
        bias_f, bias_b_ = row2(gla_gate_bias_fwd[l]), row2(gla_gate_bias_bwd[l])
        dq1, dk1, dv1, dz_f = _gla_bwd(sv["proj"], sv["lr"], wl["up_pad_f"], bias_f, sv["st_f"], d_og, None, False,
                                       f"gla_bwd_f_{l}")
        dq_g, dk_g, dv_g, dz_b = _gla_bwd(sv["proj"], sv["lr"], wl["up_pad_b"], bias_b_, sv["st_b"], d_og,
                                          (dq1, dk1, dv1), True, f"gla_bwd_b_{l}")
        dz = jnp.concatenate([dz_f, dz_b], axis=1)
        (d_lr,) = _mm(dz, wl["up_cat"], "nt", [BF16], f"d_lowrank_{l}")
        (d_upcat,) = _mm(sv["lr"], dz, "tn", [BF16], f"dw_gate_up_{l}")
        dzsum = _colsum(dz, f"d_gate_bias_{l}")
        dq_d = dk_d = dv_d = None
        for bi, (window, dilation) in enumerate(DIL_CONFIGS):
            dq_d, dbias = _dil_bwd_q(sv["proj"], d_od, sv["dil_lse"], dl_d, bias_a[bi], dq_d, dilation,
                                     f"dil_bwd_q_{bi}_{l}")
            prev = None if dk_d is None else (dk_d, dv_d)
            dk_d, dv_d = _dil_bwd_kv(sv["proj"], d_od, sv["dil_lse"], dl_d, bias_b[bi], prev, dilation,
                                     f"dil_bwd_kv_{bi}_{l}")
            dbias_sum[bi] = dbias if dbias_sum[bi] is None else dbias_sum[bi] + dbias
        dq_m, dkm, dvm = _mem_bwd(sv["proj"], sv["kv"], d_om, sv["mem_lse"], dl_m, f"mem_bwd_{l}")
        dkv = jnp.concatenate([dkm, dvm], axis=1).astype(BF16)
        (dw_kv,) = _mm(sv["hm"], dkv, "tn", [BF16], f"dw_mem_kv_{l}")
        (dhm,) = _mm(dkv, wl["wkv"], "nt", [F32], f"d_mem_norm_in_{l}")
        _, _, dg_mem = _rmsnorm_bwd(mems, row2(mem_norm[l]), dhm, None, f"norm_mem_bwd_{l}")
        dproj = jnp.concatenate([t.astype(BF16) for t in (dq_g, dk_g, dv_g, d_r, dq_d, dk_d, dv_d, dq_m)], axis=1)
        (dw_main,) = _mm(sv["h"], dproj, "tn", [BF16], f"dw_in_{l}")
        (dw_lr,) = _mm(sv["h"], d_lr, "tn", [BF16], f"dw_in_lr_{l}")
        (dh_lr,) = _mm(d_lr, wl["w_lr"], "nt", [F32], f"d_norm_mix_in_lr_{l}")
        (dh,) = _mm(dproj, wl["w_main"], "nt", [F32], f"d_norm_mix_in_{l}",
                    epilogue=lambda acc, other: (acc + other,), extras=(dh_lr,))
        dx, dx_bf, dg_mix = _rmsnorm_bwd(sv["x0"], row2(norm_mix[l]), dh, dx1, f"norm_mix_bwd_{l}")
        tail = [(to_rows(dw_kv), True), (to_cols(_join_in(dw_main, dw_lr)), True),
                (to_cols(d_upcat[:GLA_RANK, :HEADS * GLA_DK]), True),
                (to_cols(d_upcat[GLA_RANK:2 * GLA_RANK, HEADS * GLA_DK:]), True)]

        grads_small["norm_mix"][l] = dg_mix
        grads_small["gla_gate_bias_fwd"][l] = dzsum[:, :HEADS * GLA_DK]
        grads_small["gla_gate_bias_bwd"][l] = dzsum[:, HEADS * GLA_DK:]
        grads_small["gla_norm"][l] = dg_gla
        grads_small["dil_norm"][l] = dg_dil
        grads_small["mem_norm"][l] = dg_mem
        grads_small["mem_out_norm"][l] = dg_memo
        grads_small["norm_mlp"][l] = dg_mlp

    for nm, part in zip(tail_names, _exchange(tail, "rs_tail")):
        recv[nm][0] = part
    d_table = _bias_grad(bands, dbias_sum, "bias_grad")[:, :HEADS]
    layers = lambda nm: jnp.stack(recv[nm], axis=1)
    r_in, r_upf, r_upb, r_kv = layers("w_in"), layers("gla_gate_up_fwd"), layers("gla_gate_up_bwd"), layers("w_mem_kv")
    r_out, r_up, r_down = layers("w_out"), layers("w_up"), layers("w_down")

    small_shapes = [weights[k].shape for k in SMALL]
    small_grads = []
    for k in SMALL:
        if k == "rel_bias":
            small_grads.append(d_table)
        elif k == "norm_final":
            small_grads.append(dg_final)
        else:
            small_grads.append(jnp.concatenate(grads_small[k], axis=0))
    n_small = sum(int(np.prod(shp)) for shp in small_shapes)
    small_rows = -(-(n_small + 128) // (8 * 128)) * 8
    pack = lambda arrs, extra: _pack(list(arrs) + [extra], small_rows)
    zeros_tail = jnp.zeros((128,), F32)
    (small_parts,) = _exchange([(pack(small_grads, loss_part.reshape(-1)), False)], "ag_small")
    sg, sd, sm, sv_ = _adamw(small_parts, pack([weights[k] for k in SMALL], zeros_tail),
                             pack([mom1[k] for k in SMALL], zeros_tail),
                             pack([mom2[k] for k in SMALL], zeros_tail), "adamw_small")
    loss = sg.reshape(-1)[n_small]
    small_out = [dict(zip(SMALL, _unpack(buf, small_shapes))) for buf in (sg, sd, sm, sv_)]

    def shard_update(parts, name):
        w = weights[name]
        shp = w.shape
        flat = lambda t: t.reshape(-1, shp[-1])
        res = _adamw(parts.reshape(N_DEV, -1, shp[-1]), flat(w), flat(mom1[name]), flat(mom2[name]), f"adamw_{name}")
        return [t.reshape(shp) for t in res]

    big = dict(w_in=shard_update(r_in, "w_in"), gla_gate_up_fwd=shard_update(r_upf, "gla_gate_up_fwd"),
               gla_gate_up_bwd=shard_update(r_upb, "gla_gate_up_bwd"), w_mem_kv=shard_update(r_kv, "w_mem_kv"),
               w_out=shard_update(r_out, "w_out"), w_up=shard_update(r_up, "w_up"),
               w_down=shard_update(r_down, "w_down"))

    order = ("norm_mix", "w_in", "gla_gate_up_fwd", "gla_gate_bias_fwd", "gla_gate_up_bwd", "gla_gate_bias_bwd",
             "gla_norm", "rel_bias", "dil_norm", "mem_norm", "w_mem_kv", "mem_out_norm", "w_out", "norm_mlp", "w_up",
             "w_down", "norm_final")
    outs = [loss, dx.reshape(x.shape)]
    for which in range(4):
        for name in order:
            outs.append(big[name][which] if name in big else small_out[which][name])
    return tuple(outs)
```

```python
import functools
import math

import numpy as np
import jax
import jax.numpy as jnp
from jax import lax
from jax.experimental import pallas as pl
from jax.experimental.pallas import tpu as pltpu

F32 = jnp.float32
BF16 = jnp.bfloat16

N_DEV = 8
DEPTH = 4
HEADS = 4
GLA_DK = 128
GLA_DV = 256
GLA_RANK = 16
GLA_GATE_NORMALIZER = 16.0
GLA_CHUNK = 64
HEAD_DIM = 128
DIL_CONFIGS = ((128, 1), (512, 4), (2048, 16))
DIL_HALF = 64
REL_BUCKETS = 32
REL_MAX_DISTANCE = 1024
EPS = 1e-6
NEG_INF = -1e30
IN_SPLITS = (512, 512, 1024, 1024, 16, 16, 512, 512, 512, 512)
IN_WIDTH = sum(IN_SPLITS)
MAIN_WIDTH = IN_WIDTH - 2 * GLA_RANK
LR_PAD = 128
OFF_GQ, OFF_GK, OFF_GV, OFF_GR, OFF_DQ, OFF_DK, OFF_DV, OFF_MQ = 0, 512, 1024, 2048, 3072, 3584, 4096, 4608

ADAM_LR = 0.001
ADAM_B1 = 0.9
ADAM_B2 = 0.999
ADAM_EPS = 1e-08
ADAM_WD = 0.01
ADAM_STEP = 10

VMEM_LIMIT = 56 * 1024 * 1024
ADAMW_TILE_ELEMS = 128 * 1024
MESH = pl.DeviceIdType.MESH

NN = ((1,), (0,))
NT = ((1,), (1,))
TN = ((0,), (0,))


def _dot(a, b, dims, precision=None):
    return lax.dot_general(a, b, (dims, ((), ())), preferred_element_type=F32, precision=precision)


def _tile(n, pref, mult=128):
    if n <= pref:
        return n
    t = (pref // mult) * mult
    while t >= mult:
        if n % t == 0:
            return t
        t -= mult
    return n


def _params(sem, **kw):
    return pltpu.CompilerParams(dimension_semantics=sem, vmem_limit_bytes=VMEM_LIMIT, **kw)


def _mm(a, b, mode, outs, name, epilogue=None, extras=(), tm=1024, tn=1024, tk=2048, cargo=()):
    if mode == "nn":
        (m, k), (k2, n) = a.shape, b.shape
    elif mode == "nt":
        (m, k), (n, k2) = a.shape, b.shape
    else:
        (k, m), (k2, n) = a.shape, b.shape
    assert k == k2, (a.shape, b.shape, mode)
    tm, tn, tk = _tile(m, tm), _tile(n, tn), _tile(k, tk)
    gi, gj, nk = m // tm, n // tn, k // tk
    if mode == "nn":
        a_spec = pl.BlockSpec((tm, tk), lambda i, j, kk: (i, kk))
        b_spec = pl.BlockSpec((tk, tn), lambda i, j, kk: (kk, j))
        dims = NN
    elif mode == "nt":
        a_spec = pl.BlockSpec((tm, tk), lambda i, j, kk: (i, kk))
        b_spec = pl.BlockSpec((tn, tk), lambda i, j, kk: (j, kk))
        dims = NT
    else:
        a_spec = pl.BlockSpec((tk, tm), lambda i, j, kk: (kk, i))
        b_spec = pl.BlockSpec((tk, tn), lambda i, j, kk: (kk, j))
        dims = TN
    tile_spec = pl.BlockSpec((tm, tn), lambda i, j, kk: (i, j))
    any_spec = pl.BlockSpec(memory_space=pl.ANY)
    n_extra, n_out, n_cargo = len(extras), len(outs), len(cargo)
    scatter = [sc for _, sc in cargo]
    if epilogue is None:
        epilogue = lambda acc: (acc,)

    def body(a_ref, b_ref, *rest):
        extra_refs, rest = rest[:n_extra], rest[n_extra:]
        cargo_in, rest = rest[:n_cargo], rest[n_cargo:]
        out_refs, rest = rest[:n_out], rest[n_out:]
        cargo_out, rest = rest[:n_cargo], rest[n_cargo:]
        i, j, kk = pl.program_id(0), pl.program_id(1), pl.program_id(2)
        if n_cargo:
            sems = rest[-3:]

            @pl.when((i == 0) & (j == 0) & (kk == 0))
            def _():
                _cargo_start(cargo_in, cargo_out, sems, scatter)

        def finish(total):
            res = epilogue(total, *[e[...] for e in extra_refs])
            for o_ref, r in zip(out_refs, res):
                o_ref[...] = r.astype(o_ref.dtype)

        part = _dot(a_ref[...].astype(BF16), b_ref[...].astype(BF16), dims)
        if nk == 1:
            finish(part)
        else:
            acc = rest[0]

            @pl.when(kk == 0)
            def _():
                acc[...] = part

            @pl.when((kk > 0) & (kk < nk - 1))
            def _():
                acc[...] += part

            @pl.when(kk == nk - 1)
            def _():
                finish(acc[...] + part)

        if n_cargo:
            @pl.when((i == gi - 1) & (j == gj - 1) & (kk == nk - 1))
            def _():
                _cargo_wait(cargo_in, cargo_out, sems, scatter)

    scratch = [pltpu.VMEM((tm, tn), F32)] if nk > 1 else []
    if n_cargo:
        scratch += _cargo_sems(n_cargo)
    sem = ("arbitrary",) * 3 if n_cargo else ("parallel", "parallel", "arbitrary")
    return pl.pallas_call(
        body,
        name=name,
        grid=(gi, gj, nk),
        in_specs=[a_spec, b_spec] + [tile_spec] * n_extra + [any_spec] * n_cargo,
        out_specs=[tile_spec] * n_out + [any_spec] * n_cargo,
        out_shape=[jax.ShapeDtypeStruct((m, n), d) for d in outs] + _cargo_shapes(cargo),
        scratch_shapes=scratch,
        compiler_params=_params(sem),
    )(a, b, *extras, *[x for x, _ in cargo])


def _rmsnorm_fwd(x, gain, name, rows=256):
    s, d = x.shape
    tr = _tile(s, rows, 8)

    def body(x_ref, g_ref, h_ref):
        xv = x_ref[...]
        r = lax.rsqrt(jnp.mean(xv * xv, axis=-1, keepdims=True) + EPS)
        h_ref[...] = (xv * r * g_ref[...]).astype(h_ref.dtype)

    return pl.pallas_call(
        body, name=name, grid=(s // tr,),
        in_specs=[pl.BlockSpec((tr, d), lambda i: (i, 0)), pl.BlockSpec((1, d), lambda i: (0, 0))],
        out_specs=pl.BlockSpec((tr, d), lambda i: (i, 0)),
        out_shape=jax.ShapeDtypeStruct((s, d), BF16),
        compiler_params=_params(("parallel",)),
    )(x, gain)


def _rmsnorm_bwd(x, gain, dh, dres, name, rows=256):
    s, d = x.shape
    tr = _tile(s, rows, 8)
    has_res = dres is not None

    def body(*refs):
        if has_res:
            x_ref, g_ref, dh_ref, dres_ref, dx_ref, dxb_ref, dg_ref = refs
        else:
            x_ref, g_ref, dh_ref, dx_ref, dxb_ref, dg_ref = refs
        i = pl.program_id(0)
        xv = x_ref[...]
        dy = dh_ref[...].astype(F32)
        r = lax.rsqrt(jnp.mean(xv * xv, axis=-1, keepdims=True) + EPS)
        xn = xv * r
        t = dy * g_ref[...]
        dx = r * (t - xn * jnp.mean(t * xn, axis=-1, keepdims=True))
        if has_res:
            dx = dx + dres_ref[...]
        dx_ref[...] = dx
        dxb_ref[...] = dx.astype(BF16)

        @pl.when(i == 0)
        def _():
            dg_ref[...] = jnp.zeros_like(dg_ref)

        dg_ref[...] += jnp.sum(dy * xn, axis=0, keepdims=True)

    row_spec = pl.BlockSpec((tr, d), lambda i: (i, 0))
    vec_spec = pl.BlockSpec((1, d), lambda i: (0, 0))
    args = [x, gain, dh] + ([dres] if has_res else [])
    return pl.pallas_call(
        body, name=name, grid=(s // tr,),
        in_specs=[row_spec, vec_spec, row_spec] + ([row_spec] if has_res else []),
        out_specs=[row_spec, row_spec, vec_spec],
        out_shape=[jax.ShapeDtypeStruct((s, d), F32), jax.ShapeDtypeStruct((s, d), BF16),
                   jax.ShapeDtypeStruct((1, d), F32)],
        compiler_params=_params(("arbitrary",)),
    )(*args)


def _loss_head(x, gain, target, name, rows=256):
    s, d = x.shape
    tr = _tile(s, rows, 8)

    def body(x_ref, g_ref, t_ref, dx_ref, dxb_ref, dg_ref, loss_ref):
        i = pl.program_id(0)
        xv = x_ref[...]
        g = g_ref[...]
        r = lax.rsqrt(jnp.mean(xv * xv, axis=-1, keepdims=True) + EPS)
        xn = xv * r
        err = xn * g - t_ref[...]
        dy = err * (1.0 / d)
        t = dy * g
        dx = r * (t - xn * jnp.mean(t * xn, axis=-1, keepdims=True))
        dx_ref[...] = dx
        dxb_ref[...] = dx.astype(BF16)

        @pl.when(i == 0)
        def _():
            dg_ref[...] = jnp.zeros_like(dg_ref)
            loss_ref[...] = jnp.zeros_like(loss_ref)

        dg_ref[...] += jnp.sum(dy * xn, axis=0, keepdims=True)
        part = 0.5 * jnp.sum(jnp.mean(err * err, axis=-1, keepdims=True), axis=0, keepdims=True)
        loss_ref[...] += jnp.broadcast_to(part, loss_ref.shape)

    row_spec = pl.BlockSpec((tr, d), lambda i: (i, 0))
    vec_spec = pl.BlockSpec((1, d), lambda i: (0, 0))
    return pl.pallas_call(
        body, name=name, grid=(s // tr,),
        in_specs=[row_spec, vec_spec, row_spec],
        out_specs=[row_spec, row_spec, vec_spec, pl.BlockSpec((1, 128), lambda i: (0, 0))],
        out_shape=[jax.ShapeDtypeStruct((s, d), F32), jax.ShapeDtypeStruct((s, d), BF16),
                   jax.ShapeDtypeStruct((1, d), F32), jax.ShapeDtypeStruct((1, 128), F32)],
        compiler_params=_params(("arbitrary",)),
    )(x, gain, target)


def _log_sigmoid(z):
    return jnp.minimum(z, 0.0) - jnp.log1p(jnp.exp(-jnp.abs(z)))


def _gla_tri(reverse):
    row = lax.broadcasted_iota(jnp.int32, (GLA_CHUNK, GLA_CHUNK), 0)
    col = lax.broadcasted_iota(jnp.int32, (GLA_CHUNK, GLA_CHUNK), 1)
    return (col >= row) if reverse else (col <= row)


def _gla_rows(s):
    return _tile(s, 256, GLA_CHUNK)


def _gla_chunk_terms(q, k, g, tri, reverse):
    b = _dot(tri.astype(F32), g, NN, precision=lax.Precision.HIGHEST)
    bl = b[0:1] if reverse else b[GLA_CHUNK - 1:GLA_CHUNK]
    qd = q * jnp.exp(b)
    ki = k * jnp.exp(-b)
    ke = k * jnp.exp(bl - b)
    return b, bl, qd, ki, ke


def _gla_fwd(proj, lr, up_pad, bias, reverse, name):
    s = proj.shape[0]
    ts = _gla_rows(s)
    nblk, cpb = s // ts, ts // GLA_CHUNK
    scale = GLA_DK ** -0.5

    def blk(i):
        return (nblk - 1 - i) if reverse else i

    def body(q_ref, k_ref, v_ref, lr_ref, up_ref, b_ref, o_ref, st_ref, state):
        @pl.when(pl.program_id(1) == 0)
        def _():
            state[...] = jnp.zeros_like(state)

        tri = _gla_tri(reverse)
        z = _dot(lr_ref[...].astype(BF16), up_ref[...], NN) + b_ref[...]
        g_all = _log_sigmoid(z) * (1.0 / GLA_GATE_NORMALIZER)
        order = range(cpb - 1, -1, -1) if reverse else range(cpb)
        for c in order:
            sl = slice(c * GLA_CHUNK, (c + 1) * GLA_CHUNK)
            _, bl, qd, ki, ke = _gla_chunk_terms(q_ref[sl, :] * scale, k_ref[sl, :], g_all[sl, :], tri, reverse)
            qdb = qd.astype(BF16)
            a = jnp.where(tri, _dot(qdb, ki.astype(BF16), NT), 0.0)
            vb = v_ref[sl, :].astype(BF16)
            st = state[...]
            o_ref[sl, :] = _dot(a.astype(BF16), vb, NN) + _dot(qdb, st.astype(BF16), NT)
            st_ref[0, c] = st
            state[...] = st * jnp.exp(bl) + _dot(vb, ke.astype(BF16), TN)

    qk = lambda off: pl.BlockSpec((ts, GLA_DK), lambda h, i: (blk(i), off // GLA_DK + h))
    return pl.pallas_call(
        body, name=name, grid=(HEADS, nblk),
        in_specs=[qk(OFF_GQ), qk(OFF_GK),
                  pl.BlockSpec((ts, GLA_DV), lambda h, i: (blk(i), OFF_GV // GLA_DV + h)),
                  pl.BlockSpec((ts, LR_PAD), lambda h, i: (blk(i), 0)),
                  pl.BlockSpec((LR_PAD, GLA_DK), lambda h, i: (0, h)),
                  pl.BlockSpec((1, GLA_DK), lambda h, i: (0, h))],
        out_specs=[pl.BlockSpec((ts, GLA_DV), lambda h, i: (blk(i), h)),
                   pl.BlockSpec((1, cpb, GLA_DV, GLA_DK), lambda h, i: (h, blk(i), 0, 0))],
        out_shape=[jax.ShapeDtypeStruct((s, HEADS * GLA_DV), F32),
                   jax.ShapeDtypeStruct((HEADS, s // GLA_CHUNK, GLA_DV, GLA_DK), F32)],
        scratch_shapes=[pltpu.VMEM((GLA_DV, GLA_DK), F32)],
        compiler_params=_params(("parallel", "arbitrary")),
    )(proj, proj, proj, lr, up_pad, bias)


def _gla_bwd(proj, lr, up_pad, bias, states, d_o, prev, reverse, name):
    s = proj.shape[0]
    ts = _gla_rows(s)
    nblk, cpb = s // ts, ts // GLA_CHUNK
    scale = GLA_DK ** -0.5
    has_prev = prev is not None

    def blk(i):
        return i if reverse else (nblk - 1 - i)

    def body(*refs):
        q_ref, k_ref, v_ref, lr_ref, up_ref, b_ref, st_ref, do_ref = refs[:8]
        refs = refs[8:]
        if has_prev:
            pq_ref, pk_ref, pv_ref = refs[:3]
            refs = refs[3:]
        dq_ref, dk_ref, dv_ref, dz_ref, dstate = refs

        @pl.when(pl.program_id(1) == 0)
        def _():
            dstate[...] = jnp.zeros_like(dstate)

        tri = _gla_tri(reverse)
        tri_t = _gla_tri(not reverse)
        row = lax.broadcasted_iota(jnp.int32, (GLA_CHUNK, GLA_DK), 0)
        last_row = (row == 0) if reverse else (row == GLA_CHUNK - 1)
        z = _dot(lr_ref[...].astype(BF16), up_ref[...], NN) + b_ref[...]
        g_all = _log_sigmoid(z) * (1.0 / GLA_GATE_NORMALIZER)
        dgate = (1.0 / GLA_GATE_NORMALIZER) * (1.0 - jax.nn.sigmoid(z))
        order = range(cpb) if reverse else range(cpb - 1, -1, -1)
        for c in order:
            sl = slice(c * GLA_CHUNK, (c + 1) * GLA_CHUNK)
            b, bl, qd, ki, ke = _gla_chunk_terms(q_ref[sl, :] * scale, k_ref[sl, :], g_all[sl, :], tri, reverse)
            qdb, kib, keb = qd.astype(BF16), ki.astype(BF16), ke.astype(BF16)
            a = jnp.where(tri, _dot(qdb, kib, NT), 0.0)
            vb = v_ref[sl, :].astype(BF16)
            dob = do_ref[sl, :].astype(BF16)
            st = st_ref[0, c]
            dst = dstate[...]
            dstb = dst.astype(BF16)
            da = jnp.where(tri, _dot(dob, vb, NT), 0.0).astype(BF16)
            dv = _dot(a.astype(BF16), dob, TN) + _dot(keb, dstb, NT)
            dqd = _dot(da, kib, NN) + _dot(dob, st.astype(BF16), NN)
            dki = _dot(da, qdb, TN)
            dke = _dot(vb, dstb, NN)
            decay = jnp.exp(bl)
            dbl = decay * jnp.sum(dst * st, axis=0, keepdims=True) + jnp.sum(dke * ke, axis=0, keepdims=True)
            dstate[...] = dst * decay + _dot(dob, qdb, TN)
            db = dqd * qd - dki * ki - dke * ke + jnp.where(last_row, dbl, 0.0)
            dg = _dot(tri_t.astype(F32), db, NN, precision=lax.Precision.HIGHEST)
            dq = dqd * jnp.exp(b) * scale
            dk = dki * jnp.exp(-b) + dke * jnp.exp(bl - b)
            if has_prev:
                dq = dq + pq_ref[sl, :]
                dk = dk + pk_ref[sl, :]
                dv = dv + pv_ref[sl, :]
            dq_ref[sl, :] = dq
            dk_ref[sl, :] = dk
            dv_ref[sl, :] = dv
            dz_ref[sl, :] = dg * dgate[sl, :]

    qk = lambda off: pl.BlockSpec((ts, GLA_DK), lambda h, i: (blk(i), off // GLA_DK + h))
    hk = pl.BlockSpec((ts, GLA_DK), lambda h, i: (blk(i), h))
    hv = pl.BlockSpec((ts, GLA_DV), lambda h, i: (blk(i), h))
    in_specs = [qk(OFF_GQ), qk(OFF_GK),
                pl.BlockSpec((ts, GLA_DV), lambda h, i: (blk(i), OFF_GV // GLA_DV + h)),
                pl.BlockSpec((ts, LR_PAD), lambda h, i: (blk(i), 0)),
                pl.BlockSpec((LR_PAD, GLA_DK), lambda h, i: (0, h)),
                pl.BlockSpec((1, GLA_DK), lambda h, i: (0, h)),
                pl.BlockSpec((1, cpb, GLA_DV, GLA_DK), lambda h, i: (h, blk(i), 0, 0)),
                hv]
    args = [proj, proj, proj, lr, up_pad, bias, states, d_o]
    if has_prev:
        in_specs += [hk, hk, hv]
        args += list(prev)
    return pl.pallas_call(
        body, name=name, grid=(HEADS, nblk),
        in_specs=in_specs,
        out_specs=[hk, hk, hv, hk],
        out_shape=[jax.ShapeDtypeStruct((s, HEADS * GLA_DK), F32), jax.ShapeDtypeStruct((s, HEADS * GLA_DK), F32),
                   jax.ShapeDtypeStruct((s, HEADS * GLA_DV), F32), jax.ShapeDtypeStruct((s, HEADS * GLA_DK), F32)],
        scratch_shapes=[pltpu.VMEM((GLA_DV, GLA_DK), F32)],
        compiler_params=_params(("parallel", "arbitrary")),
    )(*args)


def _dil_tile(l):
    return _tile(l, 256, DIL_HALF)


def t5_bucket(rel):
    half = REL_BUCKETS // 2
    max_exact = half // 2
    ret = jnp.where(rel > 0, half, 0)
    n = jnp.abs(rel)
    nf = jnp.maximum(n, 1).astype(jnp.float32)
    large = max_exact + (jnp.log(nf / max_exact) / math.log(REL_MAX_DISTANCE / max_exact)
                         * (half - max_exact)).astype(jnp.int32)
    large = jnp.minimum(large, half - 1)
    return ret + jnp.where(n < max_exact, n, large)


def _band_buckets(dilation):
    w = DIL_HALF
    rel_sub = jnp.arange(3 * w)[None, :] - w - jnp.arange(w)[:, None]
    return t5_bucket(rel_sub * dilation)[0, :2 * w + 1].astype(jnp.int32)


def _band_offsets(tq):
    tk = tq + 2 * DIL_HALF
    da = lax.broadcasted_iota(jnp.int32, (tq, tk), 1) - lax.broadcasted_iota(jnp.int32, (tq, tk), 0)
    db = (lax.broadcasted_iota(jnp.int32, (tk, tq), 1) - lax.broadcasted_iota(jnp.int32, (tk, tq), 0)
          + 2 * DIL_HALF)
    return jnp.clip(da, 0, 2 * DIL_HALF), jnp.clip(db, 0, 2 * DIL_HALF)


def _bias_tiles(band, table, tq, name):
    tk = tq + 2 * DIL_HALF

    def body(band_ref, t_ref, oa_ref, ob_ref):
        h = pl.program_id(0)
        off_a, off_b = _band_offsets(tq)

        def step(t, carry):
            bkt_a, bkt_b = carry
            bkt = band_ref[t]
            return jnp.where(off_a == t, bkt, bkt_a), jnp.where(off_b == t, bkt, bkt_b)

        bkt_a, bkt_b = lax.fori_loop(0, 2 * DIL_HALF + 1, step,
                                     (jnp.zeros((tq, tk), jnp.int32), jnp.zeros((tk, tq), jnp.int32)))
        acc_a, acc_b = jnp.zeros((tq, tk), F32), jnp.zeros((tk, tq), F32)
        for bkt in range(REL_BUCKETS):
            val = t_ref[bkt, h]
            acc_a = jnp.where(bkt_a == bkt, val, acc_a)
            acc_b = jnp.where(bkt_b == bkt, val, acc_b)
        oa_ref[0] = acc_a
        ob_ref[0] = acc_b

    smem = pl.BlockSpec(memory_space=pltpu.SMEM)
    return pl.pallas_call(
        body, name=name, grid=(HEADS,),
        in_specs=[smem, smem],
        out_specs=[pl.BlockSpec((1, tq, tk), lambda h: (h, 0, 0)), pl.BlockSpec((1, tk, tq), lambda h: (h, 0, 0))],
        out_shape=[jax.ShapeDtypeStruct((HEADS, tq, tk), F32), jax.ShapeDtypeStruct((HEADS, tk, tq), F32)],
        compiler_params=_params(("arbitrary",)),
    )(band, table)


def _bias_grad(bands, dbias_list, name):
    n = len(bands)

    def body(*refs):
        band_refs, db_refs, out_ref = refs[:n], refs[n:2 * n], refs[2 * n]
        row = lax.broadcasted_iota(jnp.int32, (REL_BUCKETS, 128), 0)
        lane = lax.broadcasted_iota(jnp.int32, (REL_BUCKETS, 128), 1)
        acc = jnp.zeros((REL_BUCKETS, 128), F32)
        for band_ref, d_ref in zip(band_refs, db_refs):
            off_a, _ = _band_offsets(d_ref.shape[1])
            for h in range(HEADS):
                def step(t, acc, band_ref=band_ref, d_ref=d_ref, h=h, off_a=off_a):
                    tot = jnp.sum(jnp.where(off_a == t, d_ref[h], 0.0))
                    return acc + jnp.where((row == band_ref[t]) & (lane == h), tot, 0.0)

                acc = lax.fori_loop(0, 2 * DIL_HALF + 1, step, acc)
        out_ref[...] = acc

    vm = pl.BlockSpec(memory_space=pltpu.VMEM)
    smem = pl.BlockSpec(memory_space=pltpu.SMEM)
    return pl.pallas_call(
        body, name=name,
        in_specs=[smem] * n + [vm] * n, out_specs=vm,
        out_shape=jax.ShapeDtypeStruct((REL_BUCKETS, 128), F32),
        compiler_params=pltpu.CompilerParams(vmem_limit_bytes=VMEM_LIMIT),
    )(*bands, *dbias_list)


def _dil_specs(l, tq, dilation, width, off, by_head_first):
    nb64 = l // DIL_HALF
    per = tq // DIL_HALF

    def col(h, r):
        return (r * width + off) // HEAD_DIM + h

    def wrap(f):
        if by_head_first:
            return lambda h, r, n: f(h, r, n)
        return lambda r, h, n: f(h, r, n)

    prev = pl.BlockSpec((DIL_HALF, HEAD_DIM), wrap(lambda h, r, n: (jnp.maximum(n * per - 1, 0), col(h, r))))
    main = pl.BlockSpec((tq, HEAD_DIM), wrap(lambda h, r, n: (n, col(h, r))))
    nxt = pl.BlockSpec((DIL_HALF, HEAD_DIM), wrap(lambda h, r, n: (jnp.minimum((n + 1) * per, nb64 - 1), col(h, r))))
    return prev, main, nxt


def _dil_fwd(proj, bias_a, dilation, name):
    s = proj.shape[0]
    l = s // dilation
    tq = _dil_tile(l)
    tk = tq + 2 * DIL_HALF
    nq = l // tq
    scale = HEAD_DIM ** -0.5
    view = proj.reshape(l, dilation * MAIN_WIDTH)

    def body(q_ref, kp_ref, km_ref, kn_ref, vp_ref, vm_ref, vn_ref, b_ref, o_ref, lse_ref):
        n = pl.program_id(2)
        q = q_ref[...].astype(BF16)
        kc = jnp.concatenate([kp_ref[...], km_ref[...], kn_ref[...]], axis=0).astype(BF16)
        vc = jnp.concatenate([vp_ref[...], vm_ref[...], vn_ref[...]], axis=0).astype(BF16)
        sc = _dot(q, kc, NT) * scale + b_ref[0]
        qpos = n * tq + lax.broadcasted_iota(jnp.int32, (tq, tk), 0)
        kpos = n * tq - DIL_HALF + lax.broadcasted_iota(jnp.int32, (tq, tk), 1)
        mask = (jnp.abs(kpos - qpos) <= DIL_HALF) & (kpos >= 0) & (kpos < l)
        sc = jnp.where(mask, sc, NEG_INF)
        m = jnp.max(sc, axis=-1, keepdims=True)
        p = jnp.exp(sc - m)
        den = jnp.sum(p, axis=-1, keepdims=True)
        o_ref[...] = _dot(p.astype(BF16), vc, NN) / den
        lse_ref[...] = jnp.broadcast_to(m + jnp.log(den), (tq, HEAD_DIM))

    _, q_main, _ = _dil_specs(l, tq, dilation, MAIN_WIDTH, OFF_DQ, False)
    k_specs = _dil_specs(l, tq, dilation, MAIN_WIDTH, OFF_DK, False)
    v_specs = _dil_specs(l, tq, dilation, MAIN_WIDTH, OFF_DV, False)
    _, o_main, _ = _dil_specs(l, tq, dilation, HEADS * HEAD_DIM, 0, False)
    o, lse = pl.pallas_call(
        body, name=name, grid=(dilation, HEADS, nq),
        in_specs=[q_main, *k_specs, *v_specs, pl.BlockSpec((1, tq, tk), lambda r, h, n: (h, 0, 0))],
        out_specs=[o_main, o_main],
        out_shape=[jax.ShapeDtypeStruct((l, dilation * HEADS * HEAD_DIM), F32)] * 2,
        compiler_params=_params(("parallel", "parallel", "parallel")),
    )(view, view, view, view, view, view, view, bias_a)
    return o.reshape(s, HEADS * HEAD_DIM), lse.reshape(s, HEADS * HEAD_DIM)


def _dil_bwd_q(proj, d_o, lse, delta, bias_a, prev_dq, dilation, name):
    s = proj.shape[0]
    l = s // dilation
    tq = _dil_tile(l)
    tk = tq + 2 * DIL_HALF
    nq = l // tq
    scale = HEAD_DIM ** -0.5
    w4 = HEADS * HEAD_DIM
    view = proj.reshape(l, dilation * MAIN_WIDTH)
    small = lambda t: t.reshape(l, dilation * w4)
    has_prev = prev_dq is not None

    def body(*refs):
        q_ref, kp_ref, km_ref, kn_ref, vp_ref, vm_ref, vn_ref, b_ref, do_ref, lse_ref, dl_ref = refs[:11]
        refs = refs[11:]
        if has_prev:
            pq_ref, refs = refs[0], refs[1:]
        dq_ref, db_ref = refs
        r, n = pl.program_id(1), pl.program_id(2)
        q = q_ref[...].astype(BF16)
        kc = jnp.concatenate([kp_ref[...], km_ref[...], kn_ref[...]], axis=0).astype(BF16)
        vc = jnp.concatenate([vp_ref[...], vm_ref[...], vn_ref[...]], axis=0).astype(BF16)
        sc = _dot(q, kc, NT) * scale + b_ref[0]
        qpos = n * tq + lax.broadcasted_iota(jnp.int32, (tq, tk), 0)
        kpos = n * tq - DIL_HALF + lax.broadcasted_iota(jnp.int32, (tq, tk), 1)
        mask = (jnp.abs(kpos - qpos) <= DIL_HALF) & (kpos >= 0) & (kpos < l)
        p = jnp.where(mask, jnp.exp(sc - lse_ref[:, 0:1]), 0.0)
        dp = _dot(do_ref[...].astype(BF16), vc, NT)
        ds = p * (dp - dl_ref[:, 0:1])
        dq = _dot(ds.astype(BF16), kc, NN) * scale
        if has_prev:
            dq = dq + pq_ref[...]
        dq_ref[...] = dq

        @pl.when((r == 0) & (n == 0))
        def _():
            db_ref[...] = jnp.zeros_like(db_ref)

        db_ref[0] += ds

    _, q_main, _ = _dil_specs(l, tq, dilation, MAIN_WIDTH, OFF_DQ, True)
    k_specs = _dil_specs(l, tq, dilation, MAIN_WIDTH, OFF_DK, True)
    v_specs = _dil_specs(l, tq, dilation, MAIN_WIDTH, OFF_DV, True)
    _, o_main, _ = _dil_specs(l, tq, dilation, w4, 0, True)
    bias_spec = pl.BlockSpec((1, tq, tk), lambda h, r, n: (h, 0, 0))
    in_specs = [q_main, *k_specs, *v_specs, bias_spec, o_main, o_main, o_main] + ([o_main] if has_prev else [])
    args = [view] * 7 + [bias_a, small(d_o), small(lse), small(delta)] + ([small(prev_dq)] if has_prev else [])
    dq, dbias = pl.pallas_call(
        body, name=name, grid=(HEADS, dilation, nq),
        in_specs=in_specs,
        out_specs=[o_main, bias_spec],
        out_shape=[jax.ShapeDtypeStruct((l, dilation * w4), F32), jax.ShapeDtypeStruct((HEADS, tq, tk), F32)],
        compiler_params=_params(("arbitrary", "arbitrary", "arbitrary")),
    )(*args)
    return dq.reshape(s, w4), dbias


def _dil_bwd_kv(proj, d_o, lse, delta, bias_b, prev, dilation, name):
    s = proj.shape[0]
    l = s // dilation
    tq = _dil_tile(l)
    tw = tq + 2 * DIL_HALF
    nq = l // tq
    scale = HEAD_DIM ** -0.5
    w4 = HEADS * HEAD_DIM
    view = proj.reshape(l, dilation * MAIN_WIDTH)
    small = lambda t: t.reshape(l, dilation * w4)
    has_prev = prev is not None

    def body(*refs):
        (qp_ref, qm_ref, qn_ref, k_ref, v_ref, b_ref, dop_ref, dom_ref, don_ref,
         lp_ref, lm_ref, ln_ref, dp_ref, dm_ref, dn_ref) = refs[:15]
        refs = refs[15:]
        if has_prev:
            pk_ref, pv_ref = refs[:2]
            refs = refs[2:]
        dk_ref, dv_ref = refs
        n = pl.program_id(2)
        cat = lambda a, b_, c: jnp.concatenate([a[...], b_[...], c[...]], axis=0)
        qc = cat(qp_ref, qm_ref, qn_ref).astype(BF16)
        doc = cat(dop_ref, dom_ref, don_ref).astype(BF16)
        lsec = cat(lp_ref, lm_ref, ln_ref)[:, 0:1]
        dlc = cat(dp_ref, dm_ref, dn_ref)[:, 0:1]
        kb = k_ref[...].astype(BF16)
        vb = v_ref[...].astype(BF16)
        sc = _dot(qc, kb, NT) * scale + b_ref[0]
        qpos = n * tq - DIL_HALF + lax.broadcasted_iota(jnp.int32, (tw, tq), 0)
        kpos = n * tq + lax.broadcasted_iota(jnp.int32, (tw, tq), 1)
        mask = (jnp.abs(kpos - qpos) <= DIL_HALF) & (qpos >= 0) & (qpos < l)
        p = jnp.where(mask, jnp.exp(sc - lsec), 0.0)
        dv = _dot(p.astype(BF16), doc, TN)
        dp = _dot(doc, vb, NT)
        ds = p * (dp - dlc)
        dk = _dot(ds.astype(BF16), qc, TN) * scale
        if has_prev:
            dk = dk + pk_ref[...]
            dv = dv + pv_ref[...]
        dk_ref[...] = dk
        dv_ref[...] = dv

    q_specs = _dil_specs(l, tq, dilation, MAIN_WIDTH, OFF_DQ, False)
    _, k_main, _ = _dil_specs(l, tq, dilation, MAIN_WIDTH, OFF_DK, False)
    _, v_main, _ = _dil_specs(l, tq, dilation, MAIN_WIDTH, OFF_DV, False)
    o_specs = _dil_specs(l, tq, dilation, w4, 0, False)
    o_main = o_specs[1]
    in_specs = [*q_specs, k_main, v_main, pl.BlockSpec((1, tw, tq), lambda r, h, n: (h, 0, 0)),
                *o_specs, *o_specs, *o_specs] + ([o_main, o_main] if has_prev else [])
    args = ([view] * 5 + [bias_b] + [small(d_o)] * 3 + [small(lse)] * 3 + [small(delta)] * 3
            + ([small(prev[0]), small(prev[1])] if has_prev else []))
    dk, dv = pl.pallas_call(
        body, name=name, grid=(dilation, HEADS, nq),
        in_specs=in_specs,
        out_specs=[o_main, o_main],
        out_shape=[jax.ShapeDtypeStruct((l, dilation * w4), F32)] * 2,
        compiler_params=_params(("parallel", "parallel", "parallel")),
    )(*args)
    return dk.reshape(s, w4), dv.reshape(s, w4)


def _pcall(body, name, grid, in_specs, out_specs, out_shape, scratch_shapes, sem, args, cargo=()):
    n_in, n_out, n_c = len(in_specs), len(out_specs), len(cargo)
    if not n_c:
        return pl.pallas_call(body, name=name, grid=grid, in_specs=in_specs, out_specs=out_specs, out_shape=out_shape,
                              scratch_shapes=scratch_shapes, compiler_params=_params(sem))(*args)
    scatter = [sc for _, sc in cargo]
    n_scr = len(scratch_shapes)

    def wrapped(*refs):
        ins, refs = refs[:n_in], refs[n_in:]
        c_in, refs = refs[:n_c], refs[n_c:]
        outs, refs = refs[:n_out], refs[n_out:]
        c_out, refs = refs[:n_c], refs[n_c:]
        scr, sems = refs[:n_scr], refs[n_scr:]
        ids = [pl.program_id(a) for a in range(len(grid))]
        first = functools.reduce(lambda p, q: p & q, [i == 0 for i in ids])
        last = functools.reduce(lambda p, q: p & q, [i == g - 1 for i, g in zip(ids, grid)])

        @pl.when(first)
        def _():
            _cargo_start(c_in, c_out, sems, scatter)

        body(*ins, *outs, *scr)

        @pl.when(last)
        def _():
            _cargo_wait(c_in, c_out, sems, scatter)

    any_spec = pl.BlockSpec(memory_space=pl.ANY)
    return pl.pallas_call(
        wrapped, name=name, grid=grid,
        in_specs=list(in_specs) + [any_spec] * n_c, out_specs=list(out_specs) + [any_spec] * n_c,
        out_shape=list(out_shape) + _cargo_shapes(cargo),
        scratch_shapes=list(scratch_shapes) + _cargo_sems(n_c),
        compiler_params=_params(("arbitrary",) * len(grid)),
    )(*args, *[x for x, _ in cargo])


def _gla4_fwd(proj, lr, up_pad, bias, reverse, name, cargo=()):
    s = proj.shape[0]
    ts = _gla_rows(s)
    nblk, cpb = s // ts, ts // GLA_CHUNK
    scale = GLA_DK ** -0.5
    wk, wv = HEADS * GLA_DK, HEADS * GLA_DV

    def blk(i):
        return (nblk - 1 - i) if reverse else i

    def body(q_ref, k_ref, v_ref, lr_ref, up_ref, b_ref, o_ref, st_ref, state):
        @pl.when(pl.program_id(0) == 0)
        def _():
            state[...] = jnp.zeros_like(state)

        tri = _gla_tri(reverse)
        z = _dot(lr_ref[...].astype(BF16), up_ref[...], NN) + b_ref[...]
        g_all = _log_sigmoid(z) * (1.0 / GLA_GATE_NORMALIZER)
        order = range(cpb - 1, -1, -1) if reverse else range(cpb)
        for c in order:
            sl = slice(c * GLA_CHUNK, (c + 1) * GLA_CHUNK)
            b_all = _dot(tri.astype(F32), g_all[sl, :], NN, precision=lax.Precision.HIGHEST)
            for h in range(HEADS):
                hk = slice(h * GLA_DK, (h + 1) * GLA_DK)
                hv = slice(h * GLA_DV, (h + 1) * GLA_DV)
                b = b_all[:, hk]
                bl = b[0:1] if reverse else b[GLA_CHUNK - 1:GLA_CHUNK]
                kc = k_ref[sl, hk]
                qdb = (q_ref[sl, hk] * scale * jnp.exp(b)).astype(BF16)
                ki = kc * jnp.exp(-b)
                ke = kc * jnp.exp(bl - b)
                a = jnp.where(tri, _dot(qdb, ki.astype(BF16), NT), 0.0)
                vb = v_ref[sl, hv].astype(BF16)
                st = state[h]
                o_ref[sl, hv] = _dot(a.astype(BF16), vb, NN) + _dot(qdb, st.astype(BF16), NT)
                st_ref[h, c] = st
                state[h] = st * jnp.exp(bl) + _dot(vb, ke.astype(BF16), TN)

    return _pcall(
        body, name, (nblk,),
        [pl.BlockSpec((ts, wk), lambda i: (blk(i), OFF_GQ // wk)), pl.BlockSpec((ts, wk), lambda i: (blk(i), OFF_GK // wk)),
         pl.BlockSpec((ts, wv), lambda i: (blk(i), OFF_GV // wv)), pl.BlockSpec((ts, LR_PAD), lambda i: (blk(i), 0)),
         pl.BlockSpec((LR_PAD, wk), lambda i: (0, 0)), pl.BlockSpec((1, wk), lambda i: (0, 0))],
        [pl.BlockSpec((ts, wv), lambda i: (blk(i), 0)),
         pl.BlockSpec((HEADS, cpb, GLA_DV, GLA_DK), lambda i: (0, blk(i), 0, 0))],
        [jax.ShapeDtypeStruct((s, wv), F32), jax.ShapeDtypeStruct((HEADS, s // GLA_CHUNK, GLA_DV, GLA_DK), F32)],
        [pltpu.VMEM((HEADS, GLA_DV, GLA_DK), F32)], ("arbitrary",),
        (proj, proj, proj, lr, up_pad, bias), cargo)


def _gla4_bwd(proj, lr, up_pad, bias, states, d_o, prev, reverse, name, cargo=()):
    s = proj.shape[0]
    ts = _gla_rows(s)
    nblk, cpb = s // ts, ts // GLA_CHUNK
    scale = GLA_DK ** -0.5
    wk, wv = HEADS * GLA_DK, HEADS * GLA_DV
    has_prev = prev is not None

    def blk(i):
        return i if reverse else (nblk - 1 - i)

    def body(*refs):
        q_ref, k_ref, v_ref, lr_ref, up_ref, b_ref, st_ref, do_ref = refs[:8]
        refs = refs[8:]
        if has_prev:
            pq_ref, pk_ref, pv_ref = refs[:3]
            refs = refs[3:]
        dq_ref, dk_ref, dv_ref, dz_ref, dstate = refs

        @pl.when(pl.program_id(0) == 0)
        def _():
            dstate[...] = jnp.zeros_like(dstate)

        tri = _gla_tri(reverse)
        tri_t = _gla_tri(not reverse)
        row = lax.broadcasted_iota(jnp.int32, (GLA_CHUNK, GLA_DK), 0)
        last_row = (row == 0) if reverse else (row == GLA_CHUNK - 1)
        z = _dot(lr_ref[...].astype(BF16), up_ref[...], NN) + b_ref[...]
        g_all = _log_sigmoid(z) * (1.0 / GLA_GATE_NORMALIZER)
        dgate = (1.0 / GLA_GATE_NORMALIZER) * (1.0 - jax.nn.sigmoid(z))
        order = range(cpb) if reverse else range(cpb - 1, -1, -1)
        for c in order:
            sl = slice(c * GLA_CHUNK, (c + 1) * GLA_CHUNK)
            b_all = _dot(tri.astype(F32), g_all[sl, :], NN, precision=lax.Precision.HIGHEST)
            for h in range(HEADS):
                hk = slice(h * GLA_DK, (h + 1) * GLA_DK)
                hv = slice(h * GLA_DV, (h + 1) * GLA_DV)
                b = b_all[:, hk]
                bl = b[0:1] if reverse else b[GLA_CHUNK - 1:GLA_CHUNK]
                eb = jnp.exp(b)
                kc = k_ref[sl, hk]
                qd = q_ref[sl, hk] * scale * eb
                ki = kc * jnp.exp(-b)
                ke = kc * jnp.exp(bl - b)
                qdb, kib, keb = qd.astype(BF16), ki.astype(BF16), ke.astype(BF16)
                a = jnp.where(tri, _dot(qdb, kib, NT), 0.0)
                vb = v_ref[sl, hv].astype(BF16)
                dob = do_ref[sl, hv].astype(BF16)
                st = st_ref[h, c]
                dst = dstate[h]
                dstb = dst.astype(BF16)
                da = jnp.where(tri, _dot(dob, vb, NT), 0.0).astype(BF16)
                dv = _dot(a.astype(BF16), dob, TN) + _dot(keb, dstb, NT)
                dqd = _dot(da, kib, NN) + _dot(dob, st.astype(BF16), NN)
                dki = _dot(da, qdb, TN)
                dke = _dot(vb, dstb, NN)
                decay = jnp.exp(bl)
                dbl = decay * jnp.sum(dst * st, axis=0, keepdims=True) + jnp.sum(dke * ke, axis=0, keepdims=True)
                dstate[h] = dst * decay + _dot(dob, qdb, TN)
                db = dqd * qd - dki * ki - dke * ke + jnp.where(last_row, dbl, 0.0)
                dg = _dot(tri_t.astype(F32), db, NN, precision=lax.Precision.HIGHEST)
                dq = dqd * eb * scale
                dk = dki * jnp.exp(-b) + dke * jnp.exp(bl - b)
                if has_prev:
                    dq = dq + pq_ref[sl, hk]
                    dk = dk + pk_ref[sl, hk]
                    dv = dv + pv_ref[sl, hv]
                dq_ref[sl, hk] = dq
                dk_ref[sl, hk] = dk
                dv_ref[sl, hv] = dv
                dz_ref[sl, hk] = dg * dgate[sl, hk]

    rk = pl.BlockSpec((ts, wk), lambda i: (blk(i), 0))
    rv = pl.BlockSpec((ts, wv), lambda i: (blk(i), 0))
    in_specs = [pl.BlockSpec((ts, wk), lambda i: (blk(i), OFF_GQ // wk)),
                pl.BlockSpec((ts, wk), lambda i: (blk(i), OFF_GK // wk)),
                pl.BlockSpec((ts, wv), lambda i: (blk(i), OFF_GV // wv)),
                pl.BlockSpec((ts, LR_PAD), lambda i: (blk(i), 0)),
                pl.BlockSpec((LR_PAD, wk), lambda i: (0, 0)), pl.BlockSpec((1, wk), lambda i: (0, 0)),
                pl.BlockSpec((HEADS, cpb, GLA_DV, GLA_DK), lambda i: (0, blk(i), 0, 0)), rv]
    args = [proj, proj, proj, lr, up_pad, bias, states, d_o]
    if has_prev:
        in_specs += [rk, rk, rv]
        args += list(prev)
    return _pcall(
        body, name, (nblk,), in_specs, [rk, rk, rv, rk],
        [jax.ShapeDtypeStruct((s, wk), F32), jax.ShapeDtypeStruct((s, wk), F32),
         jax.ShapeDtypeStruct((s, wv), F32), jax.ShapeDtypeStruct((s, wk), F32)],
        [pltpu.VMEM((HEADS, GLA_DV, GLA_DK), F32)], ("arbitrary",), args, cargo)


DILATIONS = tuple(d for _, d in DIL_CONFIGS)
DIL_HALO = DIL_HALF * max(DILATIONS)


def _dilf_block(s):
    return min(s, DIL_HALO)


def _dilf_tq(block, dilation):
    return min(128, block // dilation)


def _dilf_specs(s, block, off, width):
    nb = s // block
    col = lambda h: off // HEAD_DIM + h
    prev = pl.BlockSpec((block, HEAD_DIM), lambda h, n: (jnp.maximum(n - 1, 0), col(h)))
    main = pl.BlockSpec((block, HEAD_DIM), lambda h, n: (n, col(h)))
    nxt = pl.BlockSpec((block, HEAD_DIM), lambda h, n: (jnp.minimum(n + 1, nb - 1), col(h)))
    return prev, main, nxt


def _dilf_rows(start, count, dilation):
    if dilation == 1:
        return pl.ds(pl.multiple_of(start, 8), count)
    return pl.ds(start, count, stride=dilation)


def _dilf_fwd(proj, biases, name, cargo=()):
    s = proj.shape[0]
    blk = _dilf_block(s)
    assert s % blk == 0 and blk == DIL_HALO, s
    halo = blk
    scale = HEAD_DIM ** -0.5
    w4 = HEADS * HEAD_DIM
    nbr = len(DILATIONS)

    def body(q_ref, kp_ref, km_ref, kn_ref, vp_ref, vm_ref, vn_ref, *rest):
        b_refs, (o_ref, lse_ref, kw, vw, o_scr, l_scr) = rest[:nbr], rest[nbr:]
        p0 = pl.program_id(1) * blk
        for w_ref, parts in ((kw, (kp_ref, km_ref, kn_ref)), (vw, (vp_ref, vm_ref, vn_ref))):
            w_ref[0:halo, :] = parts[0][...]
            w_ref[halo:halo + blk, :] = parts[1][...]
            w_ref[halo + blk:, :] = parts[2][...]
        for bi, d in enumerate(DILATIONS):
            tq = _dilf_tq(blk, d)
            tk = tq + 2 * DIL_HALF
            ii = lax.broadcasted_iota(jnp.int32, (tq, tk), 0)
            jj = lax.broadcasted_iota(jnp.int32, (tq, tk), 1)
            band = jnp.abs(jj - DIL_HALF - ii) <= DIL_HALF
            bias = b_refs[bi][0]

            def tile(i, carry, d=d, tq=tq, tk=tk, band=band, bias=bias, jj=jj, bi=bi):
                start = (i % d) + d * tq * (i // d)
                wstart = halo - DIL_HALF * d + start
                q = q_ref[_dilf_rows(start, tq, d), :].astype(BF16)
                k = kw[_dilf_rows(wstart, tk, d), :].astype(BF16)
                v = vw[_dilf_rows(wstart, tk, d), :].astype(BF16)
                kpos = p0 + start + d * (jj - DIL_HALF)
                mask = band & (kpos >= 0) & (kpos < s)
                sc = jnp.where(mask, _dot(q, k, NT) * scale + bias, NEG_INF)
                m = jnp.max(sc, axis=-1, keepdims=True)
                p = jnp.exp(sc - m)
                den = jnp.sum(p, axis=-1, keepdims=True)
                o_scr[bi, _dilf_rows(start, tq, d), :] = _dot(p.astype(BF16), v, NN) / den
                l_scr[bi, _dilf_rows(start, tq, d), :] = jnp.broadcast_to(m + jnp.log(den), (tq, HEAD_DIM))
                return carry

            lax.fori_loop(0, blk // tq, tile, 0)
        ls = [l_scr[bi] for bi in range(nbr)]
        m = functools.reduce(jnp.maximum, ls)
        es = [jnp.exp(l - m) for l in ls]
        den = functools.reduce(lambda a_, b_: a_ + b_, es)
        num = functools.reduce(lambda a_, b_: a_ + b_, [e * o_scr[bi] for bi, e in enumerate(es)])
        o_ref[...] = num / den
        lse_ref[...] = m + jnp.log(den)

    _, q_main, _ = _dilf_specs(s, blk, OFF_DQ, MAIN_WIDTH)
    k_specs = _dilf_specs(s, blk, OFF_DK, MAIN_WIDTH)
    v_specs = _dilf_specs(s, blk, OFF_DV, MAIN_WIDTH)
    _, o_main, _ = _dilf_specs(s, blk, 0, w4)
    b_specs = [pl.BlockSpec((1,) + b.shape[1:], lambda h, n: (h, 0, 0)) for b in biases]
    return _pcall(
        body, name, (HEADS, s // blk), [q_main, *k_specs, *v_specs, *b_specs], [o_main, o_main],
        [jax.ShapeDtypeStruct((s, w4), F32)] * 2,
        [pltpu.VMEM((blk + 2 * halo, HEAD_DIM), F32)] * 2 + [pltpu.VMEM((nbr, blk, HEAD_DIM), F32)] * 2,
        ("parallel", "parallel"), [proj] * 7 + list(biases), cargo)


def _dilf_bwd_q(proj, d_o, lse, delta, biases, name, cargo=()):
    s = proj.shape[0]
    blk = _dilf_block(s)
    halo = blk
    scale = HEAD_DIM ** -0.5
    w4 = HEADS * HEAD_DIM
    nbr = len(DILATIONS)

    def body(q_ref, kp_ref, km_ref, kn_ref, vp_ref, vm_ref, vn_ref, do_ref, lse_ref, dl_ref, *rest):
        b_refs, rest = rest[:nbr], rest[nbr:]
        dq_ref, db_refs, (kw, vw) = rest[0], rest[1:1 + nbr], rest[1 + nbr:]
        n = pl.program_id(1)
        p0 = n * blk
        for w_ref, parts in ((kw, (kp_ref, km_ref, kn_ref)), (vw, (vp_ref, vm_ref, vn_ref))):
            w_ref[0:halo, :] = parts[0][...]
            w_ref[halo:halo + blk, :] = parts[1][...]
            w_ref[halo + blk:, :] = parts[2][...]
        dq_ref[...] = jnp.zeros_like(dq_ref)
        for bi, d in enumerate(DILATIONS):
            tq = _dilf_tq(blk, d)
            tk = tq + 2 * DIL_HALF
            ii = lax.broadcasted_iota(jnp.int32, (tq, tk), 0)
            jj = lax.broadcasted_iota(jnp.int32, (tq, tk), 1)
            band = jnp.abs(jj - DIL_HALF - ii) <= DIL_HALF
            bias = b_refs[bi][0]
            db_ref = db_refs[bi]

            @pl.when(n == 0)
            def _(db_ref=db_ref):
                db_ref[...] = jnp.zeros_like(db_ref)

            def tile(i, carry, d=d, tq=tq, tk=tk, band=band, bias=bias, jj=jj, db_ref=db_ref):
                start = (i % d) + d * tq * (i // d)
                wstart = halo - DIL_HALF * d + start
                rows = _dilf_rows(start, tq, d)
                q = q_ref[rows, :].astype(BF16)
                k = kw[_dilf_rows(wstart, tk, d), :].astype(BF16)
                v = vw[_dilf_rows(wstart, tk, d), :].astype(BF16)
                kpos = p0 + start + d * (jj - DIL_HALF)
                mask = band & (kpos >= 0) & (kpos < s)
                sc = _dot(q, k, NT) * scale + bias
                p = jnp.where(mask, jnp.exp(sc - lse_ref[rows, :][:, 0:1]), 0.0)
                dp = _dot(do_ref[rows, :].astype(BF16), v, NT)
                ds = p * (dp - dl_ref[rows, :][:, 0:1])
                dq_ref[rows, :] += _dot(ds.astype(BF16), k, NN) * scale
                db_ref[0] += ds
                return carry

            lax.fori_loop(0, blk // tq, tile, 0)

    _, q_main, _ = _dilf_specs(s, blk, OFF_DQ, MAIN_WIDTH)
    k_specs = _dilf_specs(s, blk, OFF_DK, MAIN_WIDTH)
    v_specs = _dilf_specs(s, blk, OFF_DV, MAIN_WIDTH)
    _, o_main, _ = _dilf_specs(s, blk, 0, w4)
    b_specs = [pl.BlockSpec((1,) + b.shape[1:], lambda h, n: (h, 0, 0)) for b in biases]
    return _pcall(
        body, name, (HEADS, s // blk), [q_main, *k_specs, *v_specs, o_main, o_main, o_main, *b_specs],
        [o_main, *b_specs],
        [jax.ShapeDtypeStruct((s, w4), F32)] + [jax.ShapeDtypeStruct(b.shape, F32) for b in biases],
        [pltpu.VMEM((blk + 2 * halo, HEAD_DIM), F32)] * 2,
        ("arbitrary", "arbitrary"), [proj] * 7 + [d_o, lse, delta] + list(biases), cargo)


def _dilf_bwd_kv(proj, d_o, lse, delta, biases_b, name, cargo=()):
    s = proj.shape[0]
    blk = _dilf_block(s)
    halo = blk
    scale = HEAD_DIM ** -0.5
    w4 = HEADS * HEAD_DIM
    nbr = len(DILATIONS)

    def body(qp_ref, qm_ref, qn_ref, k_ref, v_ref, dop_ref, dom_ref, don_ref, lp_ref, lm_ref, ln_ref,
             dp_ref, dm_ref, dn_ref, *rest):
        b_refs, (dk_ref, dv_ref, qw, dow, lw, dlw) = rest[:nbr], rest[nbr:]
        p0 = pl.program_id(1) * blk
        for w_ref, parts in ((qw, (qp_ref, qm_ref, qn_ref)), (dow, (dop_ref, dom_ref, don_ref)),
                             (lw, (lp_ref, lm_ref, ln_ref)), (dlw, (dp_ref, dm_ref, dn_ref))):
            w_ref[0:halo, :] = parts[0][...]
            w_ref[halo:halo + blk, :] = parts[1][...]
            w_ref[halo + blk:, :] = parts[2][...]
        dk_ref[...] = jnp.zeros_like(dk_ref)
        dv_ref[...] = jnp.zeros_like(dv_ref)
        for bi, d in enumerate(DILATIONS):
            tq = _dilf_tq(blk, d)
            tw = tq + 2 * DIL_HALF
            ii = lax.broadcasted_iota(jnp.int32, (tw, tq), 0)
            jj = lax.broadcasted_iota(jnp.int32, (tw, tq), 1)
            band = jnp.abs(jj + DIL_HALF - ii) <= DIL_HALF
            bias = b_refs[bi][0]

            def tile(i, carry, d=d, tq=tq, tw=tw, band=band, bias=bias, ii=ii):
                start = (i % d) + d * tq * (i // d)
                wstart = halo - DIL_HALF * d + start
                rows = _dilf_rows(start, tq, d)
                wrows = _dilf_rows(wstart, tw, d)
                kb = k_ref[rows, :].astype(BF16)
                vb = v_ref[rows, :].astype(BF16)
                qc = qw[wrows, :].astype(BF16)
                doc = dow[wrows, :].astype(BF16)
                qpos = p0 + start + d * (ii - DIL_HALF)
                mask = band & (qpos >= 0) & (qpos < s)
                sc = _dot(qc, kb, NT) * scale + bias
                p = jnp.where(mask, jnp.exp(sc - lw[wrows, :][:, 0:1]), 0.0)
                dp = _dot(doc, vb, NT)
                ds = p * (dp - dlw[wrows, :][:, 0:1])
                dv_ref[rows, :] += _dot(p.astype(BF16), doc, TN)
                dk_ref[rows, :] += _dot(ds.astype(BF16), qc, TN) * scale
                return carry

            lax.fori_loop(0, blk // tq, tile, 0)

    q_specs = _dilf_specs(s, blk, OFF_DQ, MAIN_WIDTH)
    _, k_main, _ = _dilf_specs(s, blk, OFF_DK, MAIN_WIDTH)
    _, v_main, _ = _dilf_specs(s, blk, OFF_DV, MAIN_WIDTH)
    o_specs = _dilf_specs(s, blk, 0, w4)
    b_specs = [pl.BlockSpec((1,) + b.shape[1:], lambda h, n: (h, 0, 0)) for b in biases_b]
    return _pcall(
        body, name, (HEADS, s // blk), [*q_specs, k_main, v_main, *o_specs, *o_specs, *o_specs, *b_specs],
        [o_specs[1], o_specs[1]], [jax.ShapeDtypeStruct((s, w4), F32)] * 2,
        [pltpu.VMEM((blk + 2 * halo, HEAD_DIM), F32)] * 4,
        ("parallel", "parallel"), [proj] * 5 + [d_o] * 3 + [lse] * 3 + [delta] * 3 + list(biases_b), cargo)


def _mem_fwd(proj, kv, name):
    s = proj.shape[0]
    mlen = kv.shape[0]
    tq = _tile(s, 512, 8)
    scale = HEAD_DIM ** -0.5
    w4 = HEADS * HEAD_DIM

    def body(q_ref, k_ref, v_ref, o_ref, lse_ref):
        sc = _dot(q_ref[...].astype(BF16), k_ref[...].astype(BF16), NT) * scale
        m = jnp.max(sc, axis=-1, keepdims=True)
        e = jnp.exp(sc - m)
        den = jnp.sum(e, axis=-1, keepdims=True)
        o_ref[...] = _dot((e / den).astype(BF16), v_ref[...].astype(BF16), NN)
        lse_ref[...] = jnp.broadcast_to(m + jnp.log(den), (tq, HEAD_DIM))

    o_spec = pl.BlockSpec((tq, HEAD_DIM), lambda h, n: (n, h))
    return pl.pallas_call(
        body, name=name, grid=(HEADS, s // tq),
        in_specs=[pl.BlockSpec((tq, HEAD_DIM), lambda h, n: (n, OFF_MQ // HEAD_DIM + h)),
                  pl.BlockSpec((mlen, HEAD_DIM), lambda h, n: (0, h)),
                  pl.BlockSpec((mlen, HEAD_DIM), lambda h, n: (0, HEADS + h))],
        out_specs=[o_spec, o_spec],
        out_shape=[jax.ShapeDtypeStruct((s, w4), F32)] * 2,
        compiler_params=_params(("parallel", "parallel")),
    )(proj, kv, kv)


def _mem_bwd(proj, kv, d_o, lse, delta, name):
    s = proj.shape[0]
    mlen = kv.shape[0]
    tq = _tile(s, 512, 8)
    scale = HEAD_DIM ** -0.5
    w4 = HEADS * HEAD_DIM

    def body(q_ref, k_ref, v_ref, do_ref, lse_ref, dl_ref, dq_ref, dk_ref, dv_ref):
        qb = q_ref[...].astype(BF16)
        kb = k_ref[...].astype(BF16)
        dob = do_ref[...].astype(BF16)
        sc = _dot(qb, kb, NT) * scale
        p = jnp.exp(sc - lse_ref[:, 0:1])
        dp = _dot(dob, v_ref[...].astype(BF16), NT)
        ds = (p * (dp - dl_ref[:, 0:1])).astype(BF16)
        dq_ref[...] = _dot(ds, kb, NN) * scale

        @pl.when(pl.program_id(1) == 0)
        def _():
            dk_ref[...] = jnp.zeros_like(dk_ref)
            dv_ref[...] = jnp.zeros_like(dv_ref)

        dk_ref[...] += _dot(ds, qb, TN) * scale
        dv_ref[...] += _dot(p.astype(BF16), dob, TN)

    o_spec = pl.BlockSpec((tq, HEAD_DIM), lambda h, n: (n, h))
    k_spec = pl.BlockSpec((mlen, HEAD_DIM), lambda h, n: (0, h))
    v_spec = pl.BlockSpec((mlen, HEAD_DIM), lambda h, n: (0, HEADS + h))
    dq, dkv, dkv2 = pl.pallas_call(
        body, name=name, grid=(HEADS, s // tq),
        in_specs=[pl.BlockSpec((tq, HEAD_DIM), lambda h, n: (n, OFF_MQ // HEAD_DIM + h)),
                  k_spec, v_spec, o_spec, o_spec, o_spec],
        out_specs=[o_spec, k_spec, k_spec],
        out_shape=[jax.ShapeDtypeStruct((s, w4), F32), jax.ShapeDtypeStruct((mlen, w4), F32),
                   jax.ShapeDtypeStruct((mlen, w4), F32)],
        compiler_params=_params(("parallel", "arbitrary")),
    )(proj, kv, kv, d_o, lse, delta)
    return dq, dkv, dkv2


def _head_norm(o, gain, width):
    outs, xns = [], []
    for h in range(HEADS):
        oh = o[:, h * width:(h + 1) * width]
        r = lax.rsqrt(jnp.mean(oh * oh, axis=-1, keepdims=True) + EPS)
        xn = oh * r
        xns.append(xn)
        outs.append(xn * gain[:, h * width:(h + 1) * width])
    return outs, xns


def _head_norm_bwd(o, gain, dy, width):
    dos, dgs = [], []
    for h in range(HEADS):
        sl = slice(h * width, (h + 1) * width)
        oh = o[:, sl]
        r = lax.rsqrt(jnp.mean(oh * oh, axis=-1, keepdims=True) + EPS)
        xn = oh * r
        t = dy[:, sl] * gain[:, sl]
        dos.append(r * (t - xn * jnp.mean(t * xn, axis=-1, keepdims=True)))
        dgs.append(jnp.sum(dy[:, sl] * xn, axis=0, keepdims=True))
    return dos, dgs


def _mix_fwd(o_f, o_b, proj, dil_o, mem_o, g_gla, g_dil, g_mem, name):
    s = o_f.shape[0]
    tr = _tile(s, 256, 8)
    w4 = HEADS * HEAD_DIM
    wv = HEADS * GLA_DV

    def body(of_ref, ob_ref, r_ref, od_ref, mo_ref, gg_ref, gd_ref, gm_ref, mix_ref):
        o = of_ref[...] + ob_ref[...]
        normed, _ = _head_norm(o, gg_ref[...], GLA_DV)
        rv = r_ref[...]
        gate = rv * jax.nn.sigmoid(rv)
        for h in range(HEADS):
            mix_ref[:, h * GLA_DV:(h + 1) * GLA_DV] = (normed[h] * gate[:, h * GLA_DV:(h + 1) * GLA_DV]).astype(BF16)
        nd, _ = _head_norm(od_ref[...], gd_ref[...], HEAD_DIM)
        nm, _ = _head_norm(mo_ref[...], gm_ref[...], HEAD_DIM)
        for h in range(HEADS):
            mix_ref[:, wv + h * HEAD_DIM:wv + (h + 1) * HEAD_DIM] = nd[h].astype(BF16)
            mix_ref[:, wv + w4 + h * HEAD_DIM:wv + w4 + (h + 1) * HEAD_DIM] = nm[h].astype(BF16)

    rows = lambda w, c=0: pl.BlockSpec((tr, w), lambda i: (i, c))
    vec = lambda w: pl.BlockSpec((1, w), lambda i: (0, 0))
    return pl.pallas_call(
        body, name=name, grid=(s // tr,),
        in_specs=[rows(wv), rows(wv), rows(wv, OFF_GR // wv), rows(w4), rows(w4), vec(wv), vec(w4), vec(w4)],
        out_specs=rows(wv + 2 * w4),
        out_shape=jax.ShapeDtypeStruct((s, wv + 2 * w4), BF16),
        compiler_params=_params(("parallel",)),
    )(o_f, o_b, proj, dil_o, mem_o, g_gla, g_dil, g_mem)


def _mix_bwd(dmixed, o_f, o_b, proj, dil_o, mem_o, g_gla, g_dil, g_mem, name):
    s = o_f.shape[0]
    tr = _tile(s, 256, 8)
    w4 = HEADS * HEAD_DIM
    wv = HEADS * GLA_DV

    def body(dm_ref, of_ref, ob_ref, r_ref, od_ref, mo_ref, gg_ref, gd_ref, gm_ref,
             dog_ref, dr_ref, dod_ref, dld_ref, dom_ref, dlm_ref, dgg_ref, dgd_ref, dgm_ref):
        i = pl.program_id(0)

        @pl.when(i == 0)
        def _():
            dgg_ref[...] = jnp.zeros_like(dgg_ref)
            dgd_ref[...] = jnp.zeros_like(dgd_ref)
            dgm_ref[...] = jnp.zeros_like(dgm_ref)

        dm = dm_ref[...]
        o = of_ref[...] + ob_ref[...]
        normed, _ = _head_norm(o, gg_ref[...], GLA_DV)
        rv = r_ref[...]
        sg = jax.nn.sigmoid(rv)
        gate = rv * sg
        dgate = sg * (1.0 + rv * (1.0 - sg))
        d_gla = dm[:, :wv]
        for h in range(HEADS):
            sl = slice(h * GLA_DV, (h + 1) * GLA_DV)
            dr_ref[:, sl] = d_gla[:, sl] * normed[h] * dgate[:, sl]
        dos, dgs = _head_norm_bwd(o, gg_ref[...], d_gla * gate, GLA_DV)
        for h in range(HEADS):
            sl = slice(h * GLA_DV, (h + 1) * GLA_DV)
            dog_ref[:, sl] = dos[h]
            dgg_ref[:, sl] += dgs[h]
        for src_ref, g_ref, off, do_out, dl_out, dg_out in (
                (od_ref, gd_ref, wv, dod_ref, dld_ref, dgd_ref),
                (mo_ref, gm_ref, wv + w4, dom_ref, dlm_ref, dgm_ref)):
            src = src_ref[...]
            dos, dgs = _head_norm_bwd(src, g_ref[...], dm[:, off:off + w4], HEAD_DIM)
            for h in range(HEADS):
                sl = slice(h * HEAD_DIM, (h + 1) * HEAD_DIM)
                do_out[:, sl] = dos[h]
                dl_out[:, sl] = jnp.broadcast_to(
                    jnp.sum(dos[h] * src[:, sl], axis=-1, keepdims=True), (tr, HEAD_DIM))
                dg_out[:, sl] += dgs[h]

    rows = lambda w, c=0: pl.BlockSpec((tr, w), lambda i: (i, c))
    vec = lambda w: pl.BlockSpec((1, w), lambda i: (0, 0))
    sds = lambda w: jax.ShapeDtypeStruct((s, w), F32)
    vds = lambda w: jax.ShapeDtypeStruct((1, w), F32)
    return pl.pallas_call(
        body, name=name, grid=(s // tr,),
        in_specs=[rows(wv + 2 * w4), rows(wv), rows(wv), rows(wv, OFF_GR // wv), rows(w4), rows(w4),
                  vec(wv), vec(w4), vec(w4)],
        out_specs=[rows(wv), rows(wv), rows(w4), rows(w4), rows(w4), rows(w4), vec(wv), vec(w4), vec(w4)],
        out_shape=[sds(wv), sds(wv), sds(w4), sds(w4), sds(w4), sds(w4), vds(wv), vds(w4), vds(w4)],
        compiler_params=_params(("arbitrary",)),
    )(dmixed, o_f, o_b, proj, dil_o, mem_o, g_gla, g_dil, g_mem)


def _colsum(x, name, rows=512):
    s, w = x.shape
    tr = _tile(s, rows, 8)

    def body(x_ref, o_ref):
        @pl.when(pl.program_id(0) == 0)
        def _():
            o_ref[...] = jnp.zeros_like(o_ref)

        o_ref[...] += jnp.sum(x_ref[...], axis=0, keepdims=True)

    return pl.pallas_call(
        body, name=name, grid=(s // tr,),
        in_specs=[pl.BlockSpec((tr, w), lambda i: (i, 0))],
        out_specs=pl.BlockSpec((1, w), lambda i: (0, 0)),
        out_shape=jax.ShapeDtypeStruct((1, w), F32),
        compiler_params=_params(("arbitrary",)),
    )(x)


def _peer(k):
    x, y, c = lax.axis_index("x"), lax.axis_index("y"), lax.axis_index("c")
    kx, ky, kc = (k >> 2) & 1, (k >> 1) & 1, k & 1
    return (x ^ kx if kx else x, y ^ ky if ky else y, c ^ kc if kc else c)


def _my_index():
    return 4 * lax.axis_index("x") + 2 * lax.axis_index("y") + lax.axis_index("c")


def _cargo_shapes(cargo):
    return [jax.ShapeDtypeStruct(x.shape if sc else (N_DEV,) + x.shape, x.dtype) for x, sc in cargo]


def _cargo_sems(n):
    return [pltpu.SemaphoreType.DMA((n * (N_DEV - 1),)), pltpu.SemaphoreType.DMA((n * (N_DEV - 1),)),
            pltpu.SemaphoreType.DMA((n,))]


def _cargo_copies(in_refs, out_refs, sems, scatter, with_arrivals=True):
    send_sems, recv_sems, local_sems = sems
    me = _my_index()
    own, sends, arrivals = [], [], []
    for i, (src_ref, dst_ref) in enumerate(zip(in_refs, out_refs)):
        own.append(pltpu.make_async_copy(src_ref.at[me] if scatter[i] else src_ref, dst_ref.at[me], local_sems.at[i]))
        for k in range(1, N_DEV):
            peer = _peer(k)
            peer_idx = 4 * peer[0] + 2 * peer[1] + peer[2]
            src = src_ref.at[peer_idx] if scatter[i] else src_ref
            sem = i * (N_DEV - 1) + k - 1
            sends.append(pltpu.make_async_remote_copy(
                src_ref=src, dst_ref=dst_ref.at[me], send_sem=send_sems.at[sem], recv_sem=recv_sems.at[sem],
                device_id=peer, device_id_type=MESH))
            if with_arrivals:
                arrivals.append(pltpu.make_async_remote_copy(
                    src_ref=src, dst_ref=dst_ref.at[peer_idx], send_sem=send_sems.at[sem], recv_sem=recv_sems.at[sem],
                    device_id=peer, device_id_type=MESH))
    return own, sends, arrivals


def _cargo_start(in_refs, out_refs, sems, scatter):
    own, sends, _ = _cargo_copies(in_refs, out_refs, sems, scatter, with_arrivals=False)
    for cp in own + sends:
        cp.start()


def _cargo_wait(in_refs, out_refs, sems, scatter):
    own, sends, arrivals = _cargo_copies(in_refs, out_refs, sems, scatter)
    for cp in arrivals:
        cp.wait_recv()
    for cp in sends:
        cp.wait_send()
    for cp in own:
        cp.wait()


def _exchange(cargo, name):
    n = len(cargo)
    scatter = [sc for _, sc in cargo]

    def body(*refs):
        in_refs, out_refs, sems = refs[:n], refs[n:2 * n], refs[2 * n:]
        _cargo_start(in_refs, out_refs, sems, scatter)
        _cargo_wait(in_refs, out_refs, sems, scatter)

    any_spec = pl.BlockSpec(memory_space=pl.ANY)
    return pl.pallas_call(
        body, name=name,
        in_specs=[any_spec] * n, out_specs=[any_spec] * n, out_shape=_cargo_shapes(cargo),
        scratch_shapes=_cargo_sems(n),
        compiler_params=pltpu.CompilerParams(has_side_effects=True),
    )(*[x for x, _ in cargo])


def _adamw(parts, w, m, v, name, rows=256):
    r, c = w.shape
    tr = _tile(r, max(8, min(rows, ADAMW_TILE_ELEMS // c)), 8)
    c1 = 1.0 - ADAM_B1 ** ADAM_STEP
    c2 = 1.0 - ADAM_B2 ** ADAM_STEP

    def body(p_ref, w_ref, m_ref, v_ref, g_ref, d_ref, nm_ref, nv_ref):
        g = p_ref[0].astype(F32)
        for d in range(1, N_DEV):
            g = g + p_ref[d].astype(F32)
        nm = ADAM_B1 * m_ref[...] + (1.0 - ADAM_B1) * g
        nv = ADAM_B2 * v_ref[...] + (1.0 - ADAM_B2) * (g * g)
        m_hat = nm / c1
        v_hat = nv / c2
        g_ref[...] = g
        d_ref[...] = -ADAM_LR * (m_hat / (jnp.sqrt(v_hat) + ADAM_EPS) + ADAM_WD * w_ref[...])
        nm_ref[...] = nm
        nv_ref[...] = nv

    spec = pl.BlockSpec((tr, c), lambda i: (i, 0))
    return pl.pallas_call(
        body, name=name, grid=(r // tr,),
        in_specs=[pl.BlockSpec((N_DEV, tr, c), lambda i: (0, i, 0)), spec, spec, spec],
        out_specs=[spec] * 4,
        out_shape=[jax.ShapeDtypeStruct((r, c), F32)] * 4,
        compiler_params=_params(("parallel",)),
    )(parts, w, m, v)


SMALL = ("norm_mix", "gla_gate_bias_fwd", "gla_gate_bias_bwd", "gla_norm", "rel_bias", "dil_norm", "mem_norm",
         "mem_out_norm", "norm_mlp", "norm_final")


def _pack(arrs, rows):
    flat = jnp.concatenate([a.reshape(-1) for a in arrs])
    return jnp.pad(flat, (0, rows * 128 - flat.shape[0])).reshape(rows, 128)


def _unpack(buf, shapes):
    flat = buf.reshape(-1)
    out, off = [], 0
    for shp in shapes:
        n = int(np.prod(shp))
        out.append(flat[off:off + n].reshape(shp))
        off += n
    return out


def _split_in(w):
    main = jnp.concatenate([w[..., :3072], w[..., 3104:]], axis=-1)
    lr = w[..., 3072:3104]
    pad = [(0, 0)] * (w.ndim - 1) + [(0, LR_PAD - 2 * GLA_RANK)]
    return main, jnp.pad(lr, pad)


def _join_in(main, lr):
    return jnp.concatenate([main[..., :3072], lr[..., :2 * GLA_RANK], main[..., 3072:]], axis=-1)


def kernel(x, mem, norm_mix, w_in, gla_gate_up_fwd, gla_gate_bias_fwd, gla_gate_up_bwd, gla_gate_bias_bwd, gla_norm, rel_bias, dil_norm, mem_norm, w_mem_kv, mem_out_norm, w_out, norm_mlp, w_up, w_down, norm_final, loss_target, m_norm_mix, m_w_in, m_gla_gate_up_fwd, m_gla_gate_bias_fwd, m_gla_gate_up_bwd, m_gla_gate_bias_bwd, m_gla_norm, m_rel_bias, m_dil_norm, m_mem_norm, m_w_mem_kv, m_mem_out_norm, m_w_out, m_norm_mlp, m_w_up, m_w_down, m_norm_final, v_norm_mix, v_w_in, v_gla_gate_up_fwd, v_gla_gate_bias_fwd, v_gla_gate_up_bwd, v_gla_gate_bias_bwd, v_gla_norm, v_rel_bias, v_dil_norm, v_mem_norm, v_w_mem_kv, v_mem_out_norm, v_w_out, v_norm_mlp, v_w_up, v_w_down, v_norm_final):
    weights = dict(norm_mix=norm_mix, w_in=w_in, gla_gate_up_fwd=gla_gate_up_fwd, gla_gate_bias_fwd=gla_gate_bias_fwd,
                   gla_gate_up_bwd=gla_gate_up_bwd, gla_gate_bias_bwd=gla_gate_bias_bwd, gla_norm=gla_norm,
                   rel_bias=rel_bias, dil_norm=dil_norm, mem_norm=mem_norm, w_mem_kv=w_mem_kv,
                   mem_out_norm=mem_out_norm, w_out=w_out, norm_mlp=norm_mlp, w_up=w_up, w_down=w_down,
                   norm_final=norm_final)
    mom1 = dict(norm_mix=m_norm_mix, w_in=m_w_in, gla_gate_up_fwd=m_gla_gate_up_fwd,
                gla_gate_bias_fwd=m_gla_gate_bias_fwd, gla_gate_up_bwd=m_gla_gate_up_bwd,
                gla_gate_bias_bwd=m_gla_gate_bias_bwd, gla_norm=m_gla_norm, rel_bias=m_rel_bias, dil_norm=m_dil_norm,
                mem_norm=m_mem_norm, w_mem_kv=m_w_mem_kv, mem_out_norm=m_mem_out_norm, w_out=m_w_out,
                norm_mlp=m_norm_mlp, w_up=m_w_up, w_down=m_w_down, norm_final=m_norm_final)
    mom2 = dict(norm_mix=v_norm_mix, w_in=v_w_in, gla_gate_up_fwd=v_gla_gate_up_fwd,
                gla_gate_bias_fwd=v_gla_gate_bias_fwd, gla_gate_up_bwd=v_gla_gate_up_bwd,
                gla_gate_bias_bwd=v_gla_gate_bias_bwd, gla_norm=v_gla_norm, rel_bias=v_rel_bias, dil_norm=v_dil_norm,
                mem_norm=v_mem_norm, w_mem_kv=v_w_mem_kv, mem_out_norm=v_mem_out_norm, w_out=v_w_out,
                norm_mlp=v_norm_mlp, w_up=v_w_up, w_down=v_w_down, norm_final=v_norm_final)

    s, d = x.shape[1], x.shape[2]
    xs = x.reshape(s, d)
    mems = mem.reshape(mem.shape[1], d)
    target = loss_target.reshape(s, d)
    me = _my_index()
    n_layers = w_in.shape[0]
    gate_w = gla_gate_up_fwd.shape[2]

    shard = lambda name, l: (weights[name][l].astype(BF16), False)
    cols = lambda t: jnp.moveaxis(t, 0, 1).reshape(t.shape[1], N_DEV * t.shape[2])
    rows = lambda t: t.reshape(N_DEV * t.shape[1], t.shape[2])
    in_names = ("w_in", "gla_gate_up_fwd", "gla_gate_up_bwd")

    def in_mats(g_in, g_upf, g_upb):
        w_main, w_lr = _split_in(cols(g_in))
        up_f, up_b = cols(g_upf), cols(g_upb)
        zeros_up = jnp.zeros((GLA_RANK, HEADS * GLA_DK), BF16)
        pad_rows = jnp.zeros((LR_PAD - 2 * GLA_RANK, HEADS * GLA_DK), BF16)
        up_pad_f = jnp.concatenate([up_f, zeros_up, pad_rows], axis=0)
        up_pad_b = jnp.concatenate([zeros_up, up_b, pad_rows], axis=0)
        return dict(w_main=w_main, w_lr=w_lr, up_pad_f=up_pad_f, up_pad_b=up_pad_b,
                    up_cat=jnp.concatenate([up_pad_f, up_pad_b], axis=1))

    def halves(name, l):
        w = weights[name][l].astype(BF16)
        return (w[:w.shape[0] // 2], False), (w[w.shape[0] // 2:], False)

    join_cols = lambda ga, gb: jnp.concatenate([cols(ga), cols(gb)], axis=0)

    in_a0, in_b0 = halves("w_in", 0)
    up_a0, up_b0 = halves("w_up", 0)
    g0 = _exchange([in_a0, in_b0, shard("gla_gate_up_fwd", 0), shard("gla_gate_up_bwd", 0), shard("w_mem_kv", 0),
                    shard("w_out", 0), up_a0], "ag_weights_0")
    wts = [dict() for _ in range(n_layers)]
    wts[0].update(in_mats(jnp.concatenate([g0[0], g0[1]], axis=1), g0[2], g0[3]), wkv=rows(g0[4]), wout=rows(g0[5]))

    row2 = lambda t: t.reshape(1, -1)

    bands, bias_a, bias_b = [], [], []
    for bi, dilation in enumerate(DILATIONS):
        band = _band_buckets(dilation)
        ba, bb = _bias_tiles(band, rel_bias, _dilf_tq(_dilf_block(s), dilation), f"bias_tiles_{bi}")
        bands.append(band), bias_a.append(ba), bias_b.append(bb)

    saved = []
    xl = xs
    up_a = g0[6]
    for l in range(n_layers):
        wl = wts[l]
        nxt = l + 1 < n_layers
        h = _rmsnorm_fwd(xl, row2(norm_mix[l]), f"norm_mix_{l}")
        cargo = ([up_b0] if l == 0 else []) + ([shard("w_mem_kv", l + 1), shard("w_out", l + 1)] if nxt else [])
        proj, *got = _mm(h, wl["w_main"], "nn", [F32], f"proj_{l}", cargo=cargo)
        if l == 0:
            wl["wup"] = join_cols(up_a, got[0])
            got = got[1:]
        if nxt:
            wts[l + 1].update(wkv=rows(got[0]), wout=rows(got[1]))
            in_a, in_b = halves("w_in", l + 1)
            up_a_n, up_b_n = halves("w_up", l + 1)
        (lr,) = _mm(h, wl["w_lr"], "nn", [F32], f"proj_lr_{l}")
        bias_f, bias_b_ = row2(gla_gate_bias_fwd[l]), row2(gla_gate_bias_bwd[l])
        o_f, st_f, *got_f = _gla4_fwd(proj, lr, wl["up_pad_f"], bias_f, False, f"gla_fwd_f_{l}", cargo=(
            [in_a, shard("gla_gate_up_fwd", l + 1), shard("gla_gate_up_bwd", l + 1)] if nxt else []))
        o_b, st_b, *got_b = _gla4_fwd(proj, lr, wl["up_pad_b"], bias_b_, True, f"gla_fwd_b_{l}",
                                      cargo=[in_b] if nxt else [])
        if nxt:
            wts[l + 1].update(in_mats(jnp.concatenate([got_f[0], got_b[0]], axis=1), got_f[1], got_f[2]))
        dil_o, dil_lse, *got = _dilf_fwd(proj, bias_a, f"dil_fwd_{l}", cargo=[up_a_n] if nxt else [])
        if nxt:
            up_a = got[0]
        hm = _rmsnorm_fwd(mems, row2(mem_norm[l]), f"norm_mem_{l}")
        (kv,) = _mm(hm, wl["wkv"], "nn", [F32], f"mem_kv_{l}")
        mem_o, mem_lse = _mem_fwd(proj, kv, f"mem_fwd_{l}")
        mixed = _mix_fwd(o_f, o_b, proj, dil_o, mem_o, row2(gla_norm[l]), row2(dil_norm[l]), row2(mem_out_norm[l]),
                         f"mix_fwd_{l}")
        (x1,) = _mm(mixed, wl["wout"], "nn", [F32], f"out_proj_{l}",
                    epilogue=lambda acc, res: (acc + res,), extras=(xl,))
        h2 = _rmsnorm_fwd(x1, row2(norm_mlp[l]), f"norm_mlp_{l}")
        cargo = ([shard("w_down", 0)] if l == 0 else []) + ([up_b_n] if nxt else [])
        a, u, *got = _mm(h2, wl["wup"], "nn", [F32, BF16], f"mlp_up_{l}",
                         epilogue=lambda acc: (acc, jnp.square(jnp.maximum(acc, 0.0))), cargo=cargo)
        if l == 0:
            wl["wdown"] = rows(got[0])
            got = got[1:]
        if nxt:
            wts[l + 1]["wup"] = join_cols(up_a, got[0])
        x2, *got = _mm(u, wl["wdown"], "nn", [F32], f"mlp_down_{l}",
                       epilogue=lambda acc, res: (acc + res,), extras=(x1,),
                       cargo=[shard("w_down", l + 1)] if nxt else [])
        if nxt:
            wts[l + 1]["wdown"] = rows(got[0])
        saved.append(dict(x0=xl, h=h, proj=proj, lr=lr, o_f=o_f, o_b=o_b, st_f=st_f, st_b=st_b, hm=hm, kv=kv,
                          mem_o=mem_o, mem_lse=mem_lse, mixed=mixed, dil_o=dil_o, dil_lse=dil_lse, x1=x1, h2=h2,
                          a=a, u=u))
        xl = x2

    dx, dx_bf, dg_final, loss_part = _loss_head(xl, row2(norm_final), target, "loss_head")

    to_cols = lambda t: jnp.moveaxis(t.reshape(t.shape[0], N_DEV, -1), 1, 0)
    to_rows = lambda t: t.reshape(N_DEV, -1, t.shape[1])
    tail_names = ("w_mem_kv", "w_in", "gla_gate_up_fwd", "gla_gate_up_bwd")
    recv = {nm: [None] * n_layers for nm in tail_names + ("w_out", "w_up", "w_down")}
    tail = None
    grads_small = {k: [None] * n_layers for k in SMALL}
    dbias_sum = [None] * len(DIL_CONFIGS)
    for l in range(n_layers - 1, -1, -1):
        sv = saved[l]
        wl = wts[l]
        dw_down, *got = _mm(sv["u"], dx_bf, "tn", [BF16], f"dw_down_{l}", cargo=tail or [])
        if tail:
            for nm, part in zip(tail_names, got):
                recv[nm][l + 1] = part
        (da,) = _mm(dx_bf, wl["wdown"], "nt", [BF16], f"d_mlp_act_{l}",
                    epilogue=lambda acc, a_: (acc * (2.0 * jnp.maximum(a_, 0.0)),), extras=(sv["a"],))
        dw_up, recv["w_down"][l] = _mm(sv["h2"], da, "tn", [BF16], f"dw_up_{l}", cargo=[(to_rows(dw_down), True)])
        dh2, recv["w_up"][l] = _mm(da, wl["wup"], "nt", [F32], f"d_norm_mlp_in_{l}", cargo=[(to_cols(dw_up), True)])
        dx1, dx1_bf, dg_mlp = _rmsnorm_bwd(sv["x1"], row2(norm_mlp[l]), dh2, dx, f"norm_mlp_bwd_{l}")
        (dw_out,) = _mm(sv["mixed"], dx1_bf, "tn", [BF16], f"dw_out_{l}")
        dmixed, recv["w_out"][l] = _mm(dx1_bf, wl["wout"], "nt", [F32], f"d_mixed_{l}",
                                       cargo=[(to_rows(dw_out), True)])
        (d_og, d_r, d_od, dl_d, d_om, dl_m, dg_gla, dg_dil, dg_memo) = _mix_bwd(
            dmixed, sv["o_f"], sv["o_b"], sv["proj"], sv["dil_o"], sv["mem_o"],
            row2(gla_norm[l]), row2(dil_norm[l]), row2(mem_out_norm[l]), f"mix_bwd_{l}")
        bias_f, bias_b_ = row2(gla_gate_bias_fwd[l]), row2(gla_gate_bias_bwd[l])
        dq1, dk1, dv1, dz_f = _gla4_bwd(sv["proj"], sv["lr"], wl["up_pad_f"], bias_f, sv["st_f"], d_og, None, False,
                                        f"gla_bwd_f_{l}")
        dq_g, dk_g, dv_g, dz_b = _gla4_bwd(sv["proj"], sv["lr"], wl["up_pad_b"], bias_b_, sv["st_b"], d_og,
                                           (dq1, dk1, dv1), True, f"gla_bwd_b_{l}")
        dz = jnp.concatenate([dz_f, dz_b], axis=1)
        (d_lr,) = _mm(dz, wl["up_cat"], "nt", [BF16], f"d_lowrank_{l}")
        (d_upcat,) = _mm(sv["lr"], dz, "tn", [BF16], f"dw_gate_up_{l}")
        dzsum = _colsum(dz, f"d_gate_bias_{l}")
        dq_d, *dbias = _dilf_bwd_q(sv["proj"], d_od, sv["dil_lse"], dl_d, bias_a, f"dil_bwd_q_{l}")
        dk_d, dv_d = _dilf_bwd_kv(sv["proj"], d_od, sv["dil_lse"], dl_d, bias_b, f"dil_bwd_kv_{l}")
        for bi in range(len(DILATIONS)):
            dbias_sum[bi] = dbias[bi] if dbias_sum[bi] is None else dbias_sum[bi] + dbias[bi]
        dq_m, dkm, dvm = _mem_bwd(sv["proj"], sv["kv"], d_om, sv["mem_lse"], dl_m, f"mem_bwd_{l}")
        dkv = jnp.concatenate([dkm, dvm], axis=1).astype(BF16)
        (dw_kv,) = _mm(sv["hm"], dkv, "tn", [BF16], f"dw_mem_kv_{l}")
        (dhm,) = _mm(dkv, wl["wkv"], "nt", [F32], f"d_mem_norm_in_{l}")
        _, _, dg_mem = _rmsnorm_bwd(mems, row2(mem_norm[l]), dhm, None, f"norm_mem_bwd_{l}")
        dproj = jnp.concatenate([t.astype(BF16) for t in (dq_g, dk_g, dv_g, d_r, dq_d, dk_d, dv_d, dq_m)], axis=1)
        (dw_main,) = _mm(sv["h"], dproj, "tn", [BF16], f"dw_in_{l}")
        (dw_lr,) = _mm(sv["h"], d_lr, "tn", [BF16], f"dw_in_lr_{l}")
        (dh_lr,) = _mm(d_lr, wl["w_lr"], "nt", [F32], f"d_norm_mix_in_lr_{l}")
        (dh,) = _mm(dproj, wl["w_main"], "nt", [F32], f"d_norm_mix_in_{l}",
                    epilogue=lambda acc, other: (acc + other,), extras=(dh_lr,))
        dx, dx_bf, dg_mix = _rmsnorm_bwd(sv["x0"], row2(norm_mix[l]), dh, dx1, f"norm_mix_bwd_{l}")
        tail = [(to_rows(dw_kv), True), (to_cols(_join_in(dw_main, dw_lr)), True),
                (to_cols(d_upcat[:GLA_RANK, :HEADS * GLA_DK]), True),
                (to_cols(d_upcat[GLA_RANK:2 * GLA_RANK, HEADS * GLA_DK:]), True)]

        grads_small["norm_mix"][l] = dg_mix
        grads_small["gla_gate_bias_fwd"][l] = dzsum[:, :HEADS * GLA_DK]
        grads_small["gla_gate_bias_bwd"][l] = dzsum[:, HEADS * GLA_DK:]
        grads_small["gla_norm"][l] = dg_gla
        grads_small["dil_norm"][l] = dg_dil
        grads_small["mem_norm"][l] = dg_mem
        grads_small["mem_out_norm"][l] = dg_memo
        grads_small["norm_mlp"][l] = dg_mlp

    for nm, part in zip(tail_names, _exchange(tail, "rs_tail")):
        recv[nm][0] = part
    d_table = _bias_grad(bands, dbias_sum, "bias_grad")[:, :HEADS]
    layers = lambda nm: jnp.stack(recv[nm], axis=1)
    r_in, r_upf, r_upb, r_kv = layers("w_in"), layers("gla_gate_up_fwd"), layers("gla_gate_up_bwd"), layers("w_mem_kv")
    r_out, r_up, r_down = layers("w_out"), layers("w_up"), layers("w_down")

    small_shapes = [weights[k].shape for k in SMALL]
    small_grads = []
    for k in SMALL:
        if k == "rel_bias":
            small_grads.append(d_table)
        elif k == "norm_final":
            small_grads.append(dg_final)
        else:
            small_grads.append(jnp.concatenate(grads_small[k], axis=0))
    n_small = sum(int(np.prod(shp)) for shp in small_shapes)
    small_rows = -(-(n_small + 128) // (8 * 128)) * 8
    pack = lambda arrs, extra: _pack(list(arrs) + [extra], small_rows)
    zeros_tail = jnp.zeros((128,), F32)
    (small_parts,) = _exchange([(pack(small_grads, loss_part.reshape(-1)), False)], "ag_small")
    sg, sd, sm, sv_ = _adamw(small_parts, pack([weights[k] for k in SMALL], zeros_tail),
                             pack([mom1[k] for k in SMALL], zeros_tail),
                             pack([mom2[k] for k in SMALL], zeros_tail), "adamw_small")
    loss = sg.reshape(-1)[n_small]
    small_out = [dict(zip(SMALL, _unpack(buf, small_shapes))) for buf in (sg, sd, sm, sv_)]

    def shard_update(parts, name):
        w = weights[name]
        shp = w.shape
        flat = lambda t: t.reshape(-1, shp[-1])
        res = _adamw(parts.reshape(N_DEV, -1, shp[-1]), flat(w), flat(mom1[name]), flat(mom2[name]), f"adamw_{name}")
        return [t.reshape(shp) for t in res]

    big = dict(w_in=shard_update(r_in, "w_in"), gla_gate_up_fwd=shard_update(r_upf, "gla_gate_up_fwd"),
               gla_gate_up_bwd=shard_update(r_upb, "gla_gate_up_bwd"), w_mem_kv=shard_update(r_kv, "w_mem_kv"),
               w_out=shard_update(r_out, "w_out"), w_up=shard_update(r_up, "w_up"),
               w_down=shard_update(r_down, "w_down"))

    order = ("norm_mix", "w_in", "gla_gate_up_fwd", "gla_gate_bias_fwd", "gla_gate_up_bwd", "gla_gate_bias_bwd",
             "gla_norm", "rel_bias", "dil_norm", "mem_norm", "w_mem_kv", "mem_out_norm", "w_out", "norm_mlp", "w_up",
             "w_down", "norm_final")
    outs = [loss, dx.reshape(x.shape)]
    for which in range(4):
        for name in order:
            outs.append(big[name][which] if name in big else small_out[which][name])
    return tuple(outs)
```

```python
import functools
import math

import numpy as np
import jax
import jax.numpy as jnp
from jax import lax
from jax.experimental import pallas as pl
from jax.experimental.pallas import tpu as pltpu

F32 = jnp.float32
BF16 = jnp.bfloat16

N_DEV = 8
DEPTH = 4
HEADS = 4
GLA_DK = 128
GLA_DV = 256
GLA_RANK = 16
GLA_GATE_NORMALIZER = 16.0
GLA_CHUNK = 64
HEAD_DIM = 128
DIL_CONFIGS = ((128, 1), (512, 4), (2048, 16))
DIL_HALF = 64
REL_BUCKETS = 32
REL_MAX_DISTANCE = 1024
EPS = 1e-6
NEG_INF = -1e30
IN_SPLITS = (512, 512, 1024, 1024, 16, 16, 512, 512, 512, 512)
IN_WIDTH = sum(IN_SPLITS)
MAIN_WIDTH = IN_WIDTH - 2 * GLA_RANK
LR_PAD = 128
OFF_GQ, OFF_GK, OFF_GV, OFF_GR, OFF_DQ, OFF_DK, OFF_DV, OFF_MQ = 0, 512, 1024, 2048, 3072, 3584, 4096, 4608

ADAM_LR = 0.001
ADAM_B1 = 0.9
ADAM_B2 = 0.999
ADAM_EPS = 1e-08
ADAM_WD = 0.01
ADAM_STEP = 10

VMEM_LIMIT = 56 * 1024 * 1024
ADAMW_TILE_ELEMS = 128 * 1024
MESH = pl.DeviceIdType.MESH

NN = ((1,), (0,))
NT = ((1,), (1,))
TN = ((0,), (0,))


def _dot(a, b, dims, precision=None):
    return lax.dot_general(a, b, (dims, ((), ())), preferred_element_type=F32, precision=precision)


def _tile(n, pref, mult=128):
    if n <= pref:
        return n
    t = (pref // mult) * mult
    while t >= mult:
        if n % t == 0:
            return t
        t -= mult
    return n


def _params(sem, **kw):
    return pltpu.CompilerParams(dimension_semantics=sem, vmem_limit_bytes=VMEM_LIMIT, **kw)


def _mm(a, b, mode, outs, name, epilogue=None, extras=(), tm=1024, tn=1024, tk=2048, cargo=()):
    if mode == "nn":
        (m, k), (k2, n) = a.shape, b.shape
    elif mode == "nt":
        (m, k), (n, k2) = a.shape, b.shape
    else:
        (k, m), (k2, n) = a.shape, b.shape
    assert k == k2, (a.shape, b.shape, mode)
    tm, tn, tk = _tile(m, tm), _tile(n, tn), _tile(k, tk)
    gi, gj, nk = m // tm, n // tn, k // tk
    if mode == "nn":
        a_spec = pl.BlockSpec((tm, tk), lambda i, j, kk: (i, kk))
        b_spec = pl.BlockSpec((tk, tn), lambda i, j, kk: (kk, j))
        dims = NN
    elif mode == "nt":
        a_spec = pl.BlockSpec((tm, tk), lambda i, j, kk: (i, kk))
        b_spec = pl.BlockSpec((tn, tk), lambda i, j, kk: (j, kk))
        dims = NT
    else:
        a_spec = pl.BlockSpec((tk, tm), lambda i, j, kk: (kk, i))
        b_spec = pl.BlockSpec((tk, tn), lambda i, j, kk: (kk, j))
        dims = TN
    tile_spec = pl.BlockSpec((tm, tn), lambda i, j, kk: (i, j))
    any_spec = pl.BlockSpec(memory_space=pl.ANY)
    n_extra, n_out, n_cargo = len(extras), len(outs), len(cargo)
    scatter = [sc for _, sc in cargo]
    if epilogue is None:
        epilogue = lambda acc: (acc,)

    def body(a_ref, b_ref, *rest):
        extra_refs, rest = rest[:n_extra], rest[n_extra:]
        cargo_in, rest = rest[:n_cargo], rest[n_cargo:]
        out_refs, rest = rest[:n_out], rest[n_out:]
        cargo_out, rest = rest[:n_cargo], rest[n_cargo:]
        i, j, kk = pl.program_id(0), pl.program_id(1), pl.program_id(2)
        if n_cargo:
            sems = rest[-3:]

            @pl.when((i == 0) & (j == 0) & (kk == 0))
            def _():
                _cargo_start(cargo_in, cargo_out, sems, scatter)

        def finish(total):
            res = epilogue(total, *[e[...] for e in extra_refs])
            for o_ref, r in zip(out_refs, res):
                o_ref[...] = r.astype(o_ref.dtype)

        part = _dot(a_ref[...].astype(BF16), b_ref[...].astype(BF16), dims)
        if nk == 1:
            finish(part)
        else:
            acc = rest[0]

            @pl.when(kk == 0)
            def _():
                acc[...] = part

            @pl.when((kk > 0) & (kk < nk - 1))
            def _():
                acc[...] += part

            @pl.when(kk == nk - 1)
            def _():
                finish(acc[...] + part)

        if n_cargo:
            @pl.when((i == gi - 1) & (j == gj - 1) & (kk == nk - 1))
            def _():
                _cargo_wait(cargo_in, cargo_out, sems, scatter)

    scratch = [pltpu.VMEM((tm, tn), F32)] if nk > 1 else []
    if n_cargo:
        scratch += _cargo_sems(n_cargo)
    sem = ("arbitrary",) * 3 if n_cargo else ("parallel", "parallel", "arbitrary")
    return pl.pallas_call(
        body,
        name=name,
        grid=(gi, gj, nk),
        in_specs=[a_spec, b_spec] + [tile_spec] * n_extra + [any_spec] * n_cargo,
        out_specs=[tile_spec] * n_out + [any_spec] * n_cargo,
        out_shape=[jax.ShapeDtypeStruct((m, n), d) for d in outs] + _cargo_shapes(cargo),
        scratch_shapes=scratch,
        compiler_params=_params(sem),
    )(a, b, *extras, *[x for x, _ in cargo])


def _rmsnorm_fwd(x, gain, name, rows=256):
    s, d = x.shape
    tr = _tile(s, rows, 8)

    def body(x_ref, g_ref, h_ref):
        xv = x_ref[...]
        r = lax.rsqrt(jnp.mean(xv * xv, axis=-1, keepdims=True) + EPS)
        h_ref[...] = (xv * r * g_ref[...]).astype(h_ref.dtype)

    return pl.pallas_call(
        body, name=name, grid=(s // tr,),
        in_specs=[pl.BlockSpec((tr, d), lambda i: (i, 0)), pl.BlockSpec((1, d), lambda i: (0, 0))],
        out_specs=pl.BlockSpec((tr, d), lambda i: (i, 0)),
        out_shape=jax.ShapeDtypeStruct((s, d), BF16),
        compiler_params=_params(("parallel",)),
    )(x, gain)


def _rmsnorm_bwd(x, gain, dh, dres, name, rows=256):
    s, d = x.shape
    tr = _tile(s, rows, 8)
    has_res = dres is not None

    def body(*refs):
        if has_res:
            x_ref, g_ref, dh_ref, dres_ref, dx_ref, dxb_ref, dg_ref = refs
        else:
            x_ref, g_ref, dh_ref, dx_ref, dxb_ref, dg_ref = refs
        i = pl.program_id(0)
        xv = x_ref[...]
        dy = dh_ref[...].astype(F32)
        r = lax.rsqrt(jnp.mean(xv * xv, axis=-1, keepdims=True) + EPS)
        xn = xv * r
        t = dy * g_ref[...]
        dx = r * (t - xn * jnp.mean(t * xn, axis=-1, keepdims=True))
        if has_res:
            dx = dx + dres_ref[...]
        dx_ref[...] = dx
        dxb_ref[...] = dx.astype(BF16)

        @pl.when(i == 0)
        def _():
            dg_ref[...] = jnp.zeros_like(dg_ref)

        dg_ref[...] += jnp.sum(dy * xn, axis=0, keepdims=True)

    row_spec = pl.BlockSpec((tr, d), lambda i: (i, 0))
    vec_spec = pl.BlockSpec((1, d), lambda i: (0, 0))
    args = [x, gain, dh] + ([dres] if has_res else [])
    return pl.pallas_call(
        body, name=name, grid=(s // tr,),
        in_specs=[row_spec, vec_spec, row_spec] + ([row_spec] if has_res else []),
        out_specs=[row_spec, row_spec, vec_spec],
        out_shape=[jax.ShapeDtypeStruct((s, d), F32), jax.ShapeDtypeStruct((s, d), BF16),
                   jax.ShapeDtypeStruct((1, d), F32)],
        compiler_params=_params(("arbitrary",)),
    )(*args)


def _loss_head(x, gain, target, name, rows=256):
    s, d = x.shape
    tr = _tile(s, rows, 8)

    def body(x_ref, g_ref, t_ref, dx_ref, dxb_ref, dg_ref, loss_ref):
        i = pl.program_id(0)
        xv = x_ref[...]
        g = g_ref[...]
        r = lax.rsqrt(jnp.mean(xv * xv, axis=-1, keepdims=True) + EPS)
        xn = xv * r
        err = xn * g - t_ref[...]
        dy = err * (1.0 / d)
        t = dy * g
        dx = r * (t - xn * jnp.mean(t * xn, axis=-1, keepdims=True))
        dx_ref[...] = dx
        dxb_ref[...] = dx.astype(BF16)

        @pl.when(i == 0)
        def _():
            dg_ref[...] = jnp.zeros_like(dg_ref)
            loss_ref[...] = jnp.zeros_like(loss_ref)

        dg_ref[...] += jnp.sum(dy * xn, axis=0, keepdims=True)
        part = 0.5 * jnp.sum(jnp.mean(err * err, axis=-1, keepdims=True), axis=0, keepdims=True)
        loss_ref[...] += jnp.broadcast_to(part, loss_ref.shape)

    row_spec = pl.BlockSpec((tr, d), lambda i: (i, 0))
    vec_spec = pl.BlockSpec((1, d), lambda i: (0, 0))
    return pl.pallas_call(
        body, name=name, grid=(s // tr,),
        in_specs=[row_spec, vec_spec, row_spec],
        out_specs=[row_spec, row_spec, vec_spec, pl.BlockSpec((1, 128), lambda i: (0, 0))],
        out_shape=[jax.ShapeDtypeStruct((s, d), F32), jax.ShapeDtypeStruct((s, d), BF16),
                   jax.ShapeDtypeStruct((1, d), F32), jax.ShapeDtypeStruct((1, 128), F32)],
        compiler_params=_params(("arbitrary",)),
    )(x, gain, target)


def _log_sigmoid(z):
    return jnp.minimum(z, 0.0) - jnp.log1p(jnp.exp(-jnp.abs(z)))


def _gla_tri(reverse):
    row = lax.broadcasted_iota(jnp.int32, (GLA_CHUNK, GLA_CHUNK), 0)
    col = lax.broadcasted_iota(jnp.int32, (GLA_CHUNK, GLA_CHUNK), 1)
    return (col >= row) if reverse else (col <= row)


def _gla_rows(s):
    return _tile(s, 256, GLA_CHUNK)


def _gla_chunk_terms(q, k, g, tri, reverse):
    b = _dot(tri.astype(F32), g, NN, precision=lax.Precision.HIGHEST)
    bl = b[0:1] if reverse else b[GLA_CHUNK - 1:GLA_CHUNK]
    qd = q * jnp.exp(b)
    ki = k * jnp.exp(-b)
    ke = k * jnp.exp(bl - b)
    return b, bl, qd, ki, ke


def _gla_fwd(proj, lr, up_pad, bias, reverse, name):
    s = proj.shape[0]
    ts = _gla_rows(s)
    nblk, cpb = s // ts, ts // GLA_CHUNK
    scale = GLA_DK ** -0.5

    def blk(i):
        return (nblk - 1 - i) if reverse else i

    def body(q_ref, k_ref, v_ref, lr_ref, up_ref, b_ref, o_ref, st_ref, state):
        @pl.when(pl.program_id(1) == 0)
        def _():
            state[...] = jnp.zeros_like(state)

        tri = _gla_tri(reverse)
        z = _dot(lr_ref[...].astype(BF16), up_ref[...], NN) + b_ref[...]
        g_all = _log_sigmoid(z) * (1.0 / GLA_GATE_NORMALIZER)
        order = range(cpb - 1, -1, -1) if reverse else range(cpb)
        for c in order:
            sl = slice(c * GLA_CHUNK, (c + 1) * GLA_CHUNK)
            _, bl, qd, ki, ke = _gla_chunk_terms(q_ref[sl, :] * scale, k_ref[sl, :], g_all[sl, :], tri, reverse)
            qdb = qd.astype(BF16)
            a = jnp.where(tri, _dot(qdb, ki.astype(BF16), NT), 0.0)
            vb = v_ref[sl, :].astype(BF16)
            st = state[...]
            o_ref[sl, :] = _dot(a.astype(BF16), vb, NN) + _dot(qdb, st.astype(BF16), NT)
            st_ref[0, c] = st
            state[...] = st * jnp.exp(bl) + _dot(vb, ke.astype(BF16), TN)

    qk = lambda off: pl.BlockSpec((ts, GLA_DK), lambda h, i: (blk(i), off // GLA_DK + h))
    return pl.pallas_call(
        body, name=name, grid=(HEADS, nblk),
        in_specs=[qk(OFF_GQ), qk(OFF_GK),
                  pl.BlockSpec((ts, GLA_DV), lambda h, i: (blk(i), OFF_GV // GLA_DV + h)),
                  pl.BlockSpec((ts, LR_PAD), lambda h, i: (blk(i), 0)),
                  pl.BlockSpec((LR_PAD, GLA_DK), lambda h, i: (0, h)),
                  pl.BlockSpec((1, GLA_DK), lambda h, i: (0, h))],
        out_specs=[pl.BlockSpec((ts, GLA_DV), lambda h, i: (blk(i), h)),
                   pl.BlockSpec((1, cpb, GLA_DV, GLA_DK), lambda h, i: (h, blk(i), 0, 0))],
        out_shape=[jax.ShapeDtypeStruct((s, HEADS * GLA_DV), F32),
                   jax.ShapeDtypeStruct((HEADS, s // GLA_CHUNK, GLA_DV, GLA_DK), F32)],
        scratch_shapes=[pltpu.VMEM((GLA_DV, GLA_DK), F32)],
        compiler_params=_params(("parallel", "arbitrary")),
    )(proj, proj, proj, lr, up_pad, bias)


def _gla_bwd(proj, lr, up_pad, bias, states, d_o, prev, reverse, name):
    s = proj.shape[0]
    ts = _gla_rows(s)
    nblk, cpb = s // ts, ts // GLA_CHUNK
    scale = GLA_DK ** -0.5
    has_prev = prev is not None

    def blk(i):
        return i if reverse else (nblk - 1 - i)

    def body(*refs):
        q_ref, k_ref, v_ref, lr_ref, up_ref, b_ref, st_ref, do_ref = refs[:8]
        refs = refs[8:]
        if has_prev:
            pq_ref, pk_ref, pv_ref = refs[:3]
            refs = refs[3:]
        dq_ref, dk_ref, dv_ref, dz_ref, dstate = refs

        @pl.when(pl.program_id(1) == 0)
        def _():
            dstate[...] = jnp.zeros_like(dstate)

        tri = _gla_tri(reverse)
        tri_t = _gla_tri(not reverse)
        row = lax.broadcasted_iota(jnp.int32, (GLA_CHUNK, GLA_DK), 0)
        last_row = (row == 0) if reverse else (row == GLA_CHUNK - 1)
        z = _dot(lr_ref[...].astype(BF16), up_ref[...], NN) + b_ref[...]
        g_all = _log_sigmoid(z) * (1.0 / GLA_GATE_NORMALIZER)
        dgate = (1.0 / GLA_GATE_NORMALIZER) * (1.0 - jax.nn.sigmoid(z))
        order = range(cpb) if reverse else range(cpb - 1, -1, -1)
        for c in order:
            sl = slice(c * GLA_CHUNK, (c + 1) * GLA_CHUNK)
            b, bl, qd, ki, ke = _gla_chunk_terms(q_ref[sl, :] * scale, k_ref[sl, :], g_all[sl, :], tri, reverse)
            qdb, kib, keb = qd.astype(BF16), ki.astype(BF16), ke.astype(BF16)
            a = jnp.where(tri, _dot(qdb, kib, NT), 0.0)
            vb = v_ref[sl, :].astype(BF16)
            dob = do_ref[sl, :].astype(BF16)
            st = st_ref[0, c]
            dst = dstate[...]
            dstb = dst.astype(BF16)
            da = jnp.where(tri, _dot(dob, vb, NT), 0.0).astype(BF16)
            dv = _dot(a.astype(BF16), dob, TN) + _dot(keb, dstb, NT)
            dqd = _dot(da, kib, NN) + _dot(dob, st.astype(BF16), NN)
            dki = _dot(da, qdb, TN)
            dke = _dot(vb, dstb, NN)
            decay = jnp.exp(bl)
            dbl = decay * jnp.sum(dst * st, axis=0, keepdims=True) + jnp.sum(dke * ke, axis=0, keepdims=True)
            dstate[...] = dst * decay + _dot(dob, qdb, TN)
            db = dqd * qd - dki * ki - dke * ke + jnp.where(last_row, dbl, 0.0)
            dg = _dot(tri_t.astype(F32), db, NN, precision=lax.Precision.HIGHEST)
            dq = dqd * jnp.exp(b) * scale
            dk = dki * jnp.exp(-b) + dke * jnp.exp(bl - b)
            if has_prev:
                dq = dq + pq_ref[sl, :]
                dk = dk + pk_ref[sl, :]
                dv = dv + pv_ref[sl, :]
            dq_ref[sl, :] = dq
            dk_ref[sl, :] = dk
            dv_ref[sl, :] = dv
            dz_ref[sl, :] = dg * dgate[sl, :]

    qk = lambda off: pl.BlockSpec((ts, GLA_DK), lambda h, i: (blk(i), off // GLA_DK + h))
    hk = pl.BlockSpec((ts, GLA_DK), lambda h, i: (blk(i), h))
    hv = pl.BlockSpec((ts, GLA_DV), lambda h, i: (blk(i), h))
    in_specs = [qk(OFF_GQ), qk(OFF_GK),
                pl.BlockSpec((ts, GLA_DV), lambda h, i: (blk(i), OFF_GV // GLA_DV + h)),
                pl.BlockSpec((ts, LR_PAD), lambda h, i: (blk(i), 0)),
                pl.BlockSpec((LR_PAD, GLA_DK), lambda h, i: (0, h)),
                pl.BlockSpec((1, GLA_DK), lambda h, i: (0, h)),
                pl.BlockSpec((1, cpb, GLA_DV, GLA_DK), lambda h, i: (h, blk(i), 0, 0)),
                hv]
    args = [proj, proj, proj, lr, up_pad, bias, states, d_o]
    if has_prev:
        in_specs += [hk, hk, hv]
        args += list(prev)
    return pl.pallas_call(
        body, name=name, grid=(HEADS, nblk),
        in_specs=in_specs,
        out_specs=[hk, hk, hv, hk],
        out_shape=[jax.ShapeDtypeStruct((s, HEADS * GLA_DK), F32), jax.ShapeDtypeStruct((s, HEADS * GLA_DK), F32),
                   jax.ShapeDtypeStruct((s, HEADS * GLA_DV), F32), jax.ShapeDtypeStruct((s, HEADS * GLA_DK), F32)],
        scratch_shapes=[pltpu.VMEM((GLA_DV, GLA_DK), F32)],
        compiler_params=_params(("parallel", "arbitrary")),
    )(*args)


def _dil_tile(l):
    return _tile(l, 256, DIL_HALF)


def t5_bucket(rel):
    half = REL_BUCKETS // 2
    max_exact = half // 2
    ret = jnp.where(rel > 0, half, 0)
    n = jnp.abs(rel)
    nf = jnp.maximum(n, 1).astype(jnp.float32)
    large = max_exact + (jnp.log(nf / max_exact) / math.log(REL_MAX_DISTANCE / max_exact)
                         * (half - max_exact)).astype(jnp.int32)
    large = jnp.minimum(large, half - 1)
    return ret + jnp.where(n < max_exact, n, large)


def _band_buckets(dilation):
    w = DIL_HALF
    rel_sub = jnp.arange(3 * w)[None, :] - w - jnp.arange(w)[:, None]
    return t5_bucket(rel_sub * dilation)[0, :2 * w + 1].astype(jnp.int32)


def _band_offsets(tq):
    tk = tq + 2 * DIL_HALF
    da = lax.broadcasted_iota(jnp.int32, (tq, tk), 1) - lax.broadcasted_iota(jnp.int32, (tq, tk), 0)
    db = (lax.broadcasted_iota(jnp.int32, (tk, tq), 1) - lax.broadcasted_iota(jnp.int32, (tk, tq), 0)
          + 2 * DIL_HALF)
    return jnp.clip(da, 0, 2 * DIL_HALF), jnp.clip(db, 0, 2 * DIL_HALF)


def _bias_tiles(band, table, tq, name):
    tk = tq + 2 * DIL_HALF

    def body(band_ref, t_ref, oa_ref, ob_ref):
        h = pl.program_id(0)
        off_a, off_b = _band_offsets(tq)

        def step(t, carry):
            bkt_a, bkt_b = carry
            bkt = band_ref[t]
            return jnp.where(off_a == t, bkt, bkt_a), jnp.where(off_b == t, bkt, bkt_b)

        bkt_a, bkt_b = lax.fori_loop(0, 2 * DIL_HALF + 1, step,
                                     (jnp.zeros((tq, tk), jnp.int32), jnp.zeros((tk, tq), jnp.int32)))
        acc_a, acc_b = jnp.zeros((tq, tk), F32), jnp.zeros((tk, tq), F32)
        for bkt in range(REL_BUCKETS):
            val = t_ref[bkt, h]
            acc_a = jnp.where(bkt_a == bkt, val, acc_a)
            acc_b = jnp.where(bkt_b == bkt, val, acc_b)
        oa_ref[0] = acc_a
        ob_ref[0] = acc_b

    smem = pl.BlockSpec(memory_space=pltpu.SMEM)
    return pl.pallas_call(
        body, name=name, grid=(HEADS,),
        in_specs=[smem, smem],
        out_specs=[pl.BlockSpec((1, tq, tk), lambda h: (h, 0, 0)), pl.BlockSpec((1, tk, tq), lambda h: (h, 0, 0))],
        out_shape=[jax.ShapeDtypeStruct((HEADS, tq, tk), F32), jax.ShapeDtypeStruct((HEADS, tk, tq), F32)],
        compiler_params=_params(("arbitrary",)),
    )(band, table)


def _bias_grad(bands, dbias_list, name):
    n = len(bands)

    def body(*refs):
        band_refs, db_refs, out_ref = refs[:n], refs[n:2 * n], refs[2 * n]
        row = lax.broadcasted_iota(jnp.int32, (REL_BUCKETS, 128), 0)
        lane = lax.broadcasted_iota(jnp.int32, (REL_BUCKETS, 128), 1)
        acc = jnp.zeros((REL_BUCKETS, 128), F32)
        for band_ref, d_ref in zip(band_refs, db_refs):
            off_a, _ = _band_offsets(d_ref.shape[1])
            for h in range(HEADS):
                def step(t, acc, band_ref=band_ref, d_ref=d_ref, h=h, off_a=off_a):
                    tot = jnp.sum(jnp.where(off_a == t, d_ref[h], 0.0))
                    return acc + jnp.where((row == band_ref[t]) & (lane == h), tot, 0.0)

                acc = lax.fori_loop(0, 2 * DIL_HALF + 1, step, acc)
        out_ref[...] = acc

    vm = pl.BlockSpec(memory_space=pltpu.VMEM)
    smem = pl.BlockSpec(memory_space=pltpu.SMEM)
    return pl.pallas_call(
        body, name=name,
        in_specs=[smem] * n + [vm] * n, out_specs=vm,
        out_shape=jax.ShapeDtypeStruct((REL_BUCKETS, 128), F32),
        compiler_params=pltpu.CompilerParams(vmem_limit_bytes=VMEM_LIMIT),
    )(*bands, *dbias_list)


def _dil_specs(l, tq, dilation, width, off, by_head_first):
    nb64 = l // DIL_HALF
    per = tq // DIL_HALF

    def col(h, r):
        return (r * width + off) // HEAD_DIM + h

    def wrap(f):
        if by_head_first:
            return lambda h, r, n: f(h, r, n)
        return lambda r, h, n: f(h, r, n)

    prev = pl.BlockSpec((DIL_HALF, HEAD_DIM), wrap(lambda h, r, n: (jnp.maximum(n * per - 1, 0), col(h, r))))
    main = pl.BlockSpec((tq, HEAD_DIM), wrap(lambda h, r, n: (n, col(h, r))))
    nxt = pl.BlockSpec((DIL_HALF, HEAD_DIM), wrap(lambda h, r, n: (jnp.minimum((n + 1) * per, nb64 - 1), col(h, r))))
    return prev, main, nxt


def _dil_fwd(proj, bias_a, dilation, name):
    s = proj.shape[0]
    l = s // dilation
    tq = _dil_tile(l)
    tk = tq + 2 * DIL_HALF
    nq = l // tq
    scale = HEAD_DIM ** -0.5
    view = proj.reshape(l, dilation * MAIN_WIDTH)

    def body(q_ref, kp_ref, km_ref, kn_ref, vp_ref, vm_ref, vn_ref, b_ref, o_ref, lse_ref):
        n = pl.program_id(2)
        q = q_ref[...].astype(BF16)
        kc = jnp.concatenate([kp_ref[...], km_ref[...], kn_ref[...]], axis=0).astype(BF16)
        vc = jnp.concatenate([vp_ref[...], vm_ref[...], vn_ref[...]], axis=0).astype(BF16)
        sc = _dot(q, kc, NT) * scale + b_ref[0]
        qpos = n * tq + lax.broadcasted_iota(jnp.int32, (tq, tk), 0)
        kpos = n * tq - DIL_HALF + lax.broadcasted_iota(jnp.int32, (tq, tk), 1)
        mask = (jnp.abs(kpos - qpos) <= DIL_HALF) & (kpos >= 0) & (kpos < l)
        sc = jnp.where(mask, sc, NEG_INF)
        m = jnp.max(sc, axis=-1, keepdims=True)
        p = jnp.exp(sc - m)
        den = jnp.sum(p, axis=-1, keepdims=True)
        o_ref[...] = _dot(p.astype(BF16), vc, NN) / den
        lse_ref[...] = jnp.broadcast_to(m + jnp.log(den), (tq, HEAD_DIM))

    _, q_main, _ = _dil_specs(l, tq, dilation, MAIN_WIDTH, OFF_DQ, False)
    k_specs = _dil_specs(l, tq, dilation, MAIN_WIDTH, OFF_DK, False)
    v_specs = _dil_specs(l, tq, dilation, MAIN_WIDTH, OFF_DV, False)
    _, o_main, _ = _dil_specs(l, tq, dilation, HEADS * HEAD_DIM, 0, False)
    o, lse = pl.pallas_call(
        body, name=name, grid=(dilation, HEADS, nq),
        in_specs=[q_main, *k_specs, *v_specs, pl.BlockSpec((1, tq, tk), lambda r, h, n: (h, 0, 0))],
        out_specs=[o_main, o_main],
        out_shape=[jax.ShapeDtypeStruct((l, dilation * HEADS * HEAD_DIM), F32)] * 2,
        compiler_params=_params(("parallel", "parallel", "parallel")),
    )(view, view, view, view, view, view, view, bias_a)
    return o.reshape(s, HEADS * HEAD_DIM), lse.reshape(s, HEADS * HEAD_DIM)


def _dil_bwd_q(proj, d_o, lse, delta, bias_a, prev_dq, dilation, name):
    s = proj.shape[0]
    l = s // dilation
    tq = _dil_tile(l)
    tk = tq + 2 * DIL_HALF
    nq = l // tq
    scale = HEAD_DIM ** -0.5
    w4 = HEADS * HEAD_DIM
    view = proj.reshape(l, dilation * MAIN_WIDTH)
    small = lambda t: t.reshape(l, dilation * w4)
    has_prev = prev_dq is not None

    def body(*refs):
        q_ref, kp_ref, km_ref, kn_ref, vp_ref, vm_ref, vn_ref, b_ref, do_ref, lse_ref, dl_ref = refs[:11]
        refs = refs[11:]
        if has_prev:
            pq_ref, refs = refs[0], refs[1:]
        dq_ref, db_ref = refs
        r, n = pl.program_id(1), pl.program_id(2)
        q = q_ref[...].astype(BF16)
        kc = jnp.concatenate([kp_ref[...], km_ref[...], kn_ref[...]], axis=0).astype(BF16)
        vc = jnp.concatenate([vp_ref[...], vm_ref[...], vn_ref[...]], axis=0).astype(BF16)
        sc = _dot(q, kc, NT) * scale + b_ref[0]
        qpos = n * tq + lax.broadcasted_iota(jnp.int32, (tq, tk), 0)
        kpos = n * tq - DIL_HALF + lax.broadcasted_iota(jnp.int32, (tq, tk), 1)
        mask = (jnp.abs(kpos - qpos) <= DIL_HALF) & (kpos >= 0) & (kpos < l)
        p = jnp.where(mask, jnp.exp(sc - lse_ref[:, 0:1]), 0.0)
        dp = _dot(do_ref[...].astype(BF16), vc, NT)
        ds = p * (dp - dl_ref[:, 0:1])
        dq = _dot(ds.astype(BF16), kc, NN) * scale
        if has_prev:
            dq = dq + pq_ref[...]
        dq_ref[...] = dq

        @pl.when((r == 0) & (n == 0))
        def _():
            db_ref[...] = jnp.zeros_like(db_ref)

        db_ref[0] += ds

    _, q_main, _ = _dil_specs(l, tq, dilation, MAIN_WIDTH, OFF_DQ, True)
    k_specs = _dil_specs(l, tq, dilation, MAIN_WIDTH, OFF_DK, True)
    v_specs = _dil_specs(l, tq, dilation, MAIN_WIDTH, OFF_DV, True)
    _, o_main, _ = _dil_specs(l, tq, dilation, w4, 0, True)
    bias_spec = pl.BlockSpec((1, tq, tk), lambda h, r, n: (h, 0, 0))
    in_specs = [q_main, *k_specs, *v_specs, bias_spec, o_main, o_main, o_main] + ([o_main] if has_prev else [])
    args = [view] * 7 + [bias_a, small(d_o), small(lse), small(delta)] + ([small(prev_dq)] if has_prev else [])
    dq, dbias = pl.pallas_call(
        body, name=name, grid=(HEADS, dilation, nq),
        in_specs=in_specs,
        out_specs=[o_main, bias_spec],
        out_shape=[jax.ShapeDtypeStruct((l, dilation * w4), F32), jax.ShapeDtypeStruct((HEADS, tq, tk), F32)],
        compiler_params=_params(("arbitrary", "arbitrary", "arbitrary")),
    )(*args)
    return dq.reshape(s, w4), dbias


def _dil_bwd_kv(proj, d_o, lse, delta, bias_b, prev, dilation, name):
    s = proj.shape[0]
    l = s // dilation
    tq = _dil_tile(l)
    tw = tq + 2 * DIL_HALF
    nq = l // tq
    scale = HEAD_DIM ** -0.5
    w4 = HEADS * HEAD_DIM
    view = proj.reshape(l, dilation * MAIN_WIDTH)
    small = lambda t: t.reshape(l, dilation * w4)
    has_prev = prev is not None

    def body(*refs):
        (qp_ref, qm_ref, qn_ref, k_ref, v_ref, b_ref, dop_ref, dom_ref, don_ref,
         lp_ref, lm_ref, ln_ref, dp_ref, dm_ref, dn_ref) = refs[:15]
        refs = refs[15:]
        if has_prev:
            pk_ref, pv_ref = refs[:2]
            refs = refs[2:]
        dk_ref, dv_ref = refs
        n = pl.program_id(2)
        cat = lambda a, b_, c: jnp.concatenate([a[...], b_[...], c[...]], axis=0)
        qc = cat(qp_ref, qm_ref, qn_ref).astype(BF16)
        doc = cat(dop_ref, dom_ref, don_ref).astype(BF16)
        lsec = cat(lp_ref, lm_ref, ln_ref)[:, 0:1]
        dlc = cat(dp_ref, dm_ref, dn_ref)[:, 0:1]
        kb = k_ref[...].astype(BF16)
        vb = v_ref[...].astype(BF16)
        sc = _dot(qc, kb, NT) * scale + b_ref[0]
        qpos = n * tq - DIL_HALF + lax.broadcasted_iota(jnp.int32, (tw, tq), 0)
        kpos = n * tq + lax.broadcasted_iota(jnp.int32, (tw, tq), 1)
        mask = (jnp.abs(kpos - qpos) <= DIL_HALF) & (qpos >= 0) & (qpos < l)
        p = jnp.where(mask, jnp.exp(sc - lsec), 0.0)
        dv = _dot(p.astype(BF16), doc, TN)
        dp = _dot(doc, vb, NT)
        ds = p * (dp - dlc)
        dk = _dot(ds.astype(BF16), qc, TN) * scale
        if has_prev:
            dk = dk + pk_ref[...]
            dv = dv + pv_ref[...]
        dk_ref[...] = dk
        dv_ref[...] = dv

    q_specs = _dil_specs(l, tq, dilation, MAIN_WIDTH, OFF_DQ, False)
    _, k_main, _ = _dil_specs(l, tq, dilation, MAIN_WIDTH, OFF_DK, False)
    _, v_main, _ = _dil_specs(l, tq, dilation, MAIN_WIDTH, OFF_DV, False)
    o_specs = _dil_specs(l, tq, dilation, w4, 0, False)
    o_main = o_specs[1]
    in_specs = [*q_specs, k_main, v_main, pl.BlockSpec((1, tw, tq), lambda r, h, n: (h, 0, 0)),
                *o_specs, *o_specs, *o_specs] + ([o_main, o_main] if has_prev else [])
    args = ([view] * 5 + [bias_b] + [small(d_o)] * 3 + [small(lse)] * 3 + [small(delta)] * 3
            + ([small(prev[0]), small(prev[1])] if has_prev else []))
    dk, dv = pl.pallas_call(
        body, name=name, grid=(dilation, HEADS, nq),
        in_specs=in_specs,
        out_specs=[o_main, o_main],
        out_shape=[jax.ShapeDtypeStruct((l, dilation * w4), F32)] * 2,
        compiler_params=_params(("parallel", "parallel", "parallel")),
    )(*args)
    return dk.reshape(s, w4), dv.reshape(s, w4)


def _pcall(body, name, grid, in_specs, out_specs, out_shape, scratch_shapes, sem, args, cargo=()):
    n_in, n_out, n_c = len(in_specs), len(out_specs), len(cargo)
    if not n_c:
        return pl.pallas_call(body, name=name, grid=grid, in_specs=in_specs, out_specs=out_specs, out_shape=out_shape,
                              scratch_shapes=scratch_shapes, compiler_params=_params(sem))(*args)
    scatter = [sc for _, sc in cargo]
    n_scr = len(scratch_shapes)

    def wrapped(*refs):
        ins, refs = refs[:n_in], refs[n_in:]
        c_in, refs = refs[:n_c], refs[n_c:]
        outs, refs = refs[:n_out], refs[n_out:]
        c_out, refs = refs[:n_c], refs[n_c:]
        scr, sems = refs[:n_scr], refs[n_scr:]
        ids = [pl.program_id(a) for a in range(len(grid))]
        first = functools.reduce(lambda p, q: p & q, [i == 0 for i in ids])
        last = functools.reduce(lambda p, q: p & q, [i == g - 1 for i, g in zip(ids, grid)])

        @pl.when(first)
        def _():
            _cargo_start(c_in, c_out, sems, scatter)

        body(*ins, *outs, *scr)

        @pl.when(last)
        def _():
            _cargo_wait(c_in, c_out, sems, scatter)

    any_spec = pl.BlockSpec(memory_space=pl.ANY)
    return pl.pallas_call(
        wrapped, name=name, grid=grid,
        in_specs=list(in_specs) + [any_spec] * n_c, out_specs=list(out_specs) + [any_spec] * n_c,
        out_shape=list(out_shape) + _cargo_shapes(cargo),
        scratch_shapes=list(scratch_shapes) + _cargo_sems(n_c),
        compiler_params=_params(("arbitrary",) * len(grid)),
    )(*args, *[x for x, _ in cargo])


def _gla4_fwd(proj, lr, up_pad, bias, reverse, name, cargo=()):
    s = proj.shape[0]
    ts = _gla_rows(s)
    nblk, cpb = s // ts, ts // GLA_CHUNK
    scale = GLA_DK ** -0.5
    wk, wv = HEADS * GLA_DK, HEADS * GLA_DV

    def blk(i):
        return (nblk - 1 - i) if reverse else i

    def body(q_ref, k_ref, v_ref, lr_ref, up_ref, b_ref, o_ref, st_ref, state):
        @pl.when(pl.program_id(0) == 0)
        def _():
            state[...] = jnp.zeros_like(state)

        tri = _gla_tri(reverse)
        z = _dot(lr_ref[...].astype(BF16), up_ref[...], NN) + b_ref[...]
        g_all = _log_sigmoid(z) * (1.0 / GLA_GATE_NORMALIZER)
        order = range(cpb - 1, -1, -1) if reverse else range(cpb)
        for c in order:
            sl = slice(c * GLA_CHUNK, (c + 1) * GLA_CHUNK)
            b_all = _dot(tri.astype(F32), g_all[sl, :], NN, precision=lax.Precision.HIGHEST)
            for h in range(HEADS):
                hk = slice(h * GLA_DK, (h + 1) * GLA_DK)
                hv = slice(h * GLA_DV, (h + 1) * GLA_DV)
                b = b_all[:, hk]
                bl = b[0:1] if reverse else b[GLA_CHUNK - 1:GLA_CHUNK]
                kc = k_ref[sl, hk]
                qdb = (q_ref[sl, hk] * scale * jnp.exp(b)).astype(BF16)
                ki = kc * jnp.exp(-b)
                ke = kc * jnp.exp(bl - b)
                a = jnp.where(tri, _dot(qdb, ki.astype(BF16), NT), 0.0)
                vb = v_ref[sl, hv].astype(BF16)
                st = state[h]
                o_ref[sl, hv] = _dot(a.astype(BF16), vb, NN) + _dot(qdb, st.astype(BF16), NT)
                st_ref[h, c] = st
                state[h] = st * jnp.exp(bl) + _dot(vb, ke.astype(BF16), TN)

    return _pcall(
        body, name, (nblk,),
        [pl.BlockSpec((ts, wk), lambda i: (blk(i), OFF_GQ // wk)), pl.BlockSpec((ts, wk), lambda i: (blk(i), OFF_GK // wk)),
         pl.BlockSpec((ts, wv), lambda i: (blk(i), OFF_GV // wv)), pl.BlockSpec((ts, LR_PAD), lambda i: (blk(i), 0)),
         pl.BlockSpec((LR_PAD, wk), lambda i: (0, 0)), pl.BlockSpec((1, wk), lambda i: (0, 0))],
        [pl.BlockSpec((ts, wv), lambda i: (blk(i), 0)),
         pl.BlockSpec((HEADS, cpb, GLA_DV, GLA_DK), lambda i: (0, blk(i), 0, 0))],
        [jax.ShapeDtypeStruct((s, wv), F32), jax.ShapeDtypeStruct((HEADS, s // GLA_CHUNK, GLA_DV, GLA_DK), F32)],
        [pltpu.VMEM((HEADS, GLA_DV, GLA_DK), F32)], ("arbitrary",),
        (proj, proj, proj, lr, up_pad, bias), cargo)


def _gla4_bwd(proj, lr, up_pad, bias, states, d_o, prev, reverse, name, cargo=()):
    s = proj.shape[0]
    ts = _gla_rows(s)
    nblk, cpb = s // ts, ts // GLA_CHUNK
    scale = GLA_DK ** -0.5
    wk, wv = HEADS * GLA_DK, HEADS * GLA_DV
    has_prev = prev is not None

    def blk(i):
        return i if reverse else (nblk - 1 - i)

    def body(*refs):
        q_ref, k_ref, v_ref, lr_ref, up_ref, b_ref, st_ref, do_ref = refs[:8]
        refs = refs[8:]
        if has_prev:
            pq_ref, pk_ref, pv_ref = refs[:3]
            refs = refs[3:]
        dq_ref, dk_ref, dv_ref, dz_ref, dstate = refs

        @pl.when(pl.program_id(0) == 0)
        def _():
            dstate[...] = jnp.zeros_like(dstate)

        tri = _gla_tri(reverse)
        tri_t = _gla_tri(not reverse)
        row = lax.broadcasted_iota(jnp.int32, (GLA_CHUNK, GLA_DK), 0)
        last_row = (row == 0) if reverse else (row == GLA_CHUNK - 1)
        z = _dot(lr_ref[...].astype(BF16), up_ref[...], NN) + b_ref[...]
        g_all = _log_sigmoid(z) * (1.0 / GLA_GATE_NORMALIZER)
        dgate = (1.0 / GLA_GATE_NORMALIZER) * (1.0 - jax.nn.sigmoid(z))
        order = range(cpb) if reverse else range(cpb - 1, -1, -1)
        for c in order:
            sl = slice(c * GLA_CHUNK, (c + 1) * GLA_CHUNK)
            b_all = _dot(tri.astype(F32), g_all[sl, :], NN, precision=lax.Precision.HIGHEST)
            for h in range(HEADS):
                hk = slice(h * GLA_DK, (h + 1) * GLA_DK)
                hv = slice(h * GLA_DV, (h + 1) * GLA_DV)
                b = b_all[:, hk]
                bl = b[0:1] if reverse else b[GLA_CHUNK - 1:GLA_CHUNK]
                eb = jnp.exp(b)
                kc = k_ref[sl, hk]
                qd = q_ref[sl, hk] * scale * eb
                ki = kc * jnp.exp(-b)
                ke = kc * jnp.exp(bl - b)
                qdb, kib, keb = qd.astype(BF16), ki.astype(BF16), ke.astype(BF16)
                a = jnp.where(tri, _dot(qdb, kib, NT), 0.0)
                vb = v_ref[sl, hv].astype(BF16)
                dob = do_ref[sl, hv].astype(BF16)
                st = st_ref[h, c]
                dst = dstate[h]
                dstb = dst.astype(BF16)
                da = jnp.where(tri, _dot(dob, vb, NT), 0.0).astype(BF16)
                dv = _dot(a.astype(BF16), dob, TN) + _dot(keb, dstb, NT)
                dqd = _dot(da, kib, NN) + _dot(dob, st.astype(BF16), NN)
                dki = _dot(da, qdb, TN)
                dke = _dot(vb, dstb, NN)
                decay = jnp.exp(bl)
                dbl = decay * jnp.sum(dst * st, axis=0, keepdims=True) + jnp.sum(dke * ke, axis=0, keepdims=True)
                dstate[h] = dst * decay + _dot(dob, qdb, TN)
                db = dqd * qd - dki * ki - dke * ke + jnp.where(last_row, dbl, 0.0)
                dg = _dot(tri_t.astype(F32), db, NN, precision=lax.Precision.HIGHEST)
                dq = dqd * eb * scale
                dk = dki * jnp.exp(-b) + dke * jnp.exp(bl - b)
                if has_prev:
                    dq = dq + pq_ref[sl, hk]
                    dk = dk + pk_ref[sl, hk]
                    dv = dv + pv_ref[sl, hv]
                dq_ref[sl, hk] = dq
                dk_ref[sl, hk] = dk
                dv_ref[sl, hv] = dv
                dz_ref[sl, hk] = dg * dgate[sl, hk]

    rk = pl.BlockSpec((ts, wk), lambda i: (blk(i), 0))
    rv = pl.BlockSpec((ts, wv), lambda i: (blk(i), 0))
    in_specs = [pl.BlockSpec((ts, wk), lambda i: (blk(i), OFF_GQ // wk)),
                pl.BlockSpec((ts, wk), lambda i: (blk(i), OFF_GK // wk)),
                pl.BlockSpec((ts, wv), lambda i: (blk(i), OFF_GV // wv)),
                pl.BlockSpec((ts, LR_PAD), lambda i: (blk(i), 0)),
                pl.BlockSpec((LR_PAD, wk), lambda i: (0, 0)), pl.BlockSpec((1, wk), lambda i: (0, 0)),
                pl.BlockSpec((HEADS, cpb, GLA_DV, GLA_DK), lambda i: (0, blk(i), 0, 0)), rv]
    args = [proj, proj, proj, lr, up_pad, bias, states, d_o]
    if has_prev:
        in_specs += [rk, rk, rv]
        args += list(prev)
    return _pcall(
        body, name, (nblk,), in_specs, [rk, rk, rv, rk],
        [jax.ShapeDtypeStruct((s, wk), F32), jax.ShapeDtypeStruct((s, wk), F32),
         jax.ShapeDtypeStruct((s, wv), F32), jax.ShapeDtypeStruct((s, wk), F32)],
        [pltpu.VMEM((HEADS, GLA_DV, GLA_DK), F32)], ("arbitrary",), args, cargo)


DILATIONS = tuple(d for _, d in DIL_CONFIGS)
DIL_HALO = DIL_HALF * max(DILATIONS)
DIL_UNROLL = 16


def _dilf_block(s):
    return min(s, DIL_HALO)


def _dilf_tq(block, dilation):
    return min(128, block // dilation)


def _dilf_specs(s, block, off, width):
    nb = s // block
    col = lambda h: off // HEAD_DIM + h
    prev = pl.BlockSpec((block, HEAD_DIM), lambda h, n: (jnp.maximum(n - 1, 0), col(h)))
    main = pl.BlockSpec((block, HEAD_DIM), lambda h, n: (n, col(h)))
    nxt = pl.BlockSpec((block, HEAD_DIM), lambda h, n: (jnp.minimum(n + 1, nb - 1), col(h)))
    return prev, main, nxt


def _dilf_rows(start, count, dilation):
    if dilation == 1:
        return pl.ds(pl.multiple_of(start, 8), count)
    return pl.ds(start, count, stride=dilation)


def _dilf_fwd(proj, biases, name, cargo=()):
    s = proj.shape[0]
    blk = _dilf_block(s)
    assert s % blk == 0 and blk == DIL_HALO, s
    halo = blk
    scale = HEAD_DIM ** -0.5
    w4 = HEADS * HEAD_DIM
    nbr = len(DILATIONS)

    def body(q_ref, kp_ref, km_ref, kn_ref, vp_ref, vm_ref, vn_ref, *rest):
        b_refs, (o_ref, lse_ref, kw, vw, o_scr, l_scr) = rest[:nbr], rest[nbr:]
        p0 = pl.program_id(1) * blk
        for w_ref, parts in ((kw, (kp_ref, km_ref, kn_ref)), (vw, (vp_ref, vm_ref, vn_ref))):
            w_ref[0:halo, :] = parts[0][...]
            w_ref[halo:halo + blk, :] = parts[1][...]
            w_ref[halo + blk:, :] = parts[2][...]
        for bi, d in enumerate(DILATIONS):
            tq = _dilf_tq(blk, d)
            tk = tq + 2 * DIL_HALF
            ii = lax.broadcasted_iota(jnp.int32, (tq, tk), 0)
            jj = lax.broadcasted_iota(jnp.int32, (tq, tk), 1)
            band = jnp.abs(jj - DIL_HALF - ii) <= DIL_HALF
            bias = b_refs[bi][0]

            def tile(i, carry, d=d, tq=tq, tk=tk, band=band, bias=bias, jj=jj, bi=bi):
                start = (i % d) + d * tq * (i // d)
                wstart = halo - DIL_HALF * d + start
                q = q_ref[_dilf_rows(start, tq, d), :].astype(BF16)
                k = kw[_dilf_rows(wstart, tk, d), :].astype(BF16)
                v = vw[_dilf_rows(wstart, tk, d), :].astype(BF16)
                kpos = p0 + start + d * (jj - DIL_HALF)
                mask = band & (kpos >= 0) & (kpos < s)
                sc = jnp.where(mask, _dot(q, k, NT) * scale + bias, NEG_INF)
                m = jnp.max(sc, axis=-1, keepdims=True)
                p = jnp.exp(sc - m)
                den = jnp.sum(p, axis=-1, keepdims=True)
                o_scr[bi, _dilf_rows(start, tq, d), :] = _dot(p.astype(BF16), v, NN) / den
                l_scr[bi, _dilf_rows(start, tq, d), :] = jnp.broadcast_to(m + jnp.log(den), (tq, HEAD_DIM))
                return carry

            lax.fori_loop(0, blk // tq, tile, 0, unroll=min(DIL_UNROLL, blk // tq))
        ls = [l_scr[bi] for bi in range(nbr)]
        m = functools.reduce(jnp.maximum, ls)
        es = [jnp.exp(l - m) for l in ls]
        den = functools.reduce(lambda a_, b_: a_ + b_, es)
        num = functools.reduce(lambda a_, b_: a_ + b_, [e * o_scr[bi] for bi, e in enumerate(es)])
        o_ref[...] = num / den
        lse_ref[...] = m + jnp.log(den)

    _, q_main, _ = _dilf_specs(s, blk, OFF_DQ, MAIN_WIDTH)
    k_specs = _dilf_specs(s, blk, OFF_DK, MAIN_WIDTH)
    v_specs = _dilf_specs(s, blk, OFF_DV, MAIN_WIDTH)
    _, o_main, _ = _dilf_specs(s, blk, 0, w4)
    b_specs = [pl.BlockSpec((1,) + b.shape[1:], lambda h, n: (h, 0, 0)) for b in biases]
    return _pcall(
        body, name, (HEADS, s // blk), [q_main, *k_specs, *v_specs, *b_specs], [o_main, o_main],
        [jax.ShapeDtypeStruct((s, w4), F32)] * 2,
        [pltpu.VMEM((blk + 2 * halo, HEAD_DIM), F32)] * 2 + [pltpu.VMEM((nbr, blk, HEAD_DIM), F32)] * 2,
        ("parallel", "parallel"), [proj] * 7 + list(biases), cargo)


def _dilf_bwd_q(proj, d_o, lse, delta, biases, name, cargo=()):
    s = proj.shape[0]
    blk = _dilf_block(s)
    halo = blk
    scale = HEAD_DIM ** -0.5
    w4 = HEADS * HEAD_DIM
    nbr = len(DILATIONS)

    def body(q_ref, kp_ref, km_ref, kn_ref, vp_ref, vm_ref, vn_ref, do_ref, lse_ref, dl_ref, *rest):
        b_refs, rest = rest[:nbr], rest[nbr:]
        dq_ref, db_refs, (kw, vw) = rest[0], rest[1:1 + nbr], rest[1 + nbr:]
        n = pl.program_id(1)
        p0 = n * blk
        for w_ref, parts in ((kw, (kp_ref, km_ref, kn_ref)), (vw, (vp_ref, vm_ref, vn_ref))):
            w_ref[0:halo, :] = parts[0][...]
            w_ref[halo:halo + blk, :] = parts[1][...]
            w_ref[halo + blk:, :] = parts[2][...]
        dq_ref[...] = jnp.zeros_like(dq_ref)
        for bi, d in enumerate(DILATIONS):
            tq = _dilf_tq(blk, d)
            tk = tq + 2 * DIL_HALF
            ii = lax.broadcasted_iota(jnp.int32, (tq, tk), 0)
            jj = lax.broadcasted_iota(jnp.int32, (tq, tk), 1)
            band = jnp.abs(jj - DIL_HALF - ii) <= DIL_HALF
            bias = b_refs[bi][0]
            db_ref = db_refs[bi]

            @pl.when(n == 0)
            def _(db_ref=db_ref):
                db_ref[...] = jnp.zeros_like(db_ref)

            def tile(i, carry, d=d, tq=tq, tk=tk, band=band, bias=bias, jj=jj, db_ref=db_ref):
                start = (i % d) + d * tq * (i // d)
                wstart = halo - DIL_HALF * d + start
                rows = _dilf_rows(start, tq, d)
                q = q_ref[rows, :].astype(BF16)
                k = kw[_dilf_rows(wstart, tk, d), :].astype(BF16)
                v = vw[_dilf_rows(wstart, tk, d), :].astype(BF16)
                kpos = p0 + start + d * (jj - DIL_HALF)
                mask = band & (kpos >= 0) & (kpos < s)
                sc = _dot(q, k, NT) * scale + bias
                p = jnp.where(mask, jnp.exp(sc - lse_ref[rows, :][:, 0:1]), 0.0)
                dp = _dot(do_ref[rows, :].astype(BF16), v, NT)
                ds = p * (dp - dl_ref[rows, :][:, 0:1])
                dq_ref[rows, :] += _dot(ds.astype(BF16), k, NN) * scale
                db_ref[0] += ds
                return carry

            lax.fori_loop(0, blk // tq, tile, 0, unroll=min(DIL_UNROLL, blk // tq))

    _, q_main, _ = _dilf_specs(s, blk, OFF_DQ, MAIN_WIDTH)
    k_specs = _dilf_specs(s, blk, OFF_DK, MAIN_WIDTH)
    v_specs = _dilf_specs(s, blk, OFF_DV, MAIN_WIDTH)
    _, o_main, _ = _dilf_specs(s, blk, 0, w4)
    b_specs = [pl.BlockSpec((1,) + b.shape[1:], lambda h, n: (h, 0, 0)) for b in biases]
    return _pcall(
        body, name, (HEADS, s // blk), [q_main, *k_specs, *v_specs, o_main, o_main, o_main, *b_specs],
        [o_main, *b_specs],
        [jax.ShapeDtypeStruct((s, w4), F32)] + [jax.ShapeDtypeStruct(b.shape, F32) for b in biases],
        [pltpu.VMEM((blk + 2 * halo, HEAD_DIM), F32)] * 2,
        ("arbitrary", "arbitrary"), [proj] * 7 + [d_o, lse, delta] + list(biases), cargo)


def _dilf_bwd_kv(proj, d_o, lse, delta, biases_b, name, cargo=()):
    s = proj.shape[0]
    blk = _dilf_block(s)
    halo = blk
    scale = HEAD_DIM ** -0.5
    w4 = HEADS * HEAD_DIM
    nbr = len(DILATIONS)

    def body(qp_ref, qm_ref, qn_ref, k_ref, v_ref, dop_ref, dom_ref, don_ref, lp_ref, lm_ref, ln_ref,
             dp_ref, dm_ref, dn_ref, *rest):
        b_refs, (dk_ref, dv_ref, qw, dow, lw, dlw) = rest[:nbr], rest[nbr:]
        p0 = pl.program_id(1) * blk
        for w_ref, parts in ((qw, (qp_ref, qm_ref, qn_ref)), (dow, (dop_ref, dom_ref, don_ref)),
                             (lw, (lp_ref, lm_ref, ln_ref)), (dlw, (dp_ref, dm_ref, dn_ref))):
            w_ref[0:halo, :] = parts[0][...]
            w_ref[halo:halo + blk, :] = parts[1][...]
            w_ref[halo + blk:, :] = parts[2][...]
        dk_ref[...] = jnp.zeros_like(dk_ref)
        dv_ref[...] = jnp.zeros_like(dv_ref)
        for bi, d in enumerate(DILATIONS):
            tq = _dilf_tq(blk, d)
            tw = tq + 2 * DIL_HALF
            ii = lax.broadcasted_iota(jnp.int32, (tw, tq), 0)
            jj = lax.broadcasted_iota(jnp.int32, (tw, tq), 1)
            band = jnp.abs(jj + DIL_HALF - ii) <= DIL_HALF
            bias = b_refs[bi][0]

            def tile(i, carry, d=d, tq=tq, tw=tw, band=band, bias=bias, ii=ii):
                start = (i % d) + d * tq * (i // d)
                wstart = halo - DIL_HALF * d + start
                rows = _dilf_rows(start, tq, d)
                wrows = _dilf_rows(wstart, tw, d)
                kb = k_ref[rows, :].astype(BF16)
                vb = v_ref[rows, :].astype(BF16)
                qc = qw[wrows, :].astype(BF16)
                doc = dow[wrows, :].astype(BF16)
                qpos = p0 + start + d * (ii - DIL_HALF)
                mask = band & (qpos >= 0) & (qpos < s)
                sc = _dot(qc, kb, NT) * scale + bias
                p = jnp.where(mask, jnp.exp(sc - lw[wrows, :][:, 0:1]), 0.0)
                dp = _dot(doc, vb, NT)
                ds = p * (dp - dlw[wrows, :][:, 0:1])
                dv_ref[rows, :] += _dot(p.astype(BF16), doc, TN)
                dk_ref[rows, :] += _dot(ds.astype(BF16), qc, TN) * scale
                return carry

            lax.fori_loop(0, blk // tq, tile, 0, unroll=min(DIL_UNROLL, blk // tq))

    q_specs = _dilf_specs(s, blk, OFF_DQ, MAIN_WIDTH)
    _, k_main, _ = _dilf_specs(s, blk, OFF_DK, MAIN_WIDTH)
    _, v_main, _ = _dilf_specs(s, blk, OFF_DV, MAIN_WIDTH)
    o_specs = _dilf_specs(s, blk, 0, w4)
    b_specs = [pl.BlockSpec((1,) + b.shape[1:], lambda h, n: (h, 0, 0)) for b in biases_b]
    return _pcall(
        body, name, (HEADS, s // blk), [*q_specs, k_main, v_main, *o_specs, *o_specs, *o_specs, *b_specs],
        [o_specs[1], o_specs[1]], [jax.ShapeDtypeStruct((s, w4), F32)] * 2,
        [pltpu.VMEM((blk + 2 * halo, HEAD_DIM), F32)] * 4,
        ("parallel", "parallel"), [proj] * 5 + [d_o] * 3 + [lse] * 3 + [delta] * 3 + list(biases_b), cargo)


def _mem_fwd(proj, kv, name):
    s = proj.shape[0]
    mlen = kv.shape[0]
    tq = _tile(s, 512, 8)
    scale = HEAD_DIM ** -0.5
    w4 = HEADS * HEAD_DIM

    def body(q_ref, k_ref, v_ref, o_ref, lse_ref):
        sc = _dot(q_ref[...].astype(BF16), k_ref[...].astype(BF16), NT) * scale
        m = jnp.max(sc, axis=-1, keepdims=True)
        e = jnp.exp(sc - m)
        den = jnp.sum(e, axis=-1, keepdims=True)
        o_ref[...] = _dot((e / den).astype(BF16), v_ref[...].astype(BF16), NN)
        lse_ref[...] = jnp.broadcast_to(m + jnp.log(den), (tq, HEAD_DIM))

    o_spec = pl.BlockSpec((tq, HEAD_DIM), lambda h, n: (n, h))
    return pl.pallas_call(
        body, name=name, grid=(HEADS, s // tq),
        in_specs=[pl.BlockSpec((tq, HEAD_DIM), lambda h, n: (n, OFF_MQ // HEAD_DIM + h)),
                  pl.BlockSpec((mlen, HEAD_DIM), lambda h, n: (0, h)),
                  pl.BlockSpec((mlen, HEAD_DIM), lambda h, n: (0, HEADS + h))],
        out_specs=[o_spec, o_spec],
        out_shape=[jax.ShapeDtypeStruct((s, w4), F32)] * 2,
        compiler_params=_params(("parallel", "parallel")),
    )(proj, kv, kv)


def _mem_bwd(proj, kv, d_o, lse, delta, name):
    s = proj.shape[0]
    mlen = kv.shape[0]
    tq = _tile(s, 512, 8)
    scale = HEAD_DIM ** -0.5
    w4 = HEADS * HEAD_DIM

    def body(q_ref, k_ref, v_ref, do_ref, lse_ref, dl_ref, dq_ref, dk_ref, dv_ref):
        qb = q_ref[...].astype(BF16)
        kb = k_ref[...].astype(BF16)
        dob = do_ref[...].astype(BF16)
        sc = _dot(qb, kb, NT) * scale
        p = jnp.exp(sc - lse_ref[:, 0:1])
        dp = _dot(dob, v_ref[...].astype(BF16), NT)
        ds = (p * (dp - dl_ref[:, 0:1])).astype(BF16)
        dq_ref[...] = _dot(ds, kb, NN) * scale

        @pl.when(pl.program_id(1) == 0)
        def _():
            dk_ref[...] = jnp.zeros_like(dk_ref)
            dv_ref[...] = jnp.zeros_like(dv_ref)

        dk_ref[...] += _dot(ds, qb, TN) * scale
        dv_ref[...] += _dot(p.astype(BF16), dob, TN)

    o_spec = pl.BlockSpec((tq, HEAD_DIM), lambda h, n: (n, h))
    k_spec = pl.BlockSpec((mlen, HEAD_DIM), lambda h, n: (0, h))
    v_spec = pl.BlockSpec((mlen, HEAD_DIM), lambda h, n: (0, HEADS + h))
    dq, dkv, dkv2 = pl.pallas_call(
        body, name=name, grid=(HEADS, s // tq),
        in_specs=[pl.BlockSpec((tq, HEAD_DIM), lambda h, n: (n, OFF_MQ // HEAD_DIM + h)),
                  k_spec, v_spec, o_spec, o_spec, o_spec],
        out_specs=[o_spec, k_spec, k_spec],
        out_shape=[jax.ShapeDtypeStruct((s, w4), F32), jax.ShapeDtypeStruct((mlen, w4), F32),
                   jax.ShapeDtypeStruct((mlen, w4), F32)],
        compiler_params=_params(("parallel", "arbitrary")),
    )(proj, kv, kv, d_o, lse, delta)
    return dq, dkv, dkv2


def _head_norm(o, gain, width):
    outs, xns = [], []
    for h in range(HEADS):
        oh = o[:, h * width:(h + 1) * width]
        r = lax.rsqrt(jnp.mean(oh * oh, axis=-1, keepdims=True) + EPS)
        xn = oh * r
        xns.append(xn)
        outs.append(xn * gain[:, h * width:(h + 1) * width])
    return outs, xns


def _head_norm_bwd(o, gain, dy, width):
    dos, dgs = [], []
    for h in range(HEADS):
        sl = slice(h * width, (h + 1) * width)
        oh = o[:, sl]
        r = lax.rsqrt(jnp.mean(oh * oh, axis=-1, keepdims=True) + EPS)
        xn = oh * r
        t = dy[:, sl] * gain[:, sl]
        dos.append(r * (t - xn * jnp.mean(t * xn, axis=-1, keepdims=True)))
        dgs.append(jnp.sum(dy[:, sl] * xn, axis=0, keepdims=True))
    return dos, dgs


def _mix_fwd(o_f, o_b, proj, dil_o, mem_o, g_gla, g_dil, g_mem, name):
    s = o_f.shape[0]
    tr = _tile(s, 256, 8)
    w4 = HEADS * HEAD_DIM
    wv = HEADS * GLA_DV

    def body(of_ref, ob_ref, r_ref, od_ref, mo_ref, gg_ref, gd_ref, gm_ref, mix_ref):
        o = of_ref[...] + ob_ref[...]
        normed, _ = _head_norm(o, gg_ref[...], GLA_DV)
        rv = r_ref[...]
        gate = rv * jax.nn.sigmoid(rv)
        for h in range(HEADS):
            mix_ref[:, h * GLA_DV:(h + 1) * GLA_DV] = (normed[h] * gate[:, h * GLA_DV:(h + 1) * GLA_DV]).astype(BF16)
        nd, _ = _head_norm(od_ref[...], gd_ref[...], HEAD_DIM)
        nm, _ = _head_norm(mo_ref[...], gm_ref[...], HEAD_DIM)
        for h in range(HEADS):
            mix_ref[:, wv + h * HEAD_DIM:wv + (h + 1) * HEAD_DIM] = nd[h].astype(BF16)
            mix_ref[:, wv + w4 + h * HEAD_DIM:wv + w4 + (h + 1) * HEAD_DIM] = nm[h].astype(BF16)

    rows = lambda w, c=0: pl.BlockSpec((tr, w), lambda i: (i, c))
    vec = lambda w: pl.BlockSpec((1, w), lambda i: (0, 0))
    return pl.pallas_call(
        body, name=name, grid=(s // tr,),
        in_specs=[rows(wv), rows(wv), rows(wv, OFF_GR // wv), rows(w4), rows(w4), vec(wv), vec(w4), vec(w4)],
        out_specs=rows(wv + 2 * w4),
        out_shape=jax.ShapeDtypeStruct((s, wv + 2 * w4), BF16),
        compiler_params=_params(("parallel",)),
    )(o_f, o_b, proj, dil_o, mem_o, g_gla, g_dil, g_mem)


def _mix_bwd(dmixed, o_f, o_b, proj, dil_o, mem_o, g_gla, g_dil, g_mem, name):
    s = o_f.shape[0]
    tr = _tile(s, 256, 8)
    w4 = HEADS * HEAD_DIM
    wv = HEADS * GLA_DV

    def body(dm_ref, of_ref, ob_ref, r_ref, od_ref, mo_ref, gg_ref, gd_ref, gm_ref,
             dog_ref, dr_ref, dod_ref, dld_ref, dom_ref, dlm_ref, dgg_ref, dgd_ref, dgm_ref):
        i = pl.program_id(0)

        @pl.when(i == 0)
        def _():
            dgg_ref[...] = jnp.zeros_like(dgg_ref)
            dgd_ref[...] = jnp.zeros_like(dgd_ref)
            dgm_ref[...] = jnp.zeros_like(dgm_ref)

        dm = dm_ref[...]
        o = of_ref[...] + ob_ref[...]
        normed, _ = _head_norm(o, gg_ref[...], GLA_DV)
        rv = r_ref[...]
        sg = jax.nn.sigmoid(rv)
        gate = rv * sg
        dgate = sg * (1.0 + rv * (1.0 - sg))
        d_gla = dm[:, :wv]
        for h in range(HEADS):
            sl = slice(h * GLA_DV, (h + 1) * GLA_DV)
            dr_ref[:, sl] = d_gla[:, sl] * normed[h] * dgate[:, sl]
        dos, dgs = _head_norm_bwd(o, gg_ref[...], d_gla * gate, GLA_DV)
        for h in range(HEADS):
            sl = slice(h * GLA_DV, (h + 1) * GLA_DV)
            dog_ref[:, sl] = dos[h]
            dgg_ref[:, sl] += dgs[h]
        for src_ref, g_ref, off, do_out, dl_out, dg_out in (
                (od_ref, gd_ref, wv, dod_ref, dld_ref, dgd_ref),
                (mo_ref, gm_ref, wv + w4, dom_ref, dlm_ref, dgm_ref)):
            src = src_ref[...]
            dos, dgs = _head_norm_bwd(src, g_ref[...], dm[:, off:off + w4], HEAD_DIM)
            for h in range(HEADS):
                sl = slice(h * HEAD_DIM, (h + 1) * HEAD_DIM)
                do_out[:, sl] = dos[h]
                dl_out[:, sl] = jnp.broadcast_to(
                    jnp.sum(dos[h] * src[:, sl], axis=-1, keepdims=True), (tr, HEAD_DIM))
                dg_out[:, sl] += dgs[h]

    rows = lambda w, c=0: pl.BlockSpec((tr, w), lambda i: (i, c))
    vec = lambda w: pl.BlockSpec((1, w), lambda i: (0, 0))
    sds = lambda w: jax.ShapeDtypeStruct((s, w), F32)
    vds = lambda w: jax.ShapeDtypeStruct((1, w), F32)
    return pl.pallas_call(
        body, name=name, grid=(s // tr,),
        in_specs=[rows(wv + 2 * w4), rows(wv), rows(wv), rows(wv, OFF_GR // wv), rows(w4), rows(w4),
                  vec(wv), vec(w4), vec(w4)],
        out_specs=[rows(wv), rows(wv), rows(w4), rows(w4), rows(w4), rows(w4), vec(wv), vec(w4), vec(w4)],
        out_shape=[sds(wv), sds(wv), sds(w4), sds(w4), sds(w4), sds(w4), vds(wv), vds(w4), vds(w4)],
        compiler_params=_params(("arbitrary",)),
    )(dmixed, o_f, o_b, proj, dil_o, mem_o, g_gla, g_dil, g_mem)


def _colsum(x, name, rows=512):
    s, w = x.shape
    tr = _tile(s, rows, 8)

    def body(x_ref, o_ref):
        @pl.when(pl.program_id(0) == 0)
        def _():
            o_ref[...] = jnp.zeros_like(o_ref)

        o_ref[...] += jnp.sum(x_ref[...], axis=0, keepdims=True)

    return pl.pallas_call(
        body, name=name, grid=(s // tr,),
        in_specs=[pl.BlockSpec((tr, w), lambda i: (i, 0))],
        out_specs=pl.BlockSpec((1, w), lambda i: (0, 0)),
        out_shape=jax.ShapeDtypeStruct((1, w), F32),
        compiler_params=_params(("arbitrary",)),
    )(x)


def _peer(k):
    x, y, c = lax.axis_index("x"), lax.axis_index("y"), lax.axis_index("c")
    kx, ky, kc = (k >> 2) & 1, (k >> 1) & 1, k & 1
    return (x ^ kx if kx else x, y ^ ky if ky else y, c ^ kc if kc else c)


def _my_index():
    return 4 * lax.axis_index("x") + 2 * lax.axis_index("y") + lax.axis_index("c")


def _cargo_shapes(cargo):
    return [jax.ShapeDtypeStruct(x.shape if sc else (N_DEV,) + x.shape, x.dtype) for x, sc in cargo]


def _cargo_sems(n):
    return [pltpu.SemaphoreType.DMA((n * (N_DEV - 1),)), pltpu.SemaphoreType.DMA((n * (N_DEV - 1),)),
            pltpu.SemaphoreType.DMA((n,))]


def _cargo_copies(in_refs, out_refs, sems, scatter, with_arrivals=True):
    send_sems, recv_sems, local_sems = sems
    me = _my_index()
    own, sends, arrivals = [], [], []
    for i, (src_ref, dst_ref) in enumerate(zip(in_refs, out_refs)):
        own.append(pltpu.make_async_copy(src_ref.at[me] if scatter[i] else src_ref, dst_ref.at[me], local_sems.at[i]))
        for k in range(1, N_DEV):
            peer = _peer(k)
            peer_idx = 4 * peer[0] + 2 * peer[1] + peer[2]
            src = src_ref.at[peer_idx] if scatter[i] else src_ref
            sem = i * (N_DEV - 1) + k - 1
            sends.append(pltpu.make_async_remote_copy(
                src_ref=src, dst_ref=dst_ref.at[me], send_sem=send_sems.at[sem], recv_sem=recv_sems.at[sem],
                device_id=peer, device_id_type=MESH))
            if with_arrivals:
                arrivals.append(pltpu.make_async_remote_copy(
                    src_ref=src, dst_ref=dst_ref.at[peer_idx], send_sem=send_sems.at[sem], recv_sem=recv_sems.at[sem],
                    device_id=peer, device_id_type=MESH))
    return own, sends, arrivals


def _cargo_start(in_refs, out_refs, sems, scatter):
    own, sends, _ = _cargo_copies(in_refs, out_refs, sems, scatter, with_arrivals=False)
    for cp in own + sends:
        cp.start()


def _cargo_wait(in_refs, out_refs, sems, scatter):
    own, sends, arrivals = _cargo_copies(in_refs, out_refs, sems, scatter)
    for cp in arrivals:
        cp.wait_recv()
    for cp in sends:
        cp.wait_send()
    for cp in own:
        cp.wait()


def _exchange(cargo, name):
    n = len(cargo)
    scatter = [sc for _, sc in cargo]

    def body(*refs):
        in_refs, out_refs, sems = refs[:n], refs[n:2 * n], refs[2 * n:]
        _cargo_start(in_refs, out_refs, sems, scatter)
        _cargo_wait(in_refs, out_refs, sems, scatter)

    any_spec = pl.BlockSpec(memory_space=pl.ANY)
    return pl.pallas_call(
        body, name=name,
        in_specs=[any_spec] * n, out_specs=[any_spec] * n, out_shape=_cargo_shapes(cargo),
        scratch_shapes=_cargo_sems(n),
        compiler_params=pltpu.CompilerParams(has_side_effects=True),
    )(*[x for x, _ in cargo])


def _adamw(parts, w, m, v, name, rows=256):
    r, c = w.shape
    tr = _tile(r, max(8, min(rows, ADAMW_TILE_ELEMS // c)), 8)
    c1 = 1.0 - ADAM_B1 ** ADAM_STEP
    c2 = 1.0 - ADAM_B2 ** ADAM_STEP

    def body(p_ref, w_ref, m_ref, v_ref, g_ref, d_ref, nm_ref, nv_ref):
        g = p_ref[0].astype(F32)
        for d in range(1, N_DEV):
            g = g + p_ref[d].astype(F32)
        nm = ADAM_B1 * m_ref[...] + (1.0 - ADAM_B1) * g
        nv = ADAM_B2 * v_ref[...] + (1.0 - ADAM_B2) * (g * g)
        m_hat = nm / c1
        v_hat = nv / c2
        g_ref[...] = g
        d_ref[...] = -ADAM_LR * (m_hat / (jnp.sqrt(v_hat) + ADAM_EPS) + ADAM_WD * w_ref[...])
        nm_ref[...] = nm
        nv_ref[...] = nv

    spec = pl.BlockSpec((tr, c), lambda i: (i, 0))
    return pl.pallas_call(
        body, name=name, grid=(r // tr,),
        in_specs=[pl.BlockSpec((N_DEV, tr, c), lambda i: (0, i, 0)), spec, spec, spec],
        out_specs=[spec] * 4,
        out_shape=[jax.ShapeDtypeStruct((r, c), F32)] * 4,
        compiler_params=_params(("parallel",)),
    )(parts, w, m, v)


def _adamw_layers(parts, w, m, v, name, rows=256):
    n_l, r, c = w.shape
    tr = _tile(r, max(8, min(rows, ADAMW_TILE_ELEMS // c)), 8)
    nb = r // tr
    c1 = 1.0 - ADAM_B1 ** ADAM_STEP
    c2 = 1.0 - ADAM_B2 ** ADAM_STEP

    def body(*refs):
        p_refs = refs[:n_l]
        w_ref, m_ref, v_ref, g_ref, d_ref, nm_ref, nv_ref = refs[n_l:]
        layer = pl.program_id(0)
        for ll in range(n_l):
            @pl.when(layer == ll)
            def _(p_ref=p_refs[ll]):
                g = p_ref[0].astype(F32)
                for d in range(1, N_DEV):
                    g = g + p_ref[d].astype(F32)
                nm = ADAM_B1 * m_ref[0] + (1.0 - ADAM_B1) * g
                nv = ADAM_B2 * v_ref[0] + (1.0 - ADAM_B2) * (g * g)
                g_ref[0] = g
                d_ref[0] = -ADAM_LR * ((nm / c1) / (jnp.sqrt(nv / c2) + ADAM_EPS) + ADAM_WD * w_ref[0])
                nm_ref[0] = nm
                nv_ref[0] = nv

    def part_spec(ll):
        return pl.BlockSpec((N_DEV, tr, c),
                            lambda l, i: (0, jnp.where(l == ll, i, jnp.where(l < ll, 0, nb - 1)), 0))

    spec = pl.BlockSpec((1, tr, c), lambda l, i: (l, i, 0))
    return pl.pallas_call(
        body, name=name, grid=(n_l, nb),
        in_specs=[part_spec(ll) for ll in range(n_l)] + [spec, spec, spec],
        out_specs=[spec] * 4,
        out_shape=[jax.ShapeDtypeStruct((n_l, r, c), F32)] * 4,
        compiler_params=_params(("arbitrary", "arbitrary")),
    )(*parts, w, m, v)


SMALL = ("norm_mix", "gla_gate_bias_fwd", "gla_gate_bias_bwd", "gla_norm", "rel_bias", "dil_norm", "mem_norm",
         "mem_out_norm", "norm_mlp", "norm_final")


def _pack(arrs, rows):
    flat = jnp.concatenate([a.reshape(-1) for a in arrs])
    return jnp.pad(flat, (0, rows * 128 - flat.shape[0])).reshape(rows, 128)


def _unpack(buf, shapes):
    flat = buf.reshape(-1)
    out, off = [], 0
    for shp in shapes:
        n = int(np.prod(shp))
        out.append(flat[off:off + n].reshape(shp))
        off += n
    return out


def _split_in(w):
    main = jnp.concatenate([w[..., :3072], w[..., 3104:]], axis=-1)
    lr = w[..., 3072:3104]
    pad = [(0, 0)] * (w.ndim - 1) + [(0, LR_PAD - 2 * GLA_RANK)]
    return main, jnp.pad(lr, pad)


def _join_in(main, lr):
    return jnp.concatenate([main[..., :3072], lr[..., :2 * GLA_RANK], main[..., 3072:]], axis=-1)


def kernel(x, mem, norm_mix, w_in, gla_gate_up_fwd, gla_gate_bias_fwd, gla_gate_up_bwd, gla_gate_bias_bwd, gla_norm, rel_bias, dil_norm, mem_norm, w_mem_kv, mem_out_norm, w_out, norm_mlp, w_up, w_down, norm_final, loss_target, m_norm_mix, m_w_in, m_gla_gate_up_fwd, m_gla_gate_bias_fwd, m_gla_gate_up_bwd, m_gla_gate_bias_bwd, m_gla_norm, m_rel_bias, m_dil_norm, m_mem_norm, m_w_mem_kv, m_mem_out_norm, m_w_out, m_norm_mlp, m_w_up, m_w_down, m_norm_final, v_norm_mix, v_w_in, v_gla_gate_up_fwd, v_gla_gate_bias_fwd, v_gla_gate_up_bwd, v_gla_gate_bias_bwd, v_gla_norm, v_rel_bias, v_dil_norm, v_mem_norm, v_w_mem_kv, v_mem_out_norm, v_w_out, v_norm_mlp, v_w_up, v_w_down, v_norm_final):
    weights = dict(norm_mix=norm_mix, w_in=w_in, gla_gate_up_fwd=gla_gate_up_fwd, gla_gate_bias_fwd=gla_gate_bias_fwd,
                   gla_gate_up_bwd=gla_gate_up_bwd, gla_gate_bias_bwd=gla_gate_bias_bwd, gla_norm=gla_norm,
                   rel_bias=rel_bias, dil_norm=dil_norm, mem_norm=mem_norm, w_mem_kv=w_mem_kv,
                   mem_out_norm=mem_out_norm, w_out=w_out, norm_mlp=norm_mlp, w_up=w_up, w_down=w_down,
                   norm_final=norm_final)
    mom1 = dict(norm_mix=m_norm_mix, w_in=m_w_in, gla_gate_up_fwd=m_gla_gate_up_fwd,
                gla_gate_bias_fwd=m_gla_gate_bias_fwd, gla_gate_up_bwd=m_gla_gate_up_bwd,
                gla_gate_bias_bwd=m_gla_gate_bias_bwd, gla_norm=m_gla_norm, rel_bias=m_rel_bias, dil_norm=m_dil_norm,
                mem_norm=m_mem_norm, w_mem_kv=m_w_mem_kv, mem_out_norm=m_mem_out_norm, w_out=m_w_out,
                norm_mlp=m_norm_mlp, w_up=m_w_up, w_down=m_w_down, norm_final=m_norm_final)
    mom2 = dict(norm_mix=v_norm_mix, w_in=v_w_in, gla_gate_up_fwd=v_gla_gate_up_fwd,
                gla_gate_bias_fwd=v_gla_gate_bias_fwd, gla_gate_up_bwd=v_gla_gate_up_bwd,
                gla_gate_bias_bwd=v_gla_gate_bias_bwd, gla_norm=v_gla_norm, rel_bias=v_rel_bias, dil_norm=v_dil_norm,
                mem_norm=v_mem_norm, w_mem_kv=v_w_mem_kv, mem_out_norm=v_mem_out_norm, w_out=v_w_out,
                norm_mlp=v_norm_mlp, w_up=v_w_up, w_down=v_w_down, norm_final=v_norm_final)

    s, d = x.shape[1], x.shape[2]
    xs = x.reshape(s, d)
    mems = mem.reshape(mem.shape[1], d)
    target = loss_target.reshape(s, d)
    me = _my_index()
    n_layers = w_in.shape[0]
    gate_w = gla_gate_up_fwd.shape[2]

    shard = lambda name, l: (weights[name][l].astype(BF16), False)
    cols = lambda t: jnp.moveaxis(t, 0, 1).reshape(t.shape[1], N_DEV * t.shape[2])
    rows = lambda t: t.reshape(N_DEV * t.shape[1], t.shape[2])
    in_names = ("w_in", "gla_gate_up_fwd", "gla_gate_up_bwd")

    def in_mats(g_in, g_upf, g_upb):
        w_main, w_lr = _split_in(cols(g_in))
        up_f, up_b = cols(g_upf), cols(g_upb)
        zeros_up = jnp.zeros((GLA_RANK, HEADS * GLA_DK), BF16)
        pad_rows = jnp.zeros((LR_PAD - 2 * GLA_RANK, HEADS * GLA_DK), BF16)
        up_pad_f = jnp.concatenate([up_f, zeros_up, pad_rows], axis=0)
        up_pad_b = jnp.concatenate([zeros_up, up_b, pad_rows], axis=0)
        return dict(w_main=w_main, w_lr=w_lr, up_pad_f=up_pad_f, up_pad_b=up_pad_b,
                    up_cat=jnp.concatenate([up_pad_f, up_pad_b], axis=1))

    def halves(name, l):
        w = weights[name][l].astype(BF16)
        return (w[:w.shape[0] // 2], False), (w[w.shape[0] // 2:], False)

    join_cols = lambda ga, gb: jnp.concatenate([cols(ga), cols(gb)], axis=0)

    in_a0, in_b0 = halves("w_in", 0)
    up_a0, up_b0 = halves("w_up", 0)
    g0 = _exchange([in_a0, in_b0, shard("gla_gate_up_fwd", 0), shard("gla_gate_up_bwd", 0), shard("w_mem_kv", 0),
                    shard("w_out", 0), up_a0], "ag_weights_0")
    wts = [dict() for _ in range(n_layers)]
    wts[0].update(in_mats(jnp.concatenate([g0[0], g0[1]], axis=1), g0[2], g0[3]), wkv=rows(g0[4]), wout=rows(g0[5]))

    row2 = lambda t: t.reshape(1, -1)

    bands, bias_a, bias_b = [], [], []
    for bi, dilation in enumerate(DILATIONS):
        band = _band_buckets(dilation)
        ba, bb = _bias_tiles(band, rel_bias, _dilf_tq(_dilf_block(s), dilation), f"bias_tiles_{bi}")
        bands.append(band), bias_a.append(ba), bias_b.append(bb)

    saved = []
    xl = xs
    up_a = g0[6]
    for l in range(n_layers):
        wl = wts[l]
        nxt = l + 1 < n_layers
        h = _rmsnorm_fwd(xl, row2(norm_mix[l]), f"norm_mix_{l}")
        cargo = ([up_b0] if l == 0 else []) + ([shard("w_mem_kv", l + 1), shard("w_out", l + 1)] if nxt else [])
        proj, *got = _mm(h, wl["w_main"], "nn", [F32], f"proj_{l}", cargo=cargo)
        if l == 0:
            wl["wup"] = join_cols(up_a, got[0])
            got = got[1:]
        if nxt:
            wts[l + 1].update(wkv=rows(got[0]), wout=rows(got[1]))
            in_a, in_b = halves("w_in", l + 1)
            up_a_n, up_b_n = halves("w_up", l + 1)
        (lr,) = _mm(h, wl["w_lr"], "nn", [F32], f"proj_lr_{l}")
        bias_f, bias_b_ = row2(gla_gate_bias_fwd[l]), row2(gla_gate_bias_bwd[l])
        o_f, st_f, *got_f = _gla4_fwd(proj, lr, wl["up_pad_f"], bias_f, False, f"gla_fwd_f_{l}", cargo=(
            [in_a, shard("gla_gate_up_fwd", l + 1), shard("gla_gate_up_bwd", l + 1)] if nxt else []))
        o_b, st_b, *got_b = _gla4_fwd(proj, lr, wl["up_pad_b"], bias_b_, True, f"gla_fwd_b_{l}",
                                      cargo=[in_b] if nxt else [])
        if nxt:
            wts[l + 1].update(in_mats(jnp.concatenate([got_f[0], got_b[0]], axis=1), got_f[1], got_f[2]))
        dil_o, dil_lse, *got = _dilf_fwd(proj, bias_a, f"dil_fwd_{l}", cargo=[up_a_n] if nxt else [])
        if nxt:
            up_a = got[0]
        hm = _rmsnorm_fwd(mems, row2(mem_norm[l]), f"norm_mem_{l}")
        (kv,) = _mm(hm, wl["wkv"], "nn", [F32], f"mem_kv_{l}")
        mem_o, mem_lse = _mem_fwd(proj, kv, f"mem_fwd_{l}")
        mixed = _mix_fwd(o_f, o_b, proj, dil_o, mem_o, row2(gla_norm[l]), row2(dil_norm[l]), row2(mem_out_norm[l]),
                         f"mix_fwd_{l}")
        (x1,) = _mm(mixed, wl["wout"], "nn", [F32], f"out_proj_{l}",
                    epilogue=lambda acc, res: (acc + res,), extras=(xl,))
        h2 = _rmsnorm_fwd(x1, row2(norm_mlp[l]), f"norm_mlp_{l}")
        cargo = ([shard("w_down", 0)] if l == 0 else []) + ([up_b_n] if nxt else [])
        a, u, *got = _mm(h2, wl["wup"], "nn", [F32, BF16], f"mlp_up_{l}",
                         epilogue=lambda acc: (acc, jnp.square(jnp.maximum(acc, 0.0))), cargo=cargo)
        if l == 0:
            wl["wdown"] = rows(got[0])
            got = got[1:]
        if nxt:
            wts[l + 1]["wup"] = join_cols(up_a, got[0])
        x2, *got = _mm(u, wl["wdown"], "nn", [F32], f"mlp_down_{l}",
                       epilogue=lambda acc, res: (acc + res,), extras=(x1,),
                       cargo=[shard("w_down", l + 1)] if nxt else [])
        if nxt:
            wts[l + 1]["wdown"] = rows(got[0])
        saved.append(dict(x0=xl, h=h, proj=proj, lr=lr, o_f=o_f, o_b=o_b, st_f=st_f, st_b=st_b, hm=hm, kv=kv,
                          mem_o=mem_o, mem_lse=mem_lse, mixed=mixed, dil_o=dil_o, dil_lse=dil_lse, x1=x1, h2=h2,
                          a=a, u=u))
        xl = x2

    dx, dx_bf, dg_final, loss_part = _loss_head(xl, row2(norm_final), target, "loss_head")

    to_cols = lambda t: jnp.moveaxis(t.reshape(t.shape[0], N_DEV, -1), 1, 0)
    to_rows = lambda t: t.reshape(N_DEV, -1, t.shape[1])
    tail_names = ("w_mem_kv", "w_in", "gla_gate_up_fwd", "gla_gate_up_bwd")
    recv = {nm: [None] * n_layers for nm in tail_names + ("w_out", "w_up", "w_down")}
    tail = None
    grads_small = {k: [None] * n_layers for k in SMALL}
    dbias_sum = [None] * len(DIL_CONFIGS)
    for l in range(n_layers - 1, -1, -1):
        sv = saved[l]
        wl = wts[l]
        dw_down, *got = _mm(sv["u"], dx_bf, "tn", [BF16], f"dw_down_{l}", cargo=tail or [])
        if tail:
            for nm, part in zip(tail_names, got):
                recv[nm][l + 1] = part
        (da,) = _mm(dx_bf, wl["wdown"], "nt", [BF16], f"d_mlp_act_{l}",
                    epilogue=lambda acc, a_: (acc * (2.0 * jnp.maximum(a_, 0.0)),), extras=(sv["a"],))
        dw_up, recv["w_down"][l] = _mm(sv["h2"], da, "tn", [BF16], f"dw_up_{l}", cargo=[(to_rows(dw_down), True)])
        dh2, recv["w_up"][l] = _mm(da, wl["wup"], "nt", [F32], f"d_norm_mlp_in_{l}", cargo=[(to_cols(dw_up), True)])
        dx1, dx1_bf, dg_mlp = _rmsnorm_bwd(sv["x1"], row2(norm_mlp[l]), dh2, dx, f"norm_mlp_bwd_{l}")
        (dw_out,) = _mm(sv["mixed"], dx1_bf, "tn", [BF16], f"dw_out_{l}")
        dmixed, recv["w_out"][l] = _mm(dx1_bf, wl["wout"], "nt", [F32], f"d_mixed_{l}",
                                       cargo=[(to_rows(dw_out), True)])
        (d_og, d_r, d_od, dl_d, d_om, dl_m, dg_gla, dg_dil, dg_memo) = _mix_bwd(
            dmixed, sv["o_f"], sv["o_b"], sv["proj"], sv["dil_o"], sv["mem_o"],
            row2(gla_norm[l]), row2(dil_norm[l]), row2(mem_out_norm[l]), f"mix_bwd_{l}")
        bias_f, bias_b_ = row2(gla_gate_bias_fwd[l]), row2(gla_gate_bias_bwd[l])
        dq1, dk1, dv1, dz_f = _gla4_bwd(sv["proj"], sv["lr"], wl["up_pad_f"], bias_f, sv["st_f"], d_og, None, False,
                                        f"gla_bwd_f_{l}")
        dq_g, dk_g, dv_g, dz_b = _gla4_bwd(sv["proj"], sv["lr"], wl["up_pad_b"], bias_b_, sv["st_b"], d_og,
                                           (dq1, dk1, dv1), True, f"gla_bwd_b_{l}")
        dz = jnp.concatenate([dz_f, dz_b], axis=1)
        (d_lr,) = _mm(dz, wl["up_cat"], "nt", [BF16], f"d_lowrank_{l}")
        (d_upcat,) = _mm(sv["lr"], dz, "tn", [BF16], f"dw_gate_up_{l}")
        dzsum = _colsum(dz, f"d_gate_bias_{l}")
        dq_d, *dbias = _dilf_bwd_q(sv["proj"], d_od, sv["dil_lse"], dl_d, bias_a, f"dil_bwd_q_{l}")
        dk_d, dv_d = _dilf_bwd_kv(sv["proj"], d_od, sv["dil_lse"], dl_d, bias_b, f"dil_bwd_kv_{l}")
        for bi in range(len(DILATIONS)):
            dbias_sum[bi] = dbias[bi] if dbias_sum[bi] is None else dbias_sum[bi] + dbias[bi]
        dq_m, dkm, dvm = _mem_bwd(sv["proj"], sv["kv"], d_om, sv["mem_lse"], dl_m, f"mem_bwd_{l}")
        dkv = jnp.concatenate([dkm, dvm], axis=1).astype(BF16)
        (dw_kv,) = _mm(sv["hm"], dkv, "tn", [BF16], f"dw_mem_kv_{l}")
        (dhm,) = _mm(dkv, wl["wkv"], "nt", [F32], f"d_mem_norm_in_{l}")
        _, _, dg_mem = _rmsnorm_bwd(mems, row2(mem_norm[l]), dhm, None, f"norm_mem_bwd_{l}")
        dproj = jnp.concatenate([t.astype(BF16) for t in (dq_g, dk_g, dv_g, d_r, dq_d, dk_d, dv_d, dq_m)], axis=1)
        (dw_main,) = _mm(sv["h"], dproj, "tn", [BF16], f"dw_in_{l}")
        (dw_lr,) = _mm(sv["h"], d_lr, "tn", [BF16], f"dw_in_lr_{l}")
        (dh_lr,) = _mm(d_lr, wl["w_lr"], "nt", [F32], f"d_norm_mix_in_lr_{l}")
        (dh,) = _mm(dproj, wl["w_main"], "nt", [F32], f"d_norm_mix_in_{l}",
                    epilogue=lambda acc, other: (acc + other,), extras=(dh_lr,))
        dx, dx_bf, dg_mix = _rmsnorm_bwd(sv["x0"], row2(norm_mix[l]), dh, dx1, f"norm_mix_bwd_{l}")
        tail = [(to_rows(dw_kv), True), (to_cols(_join_in(dw_main, dw_lr)), True),
                (to_cols(d_upcat[:GLA_RANK, :HEADS * GLA_DK]), True),
                (to_cols(d_upcat[GLA_RANK:2 * GLA_RANK, HEADS * GLA_DK:]), True)]

        grads_small["norm_mix"][l] = dg_mix
        grads_small["gla_gate_bias_fwd"][l] = dzsum[:, :HEADS * GLA_DK]
        grads_small["gla_gate_bias_bwd"][l] = dzsum[:, HEADS * GLA_DK:]
        grads_small["gla_norm"][l] = dg_gla
        grads_small["dil_norm"][l] = dg_dil
        grads_small["mem_norm"][l] = dg_mem
        grads_small["mem_out_norm"][l] = dg_memo
        grads_small["norm_mlp"][l] = dg_mlp

    for nm, part in zip(tail_names, _exchange(tail, "rs_tail")):
        recv[nm][0] = part
    d_table = _bias_grad(bands, dbias_sum, "bias_grad")[:, :HEADS]

    small_shapes = [weights[k].shape for k in SMALL]
    small_grads = []
    for k in SMALL:
        if k == "rel_bias":
            small_grads.append(d_table)
        elif k == "norm_final":
            small_grads.append(dg_final)
        else:
            small_grads.append(jnp.concatenate(grads_small[k], axis=0))
    n_small = sum(int(np.prod(shp)) for shp in small_shapes)
    small_rows = -(-(n_small + 128) // (8 * 128)) * 8
    pack = lambda arrs, extra: _pack(list(arrs) + [extra], small_rows)
    zeros_tail = jnp.zeros((128,), F32)
    (small_parts,) = _exchange([(pack(small_grads, loss_part.reshape(-1)), False)], "ag_small")
    sg, sd, sm, sv_ = _adamw(small_parts, pack([weights[k] for k in SMALL], zeros_tail),
                             pack([mom1[k] for k in SMALL], zeros_tail),
                             pack([mom2[k] for k in SMALL], zeros_tail), "adamw_small")
    loss = sg.reshape(-1)[n_small]
    small_out = [dict(zip(SMALL, _unpack(buf, small_shapes))) for buf in (sg, sd, sm, sv_)]

    big = {nm: _adamw_layers(recv[nm], weights[nm], mom1[nm], mom2[nm], f"adamw_{nm}") for nm in recv}

    order = ("norm_mix", "w_in", "gla_gate_up_fwd", "gla_gate_bias_fwd", "gla_gate_up_bwd", "gla_gate_bias_bwd",
             "gla_norm", "rel_bias", "dil_norm", "mem_norm", "w_mem_kv", "mem_out_norm", "w_out", "norm_mlp", "w_up",
             "w_down", "norm_final")
    outs = [loss, dx.reshape(x.shape)]
    for which in range(4):
        for name in order:
            outs.append(big[name][which] if name in big else small_out[which][name])
    return tuple(outs)
```

```python
import functools
import math

import numpy as np
import jax
import jax.numpy as jnp
from jax import lax
from jax.experimental import pallas as pl
from jax.experimental.pallas import tpu as pltpu

F32 = jnp.float32
BF16 = jnp.bfloat16

N_DEV = 8
DEPTH = 4
HEADS = 4
GLA_DK = 128
GLA_DV = 256
GLA_RANK = 16
GLA_GATE_NORMALIZER = 16.0
GLA_CHUNK = 64
HEAD_DIM = 128
DIL_CONFIGS = ((128, 1), (512, 4), (2048, 16))
DIL_HALF = 64
REL_BUCKETS = 32
REL_MAX_DISTANCE = 1024
EPS = 1e-6
NEG_INF = -1e30
IN_SPLITS = (512, 512, 1024, 1024, 16, 16, 512, 512, 512, 512)
IN_WIDTH = sum(IN_SPLITS)
MAIN_WIDTH = IN_WIDTH - 2 * GLA_RANK
LR_PAD = 128
OFF_GQ, OFF_GK, OFF_GV, OFF_GR, OFF_DQ, OFF_DK, OFF_DV, OFF_MQ = 0, 512, 1024, 2048, 3072, 3584, 4096, 4608

ADAM_LR = 0.001
ADAM_B1 = 0.9
ADAM_B2 = 0.999
ADAM_EPS = 1e-08
ADAM_WD = 0.01
ADAM_STEP = 10

VMEM_LIMIT = 56 * 1024 * 1024
ADAMW_TILE_ELEMS = 128 * 1024
MESH = pl.DeviceIdType.MESH

NN = ((1,), (0,))
NT = ((1,), (1,))
TN = ((0,), (0,))


def _dot(a, b, dims, precision=None):
    return lax.dot_general(a, b, (dims, ((), ())), preferred_element_type=F32, precision=precision)


def _tile(n, pref, mult=128):
    if n <= pref:
        return n
    t = (pref // mult) * mult
    while t >= mult:
        if n % t == 0:
            return t
        t -= mult
    return n


def _params(sem, **kw):
    return pltpu.CompilerParams(dimension_semantics=sem, vmem_limit_bytes=VMEM_LIMIT, **kw)


def _mm(a, b, mode, outs, name, epilogue=None, extras=(), tm=1024, tn=1024, tk=2048, cargo=()):
    if mode == "nn":
        (m, k), (k2, n) = a.shape, b.shape
    elif mode == "nt":
        (m, k), (n, k2) = a.shape, b.shape
    else:
        (k, m), (k2, n) = a.shape, b.shape
    assert k == k2, (a.shape, b.shape, mode)
    tm, tn, tk = _tile(m, tm), _tile(n, tn), _tile(k, tk)
    gi, gj, nk = m // tm, n // tn, k // tk
    if mode == "nn":
        a_spec = pl.BlockSpec((tm, tk), lambda i, j, kk: (i, kk))
        b_spec = pl.BlockSpec((tk, tn), lambda i, j, kk: (kk, j))
        dims = NN
    elif mode == "nt":
        a_spec = pl.BlockSpec((tm, tk), lambda i, j, kk: (i, kk))
        b_spec = pl.BlockSpec((tn, tk), lambda i, j, kk: (j, kk))
        dims = NT
    else:
        a_spec = pl.BlockSpec((tk, tm), lambda i, j, kk: (kk, i))
        b_spec = pl.BlockSpec((tk, tn), lambda i, j, kk: (kk, j))
        dims = TN
    tile_spec = pl.BlockSpec((tm, tn), lambda i, j, kk: (i, j))
    any_spec = pl.BlockSpec(memory_space=pl.ANY)
    n_extra, n_out, n_cargo = len(extras), len(outs), len(cargo)
    scatter = [sc for _, sc in cargo]
    if epilogue is None:
        epilogue = lambda acc: (acc,)

    def body(a_ref, b_ref, *rest):
        extra_refs, rest = rest[:n_extra], rest[n_extra:]
        cargo_in, rest = rest[:n_cargo], rest[n_cargo:]
        out_refs, rest = rest[:n_out], rest[n_out:]
        cargo_out, rest = rest[:n_cargo], rest[n_cargo:]
        i, j, kk = pl.program_id(0), pl.program_id(1), pl.program_id(2)
        if n_cargo:
            sems = rest[-3:]

            @pl.when((i == 0) & (j == 0) & (kk == 0))
            def _():
                _cargo_start(cargo_in, cargo_out, sems, scatter)

        def finish(total):
            res = epilogue(total, *[e[...] for e in extra_refs])
            for o_ref, r in zip(out_refs, res):
                o_ref[...] = r.astype(o_ref.dtype)

        part = _dot(a_ref[...].astype(BF16), b_ref[...].astype(BF16), dims)
        if nk == 1:
            finish(part)
        else:
            acc = rest[0]

            @pl.when(kk == 0)
            def _():
                acc[...] = part

            @pl.when((kk > 0) & (kk < nk - 1))
            def _():
                acc[...] += part

            @pl.when(kk == nk - 1)
            def _():
                finish(acc[...] + part)

        if n_cargo:
            @pl.when((i == gi - 1) & (j == gj - 1) & (kk == nk - 1))
            def _():
                _cargo_wait(cargo_in, cargo_out, sems, scatter)

    scratch = [pltpu.VMEM((tm, tn), F32)] if nk > 1 else []
    if n_cargo:
        scratch += _cargo_sems(n_cargo)
    sem = ("arbitrary",) * 3 if n_cargo else ("parallel", "parallel", "arbitrary")
    return pl.pallas_call(
        body,
        name=name,
        grid=(gi, gj, nk),
        in_specs=[a_spec, b_spec] + [tile_spec] * n_extra + [any_spec] * n_cargo,
        out_specs=[tile_spec] * n_out + [any_spec] * n_cargo,
        out_shape=[jax.ShapeDtypeStruct((m, n), d) for d in outs] + _cargo_shapes(cargo),
        scratch_shapes=scratch,
        compiler_params=_params(sem),
    )(a, b, *extras, *[x for x, _ in cargo])


def _rmsnorm_fwd(x, gain, name, rows=256):
    s, d = x.shape
    tr = _tile(s, rows, 8)

    def body(x_ref, g_ref, h_ref):
        xv = x_ref[...]
        r = lax.rsqrt(jnp.mean(xv * xv, axis=-1, keepdims=True) + EPS)
        h_ref[...] = (xv * r * g_ref[...]).astype(h_ref.dtype)

    return pl.pallas_call(
        body, name=name, grid=(s // tr,),
        in_specs=[pl.BlockSpec((tr, d), lambda i: (i, 0)), pl.BlockSpec((1, d), lambda i: (0, 0))],
        out_specs=pl.BlockSpec((tr, d), lambda i: (i, 0)),
        out_shape=jax.ShapeDtypeStruct((s, d), BF16),
        compiler_params=_params(("parallel",)),
    )(x, gain)


def _rmsnorm_bwd(x, gain, dh, dres, name, rows=256):
    s, d = x.shape
    tr = _tile(s, rows, 8)
    has_res = dres is not None

    def body(*refs):
        if has_res:
            x_ref, g_ref, dh_ref, dres_ref, dx_ref, dxb_ref, dg_ref = refs
        else:
            x_ref, g_ref, dh_ref, dx_ref, dxb_ref, dg_ref = refs
        i = pl.program_id(0)
        xv = x_ref[...]
        dy = dh_ref[...].astype(F32)
        r = lax.rsqrt(jnp.mean(xv * xv, axis=-1, keepdims=True) + EPS)
        xn = xv * r
        t = dy * g_ref[...]
        dx = r * (t - xn * jnp.mean(t * xn, axis=-1, keepdims=True))
        if has_res:
            dx = dx + dres_ref[...]
        dx_ref[...] = dx
        dxb_ref[...] = dx.astype(BF16)

        @pl.when(i == 0)
        def _():
            dg_ref[...] = jnp.zeros_like(dg_ref)

        dg_ref[...] += jnp.sum(dy * xn, axis=0, keepdims=True)

    row_spec = pl.BlockSpec((tr, d), lambda i: (i, 0))
    vec_spec = pl.BlockSpec((1, d), lambda i: (0, 0))
    args = [x, gain, dh] + ([dres] if has_res else [])
    return pl.pallas_call(
        body, name=name, grid=(s // tr,),
        in_specs=[row_spec, vec_spec, row_spec] + ([row_spec] if has_res else []),
        out_specs=[row_spec, row_spec, vec_spec],
        out_shape=[jax.ShapeDtypeStruct((s, d), F32), jax.ShapeDtypeStruct((s, d), BF16),
                   jax.ShapeDtypeStruct((1, d), F32)],
        compiler_params=_params(("arbitrary",)),
    )(*args)


def _loss_head(x, gain, target, name, rows=256):
    s, d = x.shape
    tr = _tile(s, rows, 8)

    def body(x_ref, g_ref, t_ref, dx_ref, dxb_ref, dg_ref, loss_ref):
        i = pl.program_id(0)
        xv = x_ref[...]
        g = g_ref[...]
        r = lax.rsqrt(jnp.mean(xv * xv, axis=-1, keepdims=True) + EPS)
        xn = xv * r
        err = xn * g - t_ref[...]
        dy = err * (1.0 / d)
        t = dy * g
        dx = r * (t - xn * jnp.mean(t * xn, axis=-1, keepdims=True))
        dx_ref[...] = dx
        dxb_ref[...] = dx.astype(BF16)

        @pl.when(i == 0)
        def _():
            dg_ref[...] = jnp.zeros_like(dg_ref)
            loss_ref[...] = jnp.zeros_like(loss_ref)

        dg_ref[...] += jnp.sum(dy * xn, axis=0, keepdims=True)
        part = 0.5 * jnp.sum(jnp.mean(err * err, axis=-1, keepdims=True), axis=0, keepdims=True)
        loss_ref[...] += jnp.broadcast_to(part, loss_ref.shape)

    row_spec = pl.BlockSpec((tr, d), lambda i: (i, 0))
    vec_spec = pl.BlockSpec((1, d), lambda i: (0, 0))
    return pl.pallas_call(
        body, name=name, grid=(s // tr,),
        in_specs=[row_spec, vec_spec, row_spec],
        out_specs=[row_spec, row_spec, vec_spec, pl.BlockSpec((1, 128), lambda i: (0, 0))],
        out_shape=[jax.ShapeDtypeStruct((s, d), F32), jax.ShapeDtypeStruct((s, d), BF16),
                   jax.ShapeDtypeStruct((1, d), F32), jax.ShapeDtypeStruct((1, 128), F32)],
        compiler_params=_params(("arbitrary",)),
    )(x, gain, target)


def _log_sigmoid(z):
    return jnp.minimum(z, 0.0) - jnp.log1p(jnp.exp(-jnp.abs(z)))


def _gla_tri(reverse):
    row = lax.broadcasted_iota(jnp.int32, (GLA_CHUNK, GLA_CHUNK), 0)
    col = lax.broadcasted_iota(jnp.int32, (GLA_CHUNK, GLA_CHUNK), 1)
    return (col >= row) if reverse else (col <= row)


def _gla_rows(s):
    return _tile(s, 256, GLA_CHUNK)


def _gla_chunk_terms(q, k, g, tri, reverse):
    b = _dot(tri.astype(F32), g, NN, precision=lax.Precision.HIGHEST)
    bl = b[0:1] if reverse else b[GLA_CHUNK - 1:GLA_CHUNK]
    qd = q * jnp.exp(b)
    ki = k * jnp.exp(-b)
    ke = k * jnp.exp(bl - b)
    return b, bl, qd, ki, ke


def _gla_fwd(proj, lr, up_pad, bias, reverse, name):
    s = proj.shape[0]
    ts = _gla_rows(s)
    nblk, cpb = s // ts, ts // GLA_CHUNK
    scale = GLA_DK ** -0.5

    def blk(i):
        return (nblk - 1 - i) if reverse else i

    def body(q_ref, k_ref, v_ref, lr_ref, up_ref, b_ref, o_ref, st_ref, state):
        @pl.when(pl.program_id(1) == 0)
        def _():
            state[...] = jnp.zeros_like(state)

        tri = _gla_tri(reverse)
        z = _dot(lr_ref[...].astype(BF16), up_ref[...], NN) + b_ref[...]
        g_all = _log_sigmoid(z) * (1.0 / GLA_GATE_NORMALIZER)
        order = range(cpb - 1, -1, -1) if reverse else range(cpb)
        for c in order:
            sl = slice(c * GLA_CHUNK, (c + 1) * GLA_CHUNK)
            _, bl, qd, ki, ke = _gla_chunk_terms(q_ref[sl, :] * scale, k_ref[sl, :], g_all[sl, :], tri, reverse)
            qdb = qd.astype(BF16)
            a = jnp.where(tri, _dot(qdb, ki.astype(BF16), NT), 0.0)
            vb = v_ref[sl, :].astype(BF16)
            st = state[...]
            o_ref[sl, :] = _dot(a.astype(BF16), vb, NN) + _dot(qdb, st.astype(BF16), NT)
            st_ref[0, c] = st
            state[...] = st * jnp.exp(bl) + _dot(vb, ke.astype(BF16), TN)

    qk = lambda off: pl.BlockSpec((ts, GLA_DK), lambda h, i: (blk(i), off // GLA_DK + h))
    return pl.pallas_call(
        body, name=name, grid=(HEADS, nblk),
        in_specs=[qk(OFF_GQ), qk(OFF_GK),
                  pl.BlockSpec((ts, GLA_DV), lambda h, i: (blk(i), OFF_GV // GLA_DV + h)),
                  pl.BlockSpec((ts, LR_PAD), lambda h, i: (blk(i), 0)),
                  pl.BlockSpec((LR_PAD, GLA_DK), lambda h, i: (0, h)),
                  pl.BlockSpec((1, GLA_DK), lambda h, i: (0, h))],
        out_specs=[pl.BlockSpec((ts, GLA_DV), lambda h, i: (blk(i), h)),
                   pl.BlockSpec((1, cpb, GLA_DV, GLA_DK), lambda h, i: (h, blk(i), 0, 0))],
        out_shape=[jax.ShapeDtypeStruct((s, HEADS * GLA_DV), F32),
                   jax.ShapeDtypeStruct((HEADS, s // GLA_CHUNK, GLA_DV, GLA_DK), F32)],
        scratch_shapes=[pltpu.VMEM((GLA_DV, GLA_DK), F32)],
        compiler_params=_params(("parallel", "arbitrary")),
    )(proj, proj, proj, lr, up_pad, bias)


def _gla_bwd(proj, lr, up_pad, bias, states, d_o, prev, reverse, name):
    s = proj.shape[0]
    ts = _gla_rows(s)
    nblk, cpb = s // ts, ts // GLA_CHUNK
    scale = GLA_DK ** -0.5
    has_prev = prev is not None

    def blk(i):
        return i if reverse else (nblk - 1 - i)

    def body(*refs):
        q_ref, k_ref, v_ref, lr_ref, up_ref, b_ref, st_ref, do_ref = refs[:8]
        refs = refs[8:]
        if has_prev:
            pq_ref, pk_ref, pv_ref = refs[:3]
            refs = refs[3:]
        dq_ref, dk_ref, dv_ref, dz_ref, dstate = refs

        @pl.when(pl.program_id(1) == 0)
        def _():
            dstate[...] = jnp.zeros_like(dstate)

        tri = _gla_tri(reverse)
        tri_t = _gla_tri(not reverse)
        row = lax.broadcasted_iota(jnp.int32, (GLA_CHUNK, GLA_DK), 0)
        last_row = (row == 0) if reverse else (row == GLA_CHUNK - 1)
        z = _dot(lr_ref[...].astype(BF16), up_ref[...], NN) + b_ref[...]
        g_all = _log_sigmoid(z) * (1.0 / GLA_GATE_NORMALIZER)
        dgate = (1.0 / GLA_GATE_NORMALIZER) * (1.0 - jax.nn.sigmoid(z))
        order = range(cpb) if reverse else range(cpb - 1, -1, -1)
        for c in order:
            sl = slice(c * GLA_CHUNK, (c + 1) * GLA_CHUNK)
            b, bl, qd, ki, ke = _gla_chunk_terms(q_ref[sl, :] * scale, k_ref[sl, :], g_all[sl, :], tri, reverse)
            qdb, kib, keb = qd.astype(BF16), ki.astype(BF16), ke.astype(BF16)
            a = jnp.where(tri, _dot(qdb, kib, NT), 0.0)
            vb = v_ref[sl, :].astype(BF16)
            dob = do_ref[sl, :].astype(BF16)
            st = st_ref[0, c]
            dst = dstate[...]
            dstb = dst.astype(BF16)
            da = jnp.where(tri, _dot(dob, vb, NT), 0.0).astype(BF16)
            dv = _dot(a.astype(BF16), dob, TN) + _dot(keb, dstb, NT)
            dqd = _dot(da, kib, NN) + _dot(dob, st.astype(BF16), NN)
            dki = _dot(da, qdb, TN)
            dke = _dot(vb, dstb, NN)
            decay = jnp.exp(bl)
            dbl = decay * jnp.sum(dst * st, axis=0, keepdims=True) + jnp.sum(dke * ke, axis=0, keepdims=True)
            dstate[...] = dst * decay + _dot(dob, qdb, TN)
            db = dqd * qd - dki * ki - dke * ke + jnp.where(last_row, dbl, 0.0)
            dg = _dot(tri_t.astype(F32), db, NN, precision=lax.Precision.HIGHEST)
            dq = dqd * jnp.exp(b) * scale
            dk = dki * jnp.exp(-b) + dke * jnp.exp(bl - b)
            if has_prev:
                dq = dq + pq_ref[sl, :]
                dk = dk + pk_ref[sl, :]
                dv = dv + pv_ref[sl, :]
            dq_ref[sl, :] = dq
            dk_ref[sl, :] = dk
            dv_ref[sl, :] = dv
            dz_ref[sl, :] = dg * dgate[sl, :]

    qk = lambda off: pl.BlockSpec((ts, GLA_DK), lambda h, i: (blk(i), off // GLA_DK + h))
    hk = pl.BlockSpec((ts, GLA_DK), lambda h, i: (blk(i), h))
    hv = pl.BlockSpec((ts, GLA_DV), lambda h, i: (blk(i), h))
    in_specs = [qk(OFF_GQ), qk(OFF_GK),
                pl.BlockSpec((ts, GLA_DV), lambda h, i: (blk(i), OFF_GV // GLA_DV + h)),
                pl.BlockSpec((ts, LR_PAD), lambda h, i: (blk(i), 0)),
                pl.BlockSpec((LR_PAD, GLA_DK), lambda h, i: (0, h)),
                pl.BlockSpec((1, GLA_DK), lambda h, i: (0, h)),
                pl.BlockSpec((1, cpb, GLA_DV, GLA_DK), lambda h, i: (h, blk(i), 0, 0)),
                hv]
    args = [proj, proj, proj, lr, up_pad, bias, states, d_o]
    if has_prev:
        in_specs += [hk, hk, hv]
        args += list(prev)
    return pl.pallas_call(
        body, name=name, grid=(HEADS, nblk),
        in_specs=in_specs,
        out_specs=[hk, hk, hv, hk],
        out_shape=[jax.ShapeDtypeStruct((s, HEADS * GLA_DK), F32), jax.ShapeDtypeStruct((s, HEADS * GLA_DK), F32),
                   jax.ShapeDtypeStruct((s, HEADS * GLA_DV), F32), jax.ShapeDtypeStruct((s, HEADS * GLA_DK), F32)],
        scratch_shapes=[pltpu.VMEM((GLA_DV, GLA_DK), F32)],
        compiler_params=_params(("parallel", "arbitrary")),
    )(*args)


def _dil_tile(l):
    return _tile(l, 256, DIL_HALF)


def t5_bucket(rel):
    half = REL_BUCKETS // 2
    max_exact = half // 2
    ret = jnp.where(rel > 0, half, 0)
    n = jnp.abs(rel)
    nf = jnp.maximum(n, 1).astype(jnp.float32)
    large = max_exact + (jnp.log(nf / max_exact) / math.log(REL_MAX_DISTANCE / max_exact)
                         * (half - max_exact)).astype(jnp.int32)
    large = jnp.minimum(large, half - 1)
    return ret + jnp.where(n < max_exact, n, large)


def _band_buckets(dilation):
    w = DIL_HALF
    rel_sub = jnp.arange(3 * w)[None, :] - w - jnp.arange(w)[:, None]
    return t5_bucket(rel_sub * dilation)[0, :2 * w + 1].astype(jnp.int32)


def _band_offsets(tq):
    tk = tq + 2 * DIL_HALF
    da = lax.broadcasted_iota(jnp.int32, (tq, tk), 1) - lax.broadcasted_iota(jnp.int32, (tq, tk), 0)
    db = (lax.broadcasted_iota(jnp.int32, (tk, tq), 1) - lax.broadcasted_iota(jnp.int32, (tk, tq), 0)
          + 2 * DIL_HALF)
    return jnp.clip(da, 0, 2 * DIL_HALF), jnp.clip(db, 0, 2 * DIL_HALF)


def _bias_tiles(band, table, tq, name):
    tk = tq + 2 * DIL_HALF

    def body(band_ref, t_ref, oa_ref, ob_ref):
        h = pl.program_id(0)
        off_a, off_b = _band_offsets(tq)

        def step(t, carry):
            bkt_a, bkt_b = carry
            bkt = band_ref[t]
            return jnp.where(off_a == t, bkt, bkt_a), jnp.where(off_b == t, bkt, bkt_b)

        bkt_a, bkt_b = lax.fori_loop(0, 2 * DIL_HALF + 1, step,
                                     (jnp.zeros((tq, tk), jnp.int32), jnp.zeros((tk, tq), jnp.int32)))
        acc_a, acc_b = jnp.zeros((tq, tk), F32), jnp.zeros((tk, tq), F32)
        for bkt in range(REL_BUCKETS):
            val = t_ref[bkt, h]
            acc_a = jnp.where(bkt_a == bkt, val, acc_a)
            acc_b = jnp.where(bkt_b == bkt, val, acc_b)
        oa_ref[0] = acc_a
        ob_ref[0] = acc_b

    smem = pl.BlockSpec(memory_space=pltpu.SMEM)
    return pl.pallas_call(
        body, name=name, grid=(HEADS,),
        in_specs=[smem, smem],
        out_specs=[pl.BlockSpec((1, tq, tk), lambda h: (h, 0, 0)), pl.BlockSpec((1, tk, tq), lambda h: (h, 0, 0))],
        out_shape=[jax.ShapeDtypeStruct((HEADS, tq, tk), F32), jax.ShapeDtypeStruct((HEADS, tk, tq), F32)],
        compiler_params=_params(("arbitrary",)),
    )(band, table)


def _bias_grad(bands, dbias_list, name):
    n = len(bands)

    def body(*refs):
        band_refs, db_refs, out_ref = refs[:n], refs[n:2 * n], refs[2 * n]
        row = lax.broadcasted_iota(jnp.int32, (REL_BUCKETS, 128), 0)
        lane = lax.broadcasted_iota(jnp.int32, (REL_BUCKETS, 128), 1)
        acc = jnp.zeros((REL_BUCKETS, 128), F32)
        for band_ref, d_ref in zip(band_refs, db_refs):
            off_a, _ = _band_offsets(d_ref.shape[1])
            for h in range(HEADS):
                def step(t, acc, band_ref=band_ref, d_ref=d_ref, h=h, off_a=off_a):
                    tot = jnp.sum(jnp.where(off_a == t, d_ref[h], 0.0))
                    return acc + jnp.where((row == band_ref[t]) & (lane == h), tot, 0.0)

                acc = lax.fori_loop(0, 2 * DIL_HALF + 1, step, acc)
        out_ref[...] = acc

    vm = pl.BlockSpec(memory_space=pltpu.VMEM)
    smem = pl.BlockSpec(memory_space=pltpu.SMEM)
    return pl.pallas_call(
        body, name=name,
        in_specs=[smem] * n + [vm] * n, out_specs=vm,
        out_shape=jax.ShapeDtypeStruct((REL_BUCKETS, 128), F32),
        compiler_params=pltpu.CompilerParams(vmem_limit_bytes=VMEM_LIMIT),
    )(*bands, *dbias_list)


def _dil_specs(l, tq, dilation, width, off, by_head_first):
    nb64 = l // DIL_HALF
    per = tq // DIL_HALF

    def col(h, r):
        return (r * width + off) // HEAD_DIM + h

    def wrap(f):
        if by_head_first:
            return lambda h, r, n: f(h, r, n)
        return lambda r, h, n: f(h, r, n)

    prev = pl.BlockSpec((DIL_HALF, HEAD_DIM), wrap(lambda h, r, n: (jnp.maximum(n * per - 1, 0), col(h, r))))
    main = pl.BlockSpec((tq, HEAD_DIM), wrap(lambda h, r, n: (n, col(h, r))))
    nxt = pl.BlockSpec((DIL_HALF, HEAD_DIM), wrap(lambda h, r, n: (jnp.minimum((n + 1) * per, nb64 - 1), col(h, r))))
    return prev, main, nxt


def _dil_fwd(proj, bias_a, dilation, name):
    s = proj.shape[0]
    l = s // dilation
    tq = _dil_tile(l)
    tk = tq + 2 * DIL_HALF
    nq = l // tq
    scale = HEAD_DIM ** -0.5
    view = proj.reshape(l, dilation * MAIN_WIDTH)

    def body(q_ref, kp_ref, km_ref, kn_ref, vp_ref, vm_ref, vn_ref, b_ref, o_ref, lse_ref):
        n = pl.program_id(2)
        q = q_ref[...].astype(BF16)
        kc = jnp.concatenate([kp_ref[...], km_ref[...], kn_ref[...]], axis=0).astype(BF16)
        vc = jnp.concatenate([vp_ref[...], vm_ref[...], vn_ref[...]], axis=0).astype(BF16)
        sc = _dot(q, kc, NT) * scale + b_ref[0]
        qpos = n * tq + lax.broadcasted_iota(jnp.int32, (tq, tk), 0)
        kpos = n * tq - DIL_HALF + lax.broadcasted_iota(jnp.int32, (tq, tk), 1)
        mask = (jnp.abs(kpos - qpos) <= DIL_HALF) & (kpos >= 0) & (kpos < l)
        sc = jnp.where(mask, sc, NEG_INF)
        m = jnp.max(sc, axis=-1, keepdims=True)
        p = jnp.exp(sc - m)
        den = jnp.sum(p, axis=-1, keepdims=True)
        o_ref[...] = _dot(p.astype(BF16), vc, NN) / den
        lse_ref[...] = jnp.broadcast_to(m + jnp.log(den), (tq, HEAD_DIM))

    _, q_main, _ = _dil_specs(l, tq, dilation, MAIN_WIDTH, OFF_DQ, False)
    k_specs = _dil_specs(l, tq, dilation, MAIN_WIDTH, OFF_DK, False)
    v_specs = _dil_specs(l, tq, dilation, MAIN_WIDTH, OFF_DV, False)
    _, o_main, _ = _dil_specs(l, tq, dilation, HEADS * HEAD_DIM, 0, False)
    o, lse = pl.pallas_call(
        body, name=name, grid=(dilation, HEADS, nq),
        in_specs=[q_main, *k_specs, *v_specs, pl.BlockSpec((1, tq, tk), lambda r, h, n: (h, 0, 0))],
        out_specs=[o_main, o_main],
        out_shape=[jax.ShapeDtypeStruct((l, dilation * HEADS * HEAD_DIM), F32)] * 2,
        compiler_params=_params(("parallel", "parallel", "parallel")),
    )(view, view, view, view, view, view, view, bias_a)
    return o.reshape(s, HEADS * HEAD_DIM), lse.reshape(s, HEADS * HEAD_DIM)


def _dil_bwd_q(proj, d_o, lse, delta, bias_a, prev_dq, dilation, name):
    s = proj.shape[0]
    l = s // dilation
    tq = _dil_tile(l)
    tk = tq + 2 * DIL_HALF
    nq = l // tq
    scale = HEAD_DIM ** -0.5
    w4 = HEADS * HEAD_DIM
    view = proj.reshape(l, dilation * MAIN_WIDTH)
    small = lambda t: t.reshape(l, dilation * w4)
    has_prev = prev_dq is not None

    def body(*refs):
        q_ref, kp_ref, km_ref, kn_ref, vp_ref, vm_ref, vn_ref, b_ref, do_ref, lse_ref, dl_ref = refs[:11]
        refs = refs[11:]
        if has_prev:
            pq_ref, refs = refs[0], refs[1:]
        dq_ref, db_ref = refs
        r, n = pl.program_id(1), pl.program_id(2)
        q = q_ref[...].astype(BF16)
        kc = jnp.concatenate([kp_ref[...], km_ref[...], kn_ref[...]], axis=0).astype(BF16)
        vc = jnp.concatenate([vp_ref[...], vm_ref[...], vn_ref[...]], axis=0).astype(BF16)
        sc = _dot(q, kc, NT) * scale + b_ref[0]
        qpos = n * tq + lax.broadcasted_iota(jnp.int32, (tq, tk), 0)
        kpos = n * tq - DIL_HALF + lax.broadcasted_iota(jnp.int32, (tq, tk), 1)
        mask = (jnp.abs(kpos - qpos) <= DIL_HALF) & (kpos >= 0) & (kpos < l)
        p = jnp.where(mask, jnp.exp(sc - lse_ref[:, 0:1]), 0.0)
        dp = _dot(do_ref[...].astype(BF16), vc, NT)
        ds = p * (dp - dl_ref[:, 0:1])
        dq = _dot(ds.astype(BF16), kc, NN) * scale
        if has_prev:
            dq = dq + pq_ref[...]
        dq_ref[...] = dq

        @pl.when((r == 0) & (n == 0))
        def _():
            db_ref[...] = jnp.zeros_like(db_ref)

        db_ref[0] += ds

    _, q_main, _ = _dil_specs(l, tq, dilation, MAIN_WIDTH, OFF_DQ, True)
    k_specs = _dil_specs(l, tq, dilation, MAIN_WIDTH, OFF_DK, True)
    v_specs = _dil_specs(l, tq, dilation, MAIN_WIDTH, OFF_DV, True)
    _, o_main, _ = _dil_specs(l, tq, dilation, w4, 0, True)
    bias_spec = pl.BlockSpec((1, tq, tk), lambda h, r, n: (h, 0, 0))
    in_specs = [q_main, *k_specs, *v_specs, bias_spec, o_main, o_main, o_main] + ([o_main] if has_prev else [])
    args = [view] * 7 + [bias_a, small(d_o), small(lse), small(delta)] + ([small(prev_dq)] if has_prev else [])
    dq, dbias = pl.pallas_call(
        body, name=name, grid=(HEADS, dilation, nq),
        in_specs=in_specs,
        out_specs=[o_main, bias_spec],
        out_shape=[jax.ShapeDtypeStruct((l, dilation * w4), F32), jax.ShapeDtypeStruct((HEADS, tq, tk), F32)],
        compiler_params=_params(("arbitrary", "arbitrary", "arbitrary")),
    )(*args)
    return dq.reshape(s, w4), dbias


def _dil_bwd_kv(proj, d_o, lse, delta, bias_b, prev, dilation, name):
    s = proj.shape[0]
    l = s // dilation
    tq = _dil_tile(l)
    tw = tq + 2 * DIL_HALF
    nq = l // tq
    scale = HEAD_DIM ** -0.5
    w4 = HEADS * HEAD_DIM
    view = proj.reshape(l, dilation * MAIN_WIDTH)
    small = lambda t: t.reshape(l, dilation * w4)
    has_prev = prev is not None

    def body(*refs):
        (qp_ref, qm_ref, qn_ref, k_ref, v_ref, b_ref, dop_ref, dom_ref, don_ref,
         lp_ref, lm_ref, ln_ref, dp_ref, dm_ref, dn_ref) = refs[:15]
        refs = refs[15:]
        if has_prev:
            pk_ref, pv_ref = refs[:2]
            refs = refs[2:]
        dk_ref, dv_ref = refs
        n = pl.program_id(2)
        cat = lambda a, b_, c: jnp.concatenate([a[...], b_[...], c[...]], axis=0)
        qc = cat(qp_ref, qm_ref, qn_ref).astype(BF16)
        doc = cat(dop_ref, dom_ref, don_ref).astype(BF16)
        lsec = cat(lp_ref, lm_ref, ln_ref)[:, 0:1]
        dlc = cat(dp_ref, dm_ref, dn_ref)[:, 0:1]
        kb = k_ref[...].astype(BF16)
        vb = v_ref[...].astype(BF16)
        sc = _dot(qc, kb, NT) * scale + b_ref[0]
        qpos = n * tq - DIL_HALF + lax.broadcasted_iota(jnp.int32, (tw, tq), 0)
        kpos = n * tq + lax.broadcasted_iota(jnp.int32, (tw, tq), 1)
        mask = (jnp.abs(kpos - qpos) <= DIL_HALF) & (qpos >= 0) & (qpos < l)
        p = jnp.where(mask, jnp.exp(sc - lsec), 0.0)
        dv = _dot(p.astype(BF16), doc, TN)
        dp = _dot(doc, vb, NT)
        ds = p * (dp - dlc)
        dk = _dot(ds.astype(BF16), qc, TN) * scale
        if has_prev:
            dk = dk + pk_ref[...]
            dv = dv + pv_ref[...]
        dk_ref[...] = dk
        dv_ref[...] = dv

    q_specs = _dil_specs(l, tq, dilation, MAIN_WIDTH, OFF_DQ, False)
    _, k_main, _ = _dil_specs(l, tq, dilation, MAIN_WIDTH, OFF_DK, False)
    _, v_main, _ = _dil_specs(l, tq, dilation, MAIN_WIDTH, OFF_DV, False)
    o_specs = _dil_specs(l, tq, dilation, w4, 0, False)
    o_main = o_specs[1]
    in_specs = [*q_specs, k_main, v_main, pl.BlockSpec((1, tw, tq), lambda r, h, n: (h, 0, 0)),
                *o_specs, *o_specs, *o_specs] + ([o_main, o_main] if has_prev else [])
    args = ([view] * 5 + [bias_b] + [small(d_o)] * 3 + [small(lse)] * 3 + [small(delta)] * 3
            + ([small(prev[0]), small(prev[1])] if has_prev else []))
    dk, dv = pl.pallas_call(
        body, name=name, grid=(dilation, HEADS, nq),
        in_specs=in_specs,
        out_specs=[o_main, o_main],
        out_shape=[jax.ShapeDtypeStruct((l, dilation * w4), F32)] * 2,
        compiler_params=_params(("parallel", "parallel", "parallel")),
    )(*args)
    return dk.reshape(s, w4), dv.reshape(s, w4)


def _pcall(body, name, grid, in_specs, out_specs, out_shape, scratch_shapes, sem, args, cargo=()):
    n_in, n_out, n_c = len(in_specs), len(out_specs), len(cargo)
    if not n_c:
        return pl.pallas_call(body, name=name, grid=grid, in_specs=in_specs, out_specs=out_specs, out_shape=out_shape,
                              scratch_shapes=scratch_shapes, compiler_params=_params(sem))(*args)
    scatter = [sc for _, sc in cargo]
    n_scr = len(scratch_shapes)

    def wrapped(*refs):
        ins, refs = refs[:n_in], refs[n_in:]
        c_in, refs = refs[:n_c], refs[n_c:]
        outs, refs = refs[:n_out], refs[n_out:]
        c_out, refs = refs[:n_c], refs[n_c:]
        scr, sems = refs[:n_scr], refs[n_scr:]
        ids = [pl.program_id(a) for a in range(len(grid))]
        first = functools.reduce(lambda p, q: p & q, [i == 0 for i in ids])
        last = functools.reduce(lambda p, q: p & q, [i == g - 1 for i, g in zip(ids, grid)])

        @pl.when(first)
        def _():
            _cargo_start(c_in, c_out, sems, scatter)

        body(*ins, *outs, *scr)

        @pl.when(last)
        def _():
            _cargo_wait(c_in, c_out, sems, scatter)

    any_spec = pl.BlockSpec(memory_space=pl.ANY)
    return pl.pallas_call(
        wrapped, name=name, grid=grid,
        in_specs=list(in_specs) + [any_spec] * n_c, out_specs=list(out_specs) + [any_spec] * n_c,
        out_shape=list(out_shape) + _cargo_shapes(cargo),
        scratch_shapes=list(scratch_shapes) + _cargo_sems(n_c),
        compiler_params=_params(("arbitrary",) * len(grid)),
    )(*args, *[x for x, _ in cargo])


def _tri_dot(tri, x):
    t = jnp.where(tri, 1.0, 0.0).astype(BF16)
    hi = x.astype(BF16)
    rest = x - hi.astype(F32)
    mid = rest.astype(BF16)
    lo = (rest - mid.astype(F32)).astype(BF16)
    return _dot(t, hi, NN) + _dot(t, mid, NN) + _dot(t, lo, NN)


def _gla4_fwd(proj, lr, up_pad, bias, reverse, name, cargo=()):
    s = proj.shape[0]
    ts = _gla_rows(s)
    nblk, cpb = s // ts, ts // GLA_CHUNK
    scale = GLA_DK ** -0.5
    wk, wv = HEADS * GLA_DK, HEADS * GLA_DV

    def blk(i):
        return (nblk - 1 - i) if reverse else i

    def body(q_ref, k_ref, v_ref, lr_ref, up_ref, b_ref, o_ref, st_ref, state):
        @pl.when(pl.program_id(0) == 0)
        def _():
            state[...] = jnp.zeros_like(state)

        tri = _gla_tri(reverse)
        z = _dot(lr_ref[...].astype(BF16), up_ref[...], NN) + b_ref[...]
        g_all = _log_sigmoid(z) * (1.0 / GLA_GATE_NORMALIZER)
        order = range(cpb - 1, -1, -1) if reverse else range(cpb)
        for c in order:
            sl = slice(c * GLA_CHUNK, (c + 1) * GLA_CHUNK)
            b_all = _tri_dot(tri, g_all[sl, :])
            for h in range(HEADS):
                hk = slice(h * GLA_DK, (h + 1) * GLA_DK)
                hv = slice(h * GLA_DV, (h + 1) * GLA_DV)
                b = b_all[:, hk]
                bl = b[0:1] if reverse else b[GLA_CHUNK - 1:GLA_CHUNK]
                kc = k_ref[sl, hk]
                qdb = (q_ref[sl, hk] * scale * jnp.exp(b)).astype(BF16)
                ki = kc * jnp.exp(-b)
                ke = kc * jnp.exp(bl - b)
                a = jnp.where(tri, _dot(qdb, ki.astype(BF16), NT), 0.0)
                vb = v_ref[sl, hv].astype(BF16)
                st = state[h]
                o_ref[sl, hv] = _dot(a.astype(BF16), vb, NN) + _dot(qdb, st.astype(BF16), NT)
                st_ref[h, c] = st
                state[h] = st * jnp.exp(bl) + _dot(vb, ke.astype(BF16), TN)

    return _pcall(
        body, name, (nblk,),
        [pl.BlockSpec((ts, wk), lambda i: (blk(i), OFF_GQ // wk)), pl.BlockSpec((ts, wk), lambda i: (blk(i), OFF_GK // wk)),
         pl.BlockSpec((ts, wv), lambda i: (blk(i), OFF_GV // wv)), pl.BlockSpec((ts, LR_PAD), lambda i: (blk(i), 0)),
         pl.BlockSpec((LR_PAD, wk), lambda i: (0, 0)), pl.BlockSpec((1, wk), lambda i: (0, 0))],
        [pl.BlockSpec((ts, wv), lambda i: (blk(i), 0)),
         pl.BlockSpec((HEADS, cpb, GLA_DV, GLA_DK), lambda i: (0, blk(i), 0, 0))],
        [jax.ShapeDtypeStruct((s, wv), F32), jax.ShapeDtypeStruct((HEADS, s // GLA_CHUNK, GLA_DV, GLA_DK), F32)],
        [pltpu.VMEM((HEADS, GLA_DV, GLA_DK), F32)], ("arbitrary",),
        (proj, proj, proj, lr, up_pad, bias), cargo)


def _gla4_bwd(proj, lr, up_pad, bias, states, d_o, prev, reverse, name, cargo=()):
    s = proj.shape[0]
    ts = _gla_rows(s)
    nblk, cpb = s // ts, ts // GLA_CHUNK
    scale = GLA_DK ** -0.5
    wk, wv = HEADS * GLA_DK, HEADS * GLA_DV
    has_prev = prev is not None

    def blk(i):
        return i if reverse else (nblk - 1 - i)

    def body(*refs):
        q_ref, k_ref, v_ref, lr_ref, up_ref, b_ref, st_ref, do_ref = refs[:8]
        refs = refs[8:]
        if has_prev:
            pq_ref, pk_ref, pv_ref = refs[:3]
            refs = refs[3:]
        dq_ref, dk_ref, dv_ref, dz_ref, dstate = refs

        @pl.when(pl.program_id(0) == 0)
        def _():
            dstate[...] = jnp.zeros_like(dstate)

        tri = _gla_tri(reverse)
        tri_t = _gla_tri(not reverse)
        row = lax.broadcasted_iota(jnp.int32, (GLA_CHUNK, GLA_DK), 0)
        last_row = (row == 0) if reverse else (row == GLA_CHUNK - 1)
        z = _dot(lr_ref[...].astype(BF16), up_ref[...], NN) + b_ref[...]
        g_all = _log_sigmoid(z) * (1.0 / GLA_GATE_NORMALIZER)
        dgate = (1.0 / GLA_GATE_NORMALIZER) * (1.0 - jax.nn.sigmoid(z))
        order = range(cpb) if reverse else range(cpb - 1, -1, -1)
        for c in order:
            sl = slice(c * GLA_CHUNK, (c + 1) * GLA_CHUNK)
            b_all = _tri_dot(tri, g_all[sl, :])
            for h in range(HEADS):
                hk = slice(h * GLA_DK, (h + 1) * GLA_DK)
                hv = slice(h * GLA_DV, (h + 1) * GLA_DV)
                b = b_all[:, hk]
                bl = b[0:1] if reverse else b[GLA_CHUNK - 1:GLA_CHUNK]
                eb = jnp.exp(b)
                kc = k_ref[sl, hk]
                qd = q_ref[sl, hk] * scale * eb
                ki = kc * jnp.exp(-b)
                ke = kc * jnp.exp(bl - b)
                qdb, kib, keb = qd.astype(BF16), ki.astype(BF16), ke.astype(BF16)
                a = jnp.where(tri, _dot(qdb, kib, NT), 0.0)
                vb = v_ref[sl, hv].astype(BF16)
                dob = do_ref[sl, hv].astype(BF16)
                st = st_ref[h, c]
                dst = dstate[h]
                dstb = dst.astype(BF16)
                da = jnp.where(tri, _dot(dob, vb, NT), 0.0).astype(BF16)
                dv = _dot(a.astype(BF16), dob, TN) + _dot(keb, dstb, NT)
                dqd = _dot(da, kib, NN) + _dot(dob, st.astype(BF16), NN)
                dki = _dot(da, qdb, TN)
                dke = _dot(vb, dstb, NN)
                decay = jnp.exp(bl)
                dbl = decay * jnp.sum(dst * st, axis=0, keepdims=True) + jnp.sum(dke * ke, axis=0, keepdims=True)
                dstate[h] = dst * decay + _dot(dob, qdb, TN)
                db = dqd * qd - dki * ki - dke * ke + jnp.where(last_row, dbl, 0.0)
                dg = _tri_dot(tri_t, db)
                dq = dqd * eb * scale
                dk = dki * jnp.exp(-b) + dke * jnp.exp(bl - b)
                if has_prev:
                    dq = dq + pq_ref[sl, hk]
                    dk = dk + pk_ref[sl, hk]
                    dv = dv + pv_ref[sl, hv]
                dq_ref[sl, hk] = dq
                dk_ref[sl, hk] = dk
                dv_ref[sl, hv] = dv
                dz_ref[sl, hk] = dg * dgate[sl, hk]

    rk = pl.BlockSpec((ts, wk), lambda i: (blk(i), 0))
    rv = pl.BlockSpec((ts, wv), lambda i: (blk(i), 0))
    in_specs = [pl.BlockSpec((ts, wk), lambda i: (blk(i), OFF_GQ // wk)),
                pl.BlockSpec((ts, wk), lambda i: (blk(i), OFF_GK // wk)),
                pl.BlockSpec((ts, wv), lambda i: (blk(i), OFF_GV // wv)),
                pl.BlockSpec((ts, LR_PAD), lambda i: (blk(i), 0)),
                pl.BlockSpec((LR_PAD, wk), lambda i: (0, 0)), pl.BlockSpec((1, wk), lambda i: (0, 0)),
                pl.BlockSpec((HEADS, cpb, GLA_DV, GLA_DK), lambda i: (0, blk(i), 0, 0)), rv]
    args = [proj, proj, proj, lr, up_pad, bias, states, d_o]
    if has_prev:
        in_specs += [rk, rk, rv]
        args += list(prev)
    return _pcall(
        body, name, (nblk,), in_specs, [rk, rk, rv, rk],
        [jax.ShapeDtypeStruct((s, wk), F32), jax.ShapeDtypeStruct((s, wk), F32),
         jax.ShapeDtypeStruct((s, wv), F32), jax.ShapeDtypeStruct((s, wk), F32)],
        [pltpu.VMEM((HEADS, GLA_DV, GLA_DK), F32)], ("arbitrary",), args, cargo)


DILATIONS = tuple(d for _, d in DIL_CONFIGS)
DIL_HALO = DIL_HALF * max(DILATIONS)
DIL_UNROLL = 16


def _dilf_block(s):
    return min(s, DIL_HALO)


def _dilf_tq(block, dilation):
    return min(128, block // dilation)


def _dilf_specs(s, block, off, width):
    nb = s // block
    col = lambda h: off // HEAD_DIM + h
    prev = pl.BlockSpec((block, HEAD_DIM), lambda h, n: (jnp.maximum(n - 1, 0), col(h)))
    main = pl.BlockSpec((block, HEAD_DIM), lambda h, n: (n, col(h)))
    nxt = pl.BlockSpec((block, HEAD_DIM), lambda h, n: (jnp.minimum(n + 1, nb - 1), col(h)))
    return prev, main, nxt


def _dilf_rows(start, count, dilation):
    if dilation == 1:
        return pl.ds(pl.multiple_of(start, 8), count)
    return pl.ds(start, count, stride=dilation)


def _dilf_fwd(proj, biases, name, cargo=()):
    s = proj.shape[0]
    blk = _dilf_block(s)
    assert s % blk == 0 and blk == DIL_HALO, s
    halo = blk
    scale = HEAD_DIM ** -0.5
    w4 = HEADS * HEAD_DIM
    nbr = len(DILATIONS)

    def body(q_ref, kp_ref, km_ref, kn_ref, vp_ref, vm_ref, vn_ref, *rest):
        b_refs, (o_ref, lse_ref, kw, vw, o_scr, l_scr) = rest[:nbr], rest[nbr:]
        p0 = pl.program_id(1) * blk
        for w_ref, parts in ((kw, (kp_ref, km_ref, kn_ref)), (vw, (vp_ref, vm_ref, vn_ref))):
            w_ref[0:halo, :] = parts[0][...]
            w_ref[halo:halo + blk, :] = parts[1][...]
            w_ref[halo + blk:, :] = parts[2][...]
        for bi, d in enumerate(DILATIONS):
            tq = _dilf_tq(blk, d)
            tk = tq + 2 * DIL_HALF
            ii = lax.broadcasted_iota(jnp.int32, (tq, tk), 0)
            jj = lax.broadcasted_iota(jnp.int32, (tq, tk), 1)
            band = jnp.abs(jj - DIL_HALF - ii) <= DIL_HALF
            bias = b_refs[bi][0]

            def tile(i, carry, d=d, tq=tq, tk=tk, band=band, bias=bias, jj=jj, bi=bi):
                start = (i % d) + d * tq * (i // d)
                wstart = halo - DIL_HALF * d + start
                q = q_ref[_dilf_rows(start, tq, d), :].astype(BF16)
                k = kw[_dilf_rows(wstart, tk, d), :].astype(BF16)
                v = vw[_dilf_rows(wstart, tk, d), :].astype(BF16)
                kpos = p0 + start + d * (jj - DIL_HALF)
                mask = band & (kpos >= 0) & (kpos < s)
                sc = jnp.where(mask, _dot(q, k, NT) * scale + bias, NEG_INF)
                m = jnp.max(sc, axis=-1, keepdims=True)
                p = jnp.exp(sc - m)
                den = jnp.sum(p, axis=-1, keepdims=True)
                o_scr[bi, _dilf_rows(start, tq, d), :] = _dot(p.astype(BF16), v, NN) / den
                l_scr[bi, _dilf_rows(start, tq, d), :] = jnp.broadcast_to(m + jnp.log(den), (tq, HEAD_DIM))
                return carry

            lax.fori_loop(0, blk // tq, tile, 0, unroll=min(DIL_UNROLL, blk // tq))
        ls = [l_scr[bi] for bi in range(nbr)]
        m = functools.reduce(jnp.maximum, ls)
        es = [jnp.exp(l - m) for l in ls]
        den = functools.reduce(lambda a_, b_: a_ + b_, es)
        num = functools.reduce(lambda a_, b_: a_ + b_, [e * o_scr[bi] for bi, e in enumerate(es)])
        o_ref[...] = num / den
        lse_ref[...] = m + jnp.log(den)

    _, q_main, _ = _dilf_specs(s, blk, OFF_DQ, MAIN_WIDTH)
    k_specs = _dilf_specs(s, blk, OFF_DK, MAIN_WIDTH)
    v_specs = _dilf_specs(s, blk, OFF_DV, MAIN_WIDTH)
    _, o_main, _ = _dilf_specs(s, blk, 0, w4)
    b_specs = [pl.BlockSpec((1,) + b.shape[1:], lambda h, n: (h, 0, 0)) for b in biases]
    return _pcall(
        body, name, (HEADS, s // blk), [q_main, *k_specs, *v_specs, *b_specs], [o_main, o_main],
        [jax.ShapeDtypeStruct((s, w4), F32)] * 2,
        [pltpu.VMEM((blk + 2 * halo, HEAD_DIM), F32)] * 2 + [pltpu.VMEM((nbr, blk, HEAD_DIM), F32)] * 2,
        ("parallel", "parallel"), [proj] * 7 + list(biases), cargo)


def _dilf_bwd_q(proj, d_o, lse, delta, biases, name, cargo=()):
    s = proj.shape[0]
    blk = _dilf_block(s)
    halo = blk
    scale = HEAD_DIM ** -0.5
    w4 = HEADS * HEAD_DIM
    nbr = len(DILATIONS)

    def body(q_ref, kp_ref, km_ref, kn_ref, vp_ref, vm_ref, vn_ref, do_ref, lse_ref, dl_ref, *rest):
        b_refs, rest = rest[:nbr], rest[nbr:]
        dq_ref, db_refs, (kw, vw) = rest[0], rest[1:1 + nbr], rest[1 + nbr:]
        n = pl.program_id(1)
        p0 = n * blk
        for w_ref, parts in ((kw, (kp_ref, km_ref, kn_ref)), (vw, (vp_ref, vm_ref, vn_ref))):
            w_ref[0:halo, :] = parts[0][...]
            w_ref[halo:halo + blk, :] = parts[1][...]
            w_ref[halo + blk:, :] = parts[2][...]
        dq_ref[...] = jnp.zeros_like(dq_ref)
        for bi, d in enumerate(DILATIONS):
            tq = _dilf_tq(blk, d)
            tk = tq + 2 * DIL_HALF
            ii = lax.broadcasted_iota(jnp.int32, (tq, tk), 0)
            jj = lax.broadcasted_iota(jnp.int32, (tq, tk), 1)
            band = jnp.abs(jj - DIL_HALF - ii) <= DIL_HALF
            bias = b_refs[bi][0]
            db_ref = db_refs[bi]

            @pl.when(n == 0)
            def _(db_ref=db_ref):
                db_ref[...] = jnp.zeros_like(db_ref)

            def tile(i, carry, d=d, tq=tq, tk=tk, band=band, bias=bias, jj=jj, db_ref=db_ref):
                start = (i % d) + d * tq * (i // d)
                wstart = halo - DIL_HALF * d + start
                rows = _dilf_rows(start, tq, d)
                q = q_ref[rows, :].astype(BF16)
                k = kw[_dilf_rows(wstart, tk, d), :].astype(BF16)
                v = vw[_dilf_rows(wstart, tk, d), :].astype(BF16)
                kpos = p0 + start + d * (jj - DIL_HALF)
                mask = band & (kpos >= 0) & (kpos < s)
                sc = _dot(q, k, NT) * scale + bias
                p = jnp.where(mask, jnp.exp(sc - lse_ref[rows, :][:, 0:1]), 0.0)
                dp = _dot(do_ref[rows, :].astype(BF16), v, NT)
                ds = p * (dp - dl_ref[rows, :][:, 0:1])
                dq_ref[rows, :] += _dot(ds.astype(BF16), k, NN) * scale
                db_ref[0] += ds
                return carry

            lax.fori_loop(0, blk // tq, tile, 0, unroll=min(DIL_UNROLL, blk // tq))

    _, q_main, _ = _dilf_specs(s, blk, OFF_DQ, MAIN_WIDTH)
    k_specs = _dilf_specs(s, blk, OFF_DK, MAIN_WIDTH)
    v_specs = _dilf_specs(s, blk, OFF_DV, MAIN_WIDTH)
    _, o_main, _ = _dilf_specs(s, blk, 0, w4)
    b_specs = [pl.BlockSpec((1,) + b.shape[1:], lambda h, n: (h, 0, 0)) for b in biases]
    return _pcall(
        body, name, (HEADS, s // blk), [q_main, *k_specs, *v_specs, o_main, o_main, o_main, *b_specs],
        [o_main, *b_specs],
        [jax.ShapeDtypeStruct((s, w4), F32)] + [jax.ShapeDtypeStruct(b.shape, F32) for b in biases],
        [pltpu.VMEM((blk + 2 * halo, HEAD_DIM), F32)] * 2,
        ("arbitrary", "arbitrary"), [proj] * 7 + [d_o, lse, delta] + list(biases), cargo)


def _dilf_bwd_kv(proj, d_o, lse, delta, biases_b, name, cargo=()):
    s = proj.shape[0]
    blk = _dilf_block(s)
    halo = blk
    scale = HEAD_DIM ** -0.5
    w4 = HEADS * HEAD_DIM
    nbr = len(DILATIONS)

    def body(qp_ref, qm_ref, qn_ref, k_ref, v_ref, dop_ref, dom_ref, don_ref, lp_ref, lm_ref, ln_ref,
             dp_ref, dm_ref, dn_ref, *rest):
        b_refs, (dk_ref, dv_ref, qw, dow, lw, dlw) = rest[:nbr], rest[nbr:]
        p0 = pl.program_id(1) * blk
        for w_ref, parts in ((qw, (qp_ref, qm_ref, qn_ref)), (dow, (dop_ref, dom_ref, don_ref)),
                             (lw, (lp_ref, lm_ref, ln_ref)), (dlw, (dp_ref, dm_ref, dn_ref))):
            w_ref[0:halo, :] = parts[0][...]
            w_ref[halo:halo + blk, :] = parts[1][...]
            w_ref[halo + blk:, :] = parts[2][...]
        dk_ref[...] = jnp.zeros_like(dk_ref)
        dv_ref[...] = jnp.zeros_like(dv_ref)
        for bi, d in enumerate(DILATIONS):
            tq = _dilf_tq(blk, d)
            tw = tq + 2 * DIL_HALF
            ii = lax.broadcasted_iota(jnp.int32, (tw, tq), 0)
            jj = lax.broadcasted_iota(jnp.int32, (tw, tq), 1)
            band = jnp.abs(jj + DIL_HALF - ii) <= DIL_HALF
            bias = b_refs[bi][0]

            def tile(i, carry, d=d, tq=tq, tw=tw, band=band, bias=bias, ii=ii):
                start = (i % d) + d * tq * (i // d)
                wstart = halo - DIL_HALF * d + start
                rows = _dilf_rows(start, tq, d)
                wrows = _dilf_rows(wstart, tw, d)
                kb = k_ref[rows, :].astype(BF16)
                vb = v_ref[rows, :].astype(BF16)
                qc = qw[wrows, :].astype(BF16)
                doc = dow[wrows, :].astype(BF16)
                qpos = p0 + start + d * (ii - DIL_HALF)
                mask = band & (qpos >= 0) & (qpos < s)
                sc = _dot(qc, kb, NT) * scale + bias
                p = jnp.where(mask, jnp.exp(sc - lw[wrows, :][:, 0:1]), 0.0)
                dp = _dot(doc, vb, NT)
                ds = p * (dp - dlw[wrows, :][:, 0:1])
                dv_ref[rows, :] += _dot(p.astype(BF16), doc, TN)
                dk_ref[rows, :] += _dot(ds.astype(BF16), qc, TN) * scale
                return carry

            lax.fori_loop(0, blk // tq, tile, 0, unroll=min(DIL_UNROLL, blk // tq))

    q_specs = _dilf_specs(s, blk, OFF_DQ, MAIN_WIDTH)
    _, k_main, _ = _dilf_specs(s, blk, OFF_DK, MAIN_WIDTH)
    _, v_main, _ = _dilf_specs(s, blk, OFF_DV, MAIN_WIDTH)
    o_specs = _dilf_specs(s, blk, 0, w4)
    b_specs = [pl.BlockSpec((1,) + b.shape[1:], lambda h, n: (h, 0, 0)) for b in biases_b]
    return _pcall(
        body, name, (HEADS, s // blk), [*q_specs, k_main, v_main, *o_specs, *o_specs, *o_specs, *b_specs],
        [o_specs[1], o_specs[1]], [jax.ShapeDtypeStruct((s, w4), F32)] * 2,
        [pltpu.VMEM((blk + 2 * halo, HEAD_DIM), F32)] * 4,
        ("parallel", "parallel"), [proj] * 5 + [d_o] * 3 + [lse] * 3 + [delta] * 3 + list(biases_b), cargo)


def _mem_fwd(proj, kv, name):
    s = proj.shape[0]
    mlen = kv.shape[0]
    tq = _tile(s, 512, 8)
    scale = HEAD_DIM ** -0.5
    w4 = HEADS * HEAD_DIM

    def body(q_ref, kv_ref, o_ref, lse_ref):
        for h in range(HEADS):
            hs = slice(h * HEAD_DIM, (h + 1) * HEAD_DIM)
            vs = slice(w4 + h * HEAD_DIM, w4 + (h + 1) * HEAD_DIM)
            sc = _dot(q_ref[:, hs].astype(BF16), kv_ref[:, hs].astype(BF16), NT) * scale
            m = jnp.max(sc, axis=-1, keepdims=True)
            e = jnp.exp(sc - m)
            den = jnp.sum(e, axis=-1, keepdims=True)
            o_ref[:, hs] = _dot((e / den).astype(BF16), kv_ref[:, vs].astype(BF16), NN)
            lse_ref[:, hs] = jnp.broadcast_to(m + jnp.log(den), (tq, HEAD_DIM))

    o_spec = pl.BlockSpec((tq, w4), lambda n: (n, 0))
    return pl.pallas_call(
        body, name=name, grid=(s // tq,),
        in_specs=[pl.BlockSpec((tq, w4), lambda n: (n, OFF_MQ // w4)),
                  pl.BlockSpec((mlen, 2 * w4), lambda n: (0, 0))],
        out_specs=[o_spec, o_spec],
        out_shape=[jax.ShapeDtypeStruct((s, w4), F32)] * 2,
        compiler_params=_params(("parallel",)),
    )(proj, kv)


def _mem_bwd(proj, kv, d_o, lse, delta, name):
    s = proj.shape[0]
    mlen = kv.shape[0]
    tq = _tile(s, 512, 8)
    scale = HEAD_DIM ** -0.5
    w4 = HEADS * HEAD_DIM

    def body(q_ref, kv_ref, do_ref, lse_ref, dl_ref, dq_ref, dk_ref, dv_ref):
        @pl.when(pl.program_id(0) == 0)
        def _():
            dk_ref[...] = jnp.zeros_like(dk_ref)
            dv_ref[...] = jnp.zeros_like(dv_ref)

        for h in range(HEADS):
            hs = slice(h * HEAD_DIM, (h + 1) * HEAD_DIM)
            vs = slice(w4 + h * HEAD_DIM, w4 + (h + 1) * HEAD_DIM)
            qb = q_ref[:, hs].astype(BF16)
            kb = kv_ref[:, hs].astype(BF16)
            dob = do_ref[:, hs].astype(BF16)
            sc = _dot(qb, kb, NT) * scale
            p = jnp.exp(sc - lse_ref[:, hs][:, 0:1])
            dp = _dot(dob, kv_ref[:, vs].astype(BF16), NT)
            ds = (p * (dp - dl_ref[:, hs][:, 0:1])).astype(BF16)
            dq_ref[:, hs] = _dot(ds, kb, NN) * scale
            dk_ref[:, hs] += _dot(ds, qb, TN) * scale
            dv_ref[:, hs] += _dot(p.astype(BF16), dob, TN)

    o_spec = pl.BlockSpec((tq, w4), lambda n: (n, 0))
    acc_spec = pl.BlockSpec((mlen, w4), lambda n: (0, 0))
    dq, dkv, dkv2 = pl.pallas_call(
        body, name=name, grid=(s // tq,),
        in_specs=[pl.BlockSpec((tq, w4), lambda n: (n, OFF_MQ // w4)),
                  pl.BlockSpec((mlen, 2 * w4), lambda n: (0, 0)), o_spec, o_spec, o_spec],
        out_specs=[o_spec, acc_spec, acc_spec],
        out_shape=[jax.ShapeDtypeStruct((s, w4), F32), jax.ShapeDtypeStruct((mlen, w4), F32),
                   jax.ShapeDtypeStruct((mlen, w4), F32)],
        compiler_params=_params(("arbitrary",)),
    )(proj, kv, d_o, lse, delta)
    return dq, dkv, dkv2


def _head_norm(o, gain, width):
    outs, xns = [], []
    for h in range(HEADS):
        oh = o[:, h * width:(h + 1) * width]
        r = lax.rsqrt(jnp.mean(oh * oh, axis=-1, keepdims=True) + EPS)
        xn = oh * r
        xns.append(xn)
        outs.append(xn * gain[:, h * width:(h + 1) * width])
    return outs, xns


def _head_norm_bwd(o, gain, dy, width):
    dos, dgs = [], []
    for h in range(HEADS):
        sl = slice(h * width, (h + 1) * width)
        oh = o[:, sl]
        r = lax.rsqrt(jnp.mean(oh * oh, axis=-1, keepdims=True) + EPS)
        xn = oh * r
        t = dy[:, sl] * gain[:, sl]
        dos.append(r * (t - xn * jnp.mean(t * xn, axis=-1, keepdims=True)))
        dgs.append(jnp.sum(dy[:, sl] * xn, axis=0, keepdims=True))
    return dos, dgs


def _mix_fwd(o_f, o_b, proj, dil_o, mem_o, g_gla, g_dil, g_mem, name):
    s = o_f.shape[0]
    tr = _tile(s, 256, 8)
    w4 = HEADS * HEAD_DIM
    wv = HEADS * GLA_DV

    def body(of_ref, ob_ref, r_ref, od_ref, mo_ref, gg_ref, gd_ref, gm_ref, mix_ref):
        o = of_ref[...] + ob_ref[...]
        normed, _ = _head_norm(o, gg_ref[...], GLA_DV)
        rv = r_ref[...]
        gate = rv * jax.nn.sigmoid(rv)
        for h in range(HEADS):
            mix_ref[:, h * GLA_DV:(h + 1) * GLA_DV] = (normed[h] * gate[:, h * GLA_DV:(h + 1) * GLA_DV]).astype(BF16)
        nd, _ = _head_norm(od_ref[...], gd_ref[...], HEAD_DIM)
        nm, _ = _head_norm(mo_ref[...], gm_ref[...], HEAD_DIM)
        for h in range(HEADS):
            mix_ref[:, wv + h * HEAD_DIM:wv + (h + 1) * HEAD_DIM] = nd[h].astype(BF16)
            mix_ref[:, wv + w4 + h * HEAD_DIM:wv + w4 + (h + 1) * HEAD_DIM] = nm[h].astype(BF16)

    rows = lambda w, c=0: pl.BlockSpec((tr, w), lambda i: (i, c))
    vec = lambda w: pl.BlockSpec((1, w), lambda i: (0, 0))
    return pl.pallas_call(
        body, name=name, grid=(s // tr,),
        in_specs=[rows(wv), rows(wv), rows(wv, OFF_GR // wv), rows(w4), rows(w4), vec(wv), vec(w4), vec(w4)],
        out_specs=rows(wv + 2 * w4),
        out_shape=jax.ShapeDtypeStruct((s, wv + 2 * w4), BF16),
        compiler_params=_params(("parallel",)),
    )(o_f, o_b, proj, dil_o, mem_o, g_gla, g_dil, g_mem)


def _mix_bwd(dmixed, o_f, o_b, proj, dil_o, mem_o, g_gla, g_dil, g_mem, name):
    s = o_f.shape[0]
    tr = _tile(s, 256, 8)
    w4 = HEADS * HEAD_DIM
    wv = HEADS * GLA_DV

    def body(dm_ref, of_ref, ob_ref, r_ref, od_ref, mo_ref, gg_ref, gd_ref, gm_ref,
             dog_ref, dr_ref, dod_ref, dld_ref, dom_ref, dlm_ref, dgg_ref, dgd_ref, dgm_ref):
        i = pl.program_id(0)

        @pl.when(i == 0)
        def _():
            dgg_ref[...] = jnp.zeros_like(dgg_ref)
            dgd_ref[...] = jnp.zeros_like(dgd_ref)
            dgm_ref[...] = jnp.zeros_like(dgm_ref)

        dm = dm_ref[...]
        o = of_ref[...] + ob_ref[...]
        normed, _ = _head_norm(o, gg_ref[...], GLA_DV)
        rv = r_ref[...]
        sg = jax.nn.sigmoid(rv)
        gate = rv * sg
        dgate = sg * (1.0 + rv * (1.0 - sg))
        d_gla = dm[:, :wv]
        for h in range(HEADS):
            sl = slice(h * GLA_DV, (h + 1) * GLA_DV)
            dr_ref[:, sl] = d_gla[:, sl] * normed[h] * dgate[:, sl]
        dos, dgs = _head_norm_bwd(o, gg_ref[...], d_gla * gate, GLA_DV)
        for h in range(HEADS):
            sl = slice(h * GLA_DV, (h + 1) * GLA_DV)
            dog_ref[:, sl] = dos[h]
            dgg_ref[:, sl] += dgs[h]
        for src_ref, g_ref, off, do_out, dl_out, dg_out in (
                (od_ref, gd_ref, wv, dod_ref, dld_ref, dgd_ref),
                (mo_ref, gm_ref, wv + w4, dom_ref, dlm_ref, dgm_ref)):
            src = src_ref[...]
            dos, dgs = _head_norm_bwd(src, g_ref[...], dm[:, off:off + w4], HEAD_DIM)
            for h in range(HEADS):
                sl = slice(h * HEAD_DIM, (h + 1) * HEAD_DIM)
                do_out[:, sl] = dos[h]
                dl_out[:, sl] = jnp.broadcast_to(
                    jnp.sum(dos[h] * src[:, sl], axis=-1, keepdims=True), (tr, HEAD_DIM))
                dg_out[:, sl] += dgs[h]

    rows = lambda w, c=0: pl.BlockSpec((tr, w), lambda i: (i, c))
    vec = lambda w: pl.BlockSpec((1, w), lambda i: (0, 0))
    sds = lambda w: jax.ShapeDtypeStruct((s, w), F32)
    vds = lambda w: jax.ShapeDtypeStruct((1, w), F32)
    return pl.pallas_call(
        body, name=name, grid=(s // tr,),
        in_specs=[rows(wv + 2 * w4), rows(wv), rows(wv), rows(wv, OFF_GR // wv), rows(w4), rows(w4),
                  vec(wv), vec(w4), vec(w4)],
        out_specs=[rows(wv), rows(wv), rows(w4), rows(w4), rows(w4), rows(w4), vec(wv), vec(w4), vec(w4)],
        out_shape=[sds(wv), sds(wv), sds(w4), sds(w4), sds(w4), sds(w4), vds(wv), vds(w4), vds(w4)],
        compiler_params=_params(("arbitrary",)),
    )(dmixed, o_f, o_b, proj, dil_o, mem_o, g_gla, g_dil, g_mem)


def _colsum(x, name, rows=512):
    s, w = x.shape
    tr = _tile(s, rows, 8)

    def body(x_ref, o_ref):
        @pl.when(pl.program_id(0) == 0)
        def _():
            o_ref[...] = jnp.zeros_like(o_ref)

        o_ref[...] += jnp.sum(x_ref[...], axis=0, keepdims=True)

    return pl.pallas_call(
        body, name=name, grid=(s // tr,),
        in_specs=[pl.BlockSpec((tr, w), lambda i: (i, 0))],
        out_specs=pl.BlockSpec((1, w), lambda i: (0, 0)),
        out_shape=jax.ShapeDtypeStruct((1, w), F32),
        compiler_params=_params(("arbitrary",)),
    )(x)


def _peer(k):
    x, y, c = lax.axis_index("x"), lax.axis_index("y"), lax.axis_index("c")
    kx, ky, kc = (k >> 2) & 1, (k >> 1) & 1, k & 1
    return (x ^ kx if kx else x, y ^ ky if ky else y, c ^ kc if kc else c)


def _my_index():
    return 4 * lax.axis_index("x") + 2 * lax.axis_index("y") + lax.axis_index("c")


def _cargo_shapes(cargo):
    return [jax.ShapeDtypeStruct(x.shape if sc else (N_DEV,) + x.shape, x.dtype) for x, sc in cargo]


def _cargo_sems(n):
    return [pltpu.SemaphoreType.DMA((n * (N_DEV - 1),)), pltpu.SemaphoreType.DMA((n * (N_DEV - 1),)),
            pltpu.SemaphoreType.DMA((n,))]


def _cargo_copies(in_refs, out_refs, sems, scatter, with_arrivals=True):
    send_sems, recv_sems, local_sems = sems
    me = _my_index()
    own, sends, arrivals = [], [], []
    for i, (src_ref, dst_ref) in enumerate(zip(in_refs, out_refs)):
        own.append(pltpu.make_async_copy(src_ref.at[me] if scatter[i] else src_ref, dst_ref.at[me], local_sems.at[i]))
        for k in range(1, N_DEV):
            peer = _peer(k)
            peer_idx = 4 * peer[0] + 2 * peer[1] + peer[2]
            src = src_ref.at[peer_idx] if scatter[i] else src_ref
            sem = i * (N_DEV - 1) + k - 1
            sends.append(pltpu.make_async_remote_copy(
                src_ref=src, dst_ref=dst_ref.at[me], send_sem=send_sems.at[sem], recv_sem=recv_sems.at[sem],
                device_id=peer, device_id_type=MESH))
            if with_arrivals:
                arrivals.append(pltpu.make_async_remote_copy(
                    src_ref=src, dst_ref=dst_ref.at[peer_idx], send_sem=send_sems.at[sem], recv_sem=recv_sems.at[sem],
                    device_id=peer, device_id_type=MESH))
    return own, sends, arrivals


def _cargo_start(in_refs, out_refs, sems, scatter):
    own, sends, _ = _cargo_copies(in_refs, out_refs, sems, scatter, with_arrivals=False)
    for cp in own + sends:
        cp.start()


def _cargo_wait(in_refs, out_refs, sems, scatter):
    own, sends, arrivals = _cargo_copies(in_refs, out_refs, sems, scatter)
    for cp in arrivals:
        cp.wait_recv()
    for cp in sends:
        cp.wait_send()
    for cp in own:
        cp.wait()


def _exchange(cargo, name):
    n = len(cargo)
    scatter = [sc for _, sc in cargo]

    def body(*refs):
        in_refs, out_refs, sems = refs[:n], refs[n:2 * n], refs[2 * n:]
        _cargo_start(in_refs, out_refs, sems, scatter)
        _cargo_wait(in_refs, out_refs, sems, scatter)

    any_spec = pl.BlockSpec(memory_space=pl.ANY)
    return pl.pallas_call(
        body, name=name,
        in_specs=[any_spec] * n, out_specs=[any_spec] * n, out_shape=_cargo_shapes(cargo),
        scratch_shapes=_cargo_sems(n),
        compiler_params=pltpu.CompilerParams(has_side_effects=True),
    )(*[x for x, _ in cargo])


def _adamw(parts, w, m, v, name, rows=256):
    r, c = w.shape
    tr = _tile(r, max(8, min(rows, ADAMW_TILE_ELEMS // c)), 8)
    c1 = 1.0 - ADAM_B1 ** ADAM_STEP
    c2 = 1.0 - ADAM_B2 ** ADAM_STEP

    def body(p_ref, w_ref, m_ref, v_ref, g_ref, d_ref, nm_ref, nv_ref):
        g = p_ref[0].astype(F32)
        for d in range(1, N_DEV):
            g = g + p_ref[d].astype(F32)
        nm = ADAM_B1 * m_ref[...] + (1.0 - ADAM_B1) * g
        nv = ADAM_B2 * v_ref[...] + (1.0 - ADAM_B2) * (g * g)
        m_hat = nm / c1
        v_hat = nv / c2
        g_ref[...] = g
        d_ref[...] = -ADAM_LR * (m_hat / (jnp.sqrt(v_hat) + ADAM_EPS) + ADAM_WD * w_ref[...])
        nm_ref[...] = nm
        nv_ref[...] = nv

    spec = pl.BlockSpec((tr, c), lambda i: (i, 0))
    return pl.pallas_call(
        body, name=name, grid=(r // tr,),
        in_specs=[pl.BlockSpec((N_DEV, tr, c), lambda i: (0, i, 0)), spec, spec, spec],
        out_specs=[spec] * 4,
        out_shape=[jax.ShapeDtypeStruct((r, c), F32)] * 4,
        compiler_params=_params(("parallel",)),
    )(parts, w, m, v)


def _adamw_layers(parts, w, m, v, name, rows=256, cargo=()):
    n_l, r, c = w.shape
    tr = _tile(r, max(8, min(rows, ADAMW_TILE_ELEMS // c)), 8)
    nb = r // tr
    c1 = 1.0 - ADAM_B1 ** ADAM_STEP
    c2 = 1.0 - ADAM_B2 ** ADAM_STEP

    def body(*refs):
        p_refs = refs[:n_l]
        w_ref, m_ref, v_ref, g_ref, d_ref, nm_ref, nv_ref = refs[n_l:]
        layer = pl.program_id(0)
        for ll in range(n_l):
            @pl.when(layer == ll)
            def _(p_ref=p_refs[ll]):
                g = p_ref[0].astype(F32)
                for d in range(1, N_DEV):
                    g = g + p_ref[d].astype(F32)
                nm = ADAM_B1 * m_ref[0] + (1.0 - ADAM_B1) * g
                nv = ADAM_B2 * v_ref[0] + (1.0 - ADAM_B2) * (g * g)
                g_ref[0] = g
                d_ref[0] = -ADAM_LR * ((nm / c1) / (jnp.sqrt(nv / c2) + ADAM_EPS) + ADAM_WD * w_ref[0])
                nm_ref[0] = nm
                nv_ref[0] = nv

    def part_spec(ll):
        return pl.BlockSpec((N_DEV, tr, c),
                            lambda l, i: (0, jnp.where(l == ll, i, jnp.where(l < ll, 0, nb - 1)), 0))

    spec = pl.BlockSpec((1, tr, c), lambda l, i: (l, i, 0))
    return _pcall(body, name, (n_l, nb), [part_spec(ll) for ll in range(n_l)] + [spec, spec, spec], [spec] * 4,
                  [jax.ShapeDtypeStruct((n_l, r, c), F32)] * 4, [], ("arbitrary", "arbitrary"),
                  [*parts, w, m, v], cargo)


SMALL = ("norm_mix", "gla_gate_bias_fwd", "gla_gate_bias_bwd", "gla_norm", "rel_bias", "dil_norm", "mem_norm",
         "mem_out_norm", "norm_mlp", "norm_final")


def _pack(arrs, rows):
    flat = jnp.concatenate([a.reshape(-1) for a in arrs])
    return jnp.pad(flat, (0, rows * 128 - flat.shape[0])).reshape(rows, 128)


def _unpack(buf, shapes):
    flat = buf.reshape(-1)
    out, off = [], 0
    for shp in shapes:
        n = int(np.prod(shp))
        out.append(flat[off:off + n].reshape(shp))
        off += n
    return out


def _split_in(w):
    main = jnp.concatenate([w[..., :3072], w[..., 3104:]], axis=-1)
    lr = w[..., 3072:3104]
    pad = [(0, 0)] * (w.ndim - 1) + [(0, LR_PAD - 2 * GLA_RANK)]
    return main, jnp.pad(lr, pad)


def _join_in(main, lr):
    return jnp.concatenate([main[..., :3072], lr[..., :2 * GLA_RANK], main[..., 3072:]], axis=-1)


def kernel(x, mem, norm_mix, w_in, gla_gate_up_fwd, gla_gate_bias_fwd, gla_gate_up_bwd, gla_gate_bias_bwd, gla_norm, rel_bias, dil_norm, mem_norm, w_mem_kv, mem_out_norm, w_out, norm_mlp, w_up, w_down, norm_final, loss_target, m_norm_mix, m_w_in, m_gla_gate_up_fwd, m_gla_gate_bias_fwd, m_gla_gate_up_bwd, m_gla_gate_bias_bwd, m_gla_norm, m_rel_bias, m_dil_norm, m_mem_norm, m_w_mem_kv, m_mem_out_norm, m_w_out, m_norm_mlp, m_w_up, m_w_down, m_norm_final, v_norm_mix, v_w_in, v_gla_gate_up_fwd, v_gla_gate_bias_fwd, v_gla_gate_up_bwd, v_gla_gate_bias_bwd, v_gla_norm, v_rel_bias, v_dil_norm, v_mem_norm, v_w_mem_kv, v_mem_out_norm, v_w_out, v_norm_mlp, v_w_up, v_w_down, v_norm_final):
    weights = dict(norm_mix=norm_mix, w_in=w_in, gla_gate_up_fwd=gla_gate_up_fwd, gla_gate_bias_fwd=gla_gate_bias_fwd,
                   gla_gate_up_bwd=gla_gate_up_bwd, gla_gate_bias_bwd=gla_gate_bias_bwd, gla_norm=gla_norm,
                   rel_bias=rel_bias, dil_norm=dil_norm, mem_norm=mem_norm, w_mem_kv=w_mem_kv,
                   mem_out_norm=mem_out_norm, w_out=w_out, norm_mlp=norm_mlp, w_up=w_up, w_down=w_down,
                   norm_final=norm_final)
    mom1 = dict(norm_mix=m_norm_mix, w_in=m_w_in, gla_gate_up_fwd=m_gla_gate_up_fwd,
                gla_gate_bias_fwd=m_gla_gate_bias_fwd, gla_gate_up_bwd=m_gla_gate_up_bwd,
                gla_gate_bias_bwd=m_gla_gate_bias_bwd, gla_norm=m_gla_norm, rel_bias=m_rel_bias, dil_norm=m_dil_norm,
                mem_norm=m_mem_norm, w_mem_kv=m_w_mem_kv, mem_out_norm=m_mem_out_norm, w_out=m_w_out,
                norm_mlp=m_norm_mlp, w_up=m_w_up, w_down=m_w_down, norm_final=m_norm_final)
    mom2 = dict(norm_mix=v_norm_mix, w_in=v_w_in, gla_gate_up_fwd=v_gla_gate_up_fwd,
                gla_gate_bias_fwd=v_gla_gate_bias_fwd, gla_gate_up_bwd=v_gla_gate_up_bwd,
                gla_gate_bias_bwd=v_gla_gate_bias_bwd, gla_norm=v_gla_norm, rel_bias=v_rel_bias, dil_norm=v_dil_norm,
                mem_norm=v_mem_norm, w_mem_kv=v_w_mem_kv, mem_out_norm=v_mem_out_norm, w_out=v_w_out,
                norm_mlp=v_norm_mlp, w_up=v_w_up, w_down=v_w_down, norm_final=v_norm_final)

    s, d = x.shape[1], x.shape[2]
    xs = x.reshape(s, d)
    mems = mem.reshape(mem.shape[1], d)
    target = loss_target.reshape(s, d)
    me = _my_index()
    n_layers = w_in.shape[0]
    gate_w = gla_gate_up_fwd.shape[2]

    shard = lambda name, l: (weights[name][l].astype(BF16), False)
    cols = lambda t: jnp.moveaxis(t, 0, 1).reshape(t.shape[1], N_DEV * t.shape[2])
    rows = lambda t: t.reshape(N_DEV * t.shape[1], t.shape[2])
    in_names = ("w_in", "gla_gate_up_fwd", "gla_gate_up_bwd")

    def in_mats(g_in, g_upf, g_upb):
        w_main, w_lr = _split_in(cols(g_in))
        up_f, up_b = cols(g_upf), cols(g_upb)
        zeros_up = jnp.zeros((GLA_RANK, HEADS * GLA_DK), BF16)
        pad_rows = jnp.zeros((LR_PAD - 2 * GLA_RANK, HEADS * GLA_DK), BF16)
        up_pad_f = jnp.concatenate([up_f, zeros_up, pad_rows], axis=0)
        up_pad_b = jnp.concatenate([zeros_up, up_b, pad_rows], axis=0)
        return dict(w_main=w_main, w_lr=w_lr, up_pad_f=up_pad_f, up_pad_b=up_pad_b,
                    up_cat=jnp.concatenate([up_pad_f, up_pad_b], axis=1))

    def halves(name, l):
        w = weights[name][l].astype(BF16)
        return (w[:w.shape[0] // 2], False), (w[w.shape[0] // 2:], False)

    join_cols = lambda ga, gb: jnp.concatenate([cols(ga), cols(gb)], axis=0)

    first_names = in_names + ("w_mem_kv",)
    g0 = _exchange([shard(nm, 0) for nm in first_names], "ag_weights_0")
    wts = [dict() for _ in range(n_layers)]
    wts[0].update(in_mats(*g0[:3]), wkv=rows(g0[3]))

    row2 = lambda t: t.reshape(1, -1)

    bands, bias_a, bias_b = [], [], []
    for bi, dilation in enumerate(DILATIONS):
        band = _band_buckets(dilation)
        ba, bb = _bias_tiles(band, rel_bias, _dilf_tq(_dilf_block(s), dilation), f"bias_tiles_{bi}")
        bands.append(band), bias_a.append(ba), bias_b.append(bb)

    saved = []
    xl = xs
    for l in range(n_layers):
        wl = wts[l]
        nxt = l + 1 < n_layers
        up_a, up_b = halves("w_up", l)
        h = _rmsnorm_fwd(xl, row2(norm_mix[l]), f"norm_mix_{l}")
        proj, g_up_a = _mm(h, wl["w_main"], "nn", [F32], f"proj_{l}", cargo=[up_a])
        (lr,) = _mm(h, wl["w_lr"], "nn", [F32], f"proj_lr_{l}")
        bias_f, bias_b_ = row2(gla_gate_bias_fwd[l]), row2(gla_gate_bias_bwd[l])
        o_f, st_f, g_out = _gla4_fwd(proj, lr, wl["up_pad_f"], bias_f, False, f"gla_fwd_f_{l}",
                                     cargo=[shard("w_out", l)])
        wl["wout"] = rows(g_out)
        o_b, st_b = _gla4_fwd(proj, lr, wl["up_pad_b"], bias_b_, True, f"gla_fwd_b_{l}")
        dil_o, dil_lse, g_up_b = _dilf_fwd(proj, bias_a, f"dil_fwd_{l}", cargo=[up_b])
        wl["wup"] = join_cols(g_up_a, g_up_b)
        hm = _rmsnorm_fwd(mems, row2(mem_norm[l]), f"norm_mem_{l}")
        (kv,) = _mm(hm, wl["wkv"], "nn", [F32], f"mem_kv_{l}")
        mem_o, mem_lse = _mem_fwd(proj, kv, f"mem_fwd_{l}")
        mixed = _mix_fwd(o_f, o_b, proj, dil_o, mem_o, row2(gla_norm[l]), row2(dil_norm[l]), row2(mem_out_norm[l]),
                         f"mix_fwd_{l}")
        (x1,) = _mm(mixed, wl["wout"], "nn", [F32], f"out_proj_{l}",
                    epilogue=lambda acc, res: (acc + res,), extras=(xl,))
        h2 = _rmsnorm_fwd(x1, row2(norm_mlp[l]), f"norm_mlp_{l}")
        a, u, g_down = _mm(h2, wl["wup"], "nn", [F32, BF16], f"mlp_up_{l}",
                           epilogue=lambda acc: (acc, jnp.square(jnp.maximum(acc, 0.0))),
                           cargo=[shard("w_down", l)])
        wl["wdown"] = rows(g_down)
        x2, *got = _mm(u, wl["wdown"], "nn", [F32], f"mlp_down_{l}",
                       epilogue=lambda acc, res: (acc + res,), extras=(x1,),
                       cargo=[shard(nm, l + 1) for nm in first_names] if nxt else [])
        if nxt:
            wts[l + 1].update(in_mats(*got[:3]), wkv=rows(got[3]))
        saved.append(dict(x0=xl, h=h, proj=proj, lr=lr, o_f=o_f, o_b=o_b, st_f=st_f, st_b=st_b, hm=hm, kv=kv,
                          mem_o=mem_o, mem_lse=mem_lse, mixed=mixed, dil_o=dil_o, dil_lse=dil_lse, x1=x1, h2=h2,
                          a=a, u=u))
        xl = x2

    dx, dx_bf, dg_final, loss_part = _loss_head(xl, row2(norm_final), target, "loss_head")

    to_cols = lambda t: jnp.moveaxis(t.reshape(t.shape[0], N_DEV, -1), 1, 0)
    to_rows = lambda t: t.reshape(N_DEV, -1, t.shape[1])
    tail_names = ("w_mem_kv", "w_in", "gla_gate_up_fwd", "gla_gate_up_bwd")
    recv = {nm: [None] * n_layers for nm in tail_names + ("w_out", "w_up", "w_down")}
    tail = None
    grads_small = {k: [None] * n_layers for k in SMALL}
    dbias_sum = [None] * len(DIL_CONFIGS)
    for l in range(n_layers - 1, -1, -1):
        sv = saved[l]
        wl = wts[l]
        dw_down, *got = _mm(sv["u"], dx_bf, "tn", [BF16], f"dw_down_{l}", cargo=tail or [])
        if tail:
            for nm, part in zip(tail_names, got):
                recv[nm][l + 1] = part
        (da,) = _mm(dx_bf, wl["wdown"], "nt", [BF16], f"d_mlp_act_{l}",
                    epilogue=lambda acc, a_: (acc * (2.0 * jnp.maximum(a_, 0.0)),), extras=(sv["a"],))
        dw_up, recv["w_down"][l] = _mm(sv["h2"], da, "tn", [BF16], f"dw_up_{l}", cargo=[(to_rows(dw_down), True)])
        dh2, recv["w_up"][l] = _mm(da, wl["wup"], "nt", [F32], f"d_norm_mlp_in_{l}", cargo=[(to_cols(dw_up), True)])
        dx1, dx1_bf, dg_mlp = _rmsnorm_bwd(sv["x1"], row2(norm_mlp[l]), dh2, dx, f"norm_mlp_bwd_{l}")
        (dw_out,) = _mm(sv["mixed"], dx1_bf, "tn", [BF16], f"dw_out_{l}")
        dmixed, recv["w_out"][l] = _mm(dx1_bf, wl["wout"], "nt", [F32], f"d_mixed_{l}",
                                       cargo=[(to_rows(dw_out), True)])
        (d_og, d_r, d_od, dl_d, d_om, dl_m, dg_gla, dg_dil, dg_memo) = _mix_bwd(
            dmixed, sv["o_f"], sv["o_b"], sv["proj"], sv["dil_o"], sv["mem_o"],
            row2(gla_norm[l]), row2(dil_norm[l]), row2(mem_out_norm[l]), f"mix_bwd_{l}")
        bias_f, bias_b_ = row2(gla_gate_bias_fwd[l]), row2(gla_gate_bias_bwd[l])
        dq1, dk1, dv1, dz_f = _gla4_bwd(sv["proj"], sv["lr"], wl["up_pad_f"], bias_f, sv["st_f"], d_og, None, False,
                                        f"gla_bwd_f_{l}")
        dq_g, dk_g, dv_g, dz_b = _gla4_bwd(sv["proj"], sv["lr"], wl["up_pad_b"], bias_b_, sv["st_b"], d_og,
                                           (dq1, dk1, dv1), True, f"gla_bwd_b_{l}")
        dz = jnp.concatenate([dz_f, dz_b], axis=1)
        (d_lr,) = _mm(dz, wl["up_cat"], "nt", [BF16], f"d_lowrank_{l}")
        (d_upcat,) = _mm(sv["lr"], dz, "tn", [BF16], f"dw_gate_up_{l}")
        dzsum = _colsum(dz, f"d_gate_bias_{l}")
        dq_d, *dbias = _dilf_bwd_q(sv["proj"], d_od, sv["dil_lse"], dl_d, bias_a, f"dil_bwd_q_{l}")
        dk_d, dv_d = _dilf_bwd_kv(sv["proj"], d_od, sv["dil_lse"], dl_d, bias_b, f"dil_bwd_kv_{l}")
        for bi in range(len(DILATIONS)):
            dbias_sum[bi] = dbias[bi] if dbias_sum[bi] is None else dbias_sum[bi] + dbias[bi]
        dq_m, dkm, dvm = _mem_bwd(sv["proj"], sv["kv"], d_om, sv["mem_lse"], dl_m, f"mem_bwd_{l}")
        dkv = jnp.concatenate([dkm, dvm], axis=1).astype(BF16)
        (dw_kv,) = _mm(sv["hm"], dkv, "tn", [BF16], f"dw_mem_kv_{l}")
        (dhm,) = _mm(dkv, wl["wkv"], "nt", [F32], f"d_mem_norm_in_{l}")
        _, _, dg_mem = _rmsnorm_bwd(mems, row2(mem_norm[l]), dhm, None, f"norm_mem_bwd_{l}")
        dproj = jnp.concatenate([t.astype(BF16) for t in (dq_g, dk_g, dv_g, d_r, dq_d, dk_d, dv_d, dq_m)], axis=1)
        (dw_main,) = _mm(sv["h"], dproj, "tn", [BF16], f"dw_in_{l}")
        (dw_lr,) = _mm(sv["h"], d_lr, "tn", [BF16], f"dw_in_lr_{l}")
        (dh_lr,) = _mm(d_lr, wl["w_lr"], "nt", [F32], f"d_norm_mix_in_lr_{l}")
        (dh,) = _mm(dproj, wl["w_main"], "nt", [F32], f"d_norm_mix_in_{l}",
                    epilogue=lambda acc, other: (acc + other,), extras=(dh_lr,))
        dx, dx_bf, dg_mix = _rmsnorm_bwd(sv["x0"], row2(norm_mix[l]), dh, dx1, f"norm_mix_bwd_{l}")
        tail = [(to_rows(dw_kv), True), (to_cols(_join_in(dw_main, dw_lr)), True),
                (to_cols(d_upcat[:GLA_RANK, :HEADS * GLA_DK]), True),
                (to_cols(d_upcat[GLA_RANK:2 * GLA_RANK, HEADS * GLA_DK:]), True)]

        grads_small["norm_mix"][l] = dg_mix
        grads_small["gla_gate_bias_fwd"][l] = dzsum[:, :HEADS * GLA_DK]
        grads_small["gla_gate_bias_bwd"][l] = dzsum[:, HEADS * GLA_DK:]
        grads_small["gla_norm"][l] = dg_gla
        grads_small["dil_norm"][l] = dg_dil
        grads_small["mem_norm"][l] = dg_mem
        grads_small["mem_out_norm"][l] = dg_memo
        grads_small["norm_mlp"][l] = dg_mlp

    upd = lambda nm, cargo=(): _adamw_layers(recv[nm], weights[nm], mom1[nm], mom2[nm], f"adamw_{nm}", cargo=cargo)
    big = {}
    *big["w_down"], recv["w_in"][0] = upd("w_down", [tail[1]])
    *big["w_up"], recv["w_mem_kv"][0], recv["gla_gate_up_fwd"][0], recv["gla_gate_up_bwd"][0] = upd(
        "w_up", [tail[0], tail[2], tail[3]])
    for nm in tail_names + ("w_out",):
        big[nm] = upd(nm)
    d_table = _bias_grad(bands, dbias_sum, "bias_grad")[:, :HEADS]

    small_shapes = [weights[k].shape for k in SMALL]
    small_grads = []
    for k in SMALL:
        if k == "rel_bias":
            small_grads.append(d_table)
        elif k == "norm_final":
            small_grads.append(dg_final)
        else:
            small_grads.append(jnp.concatenate(grads_small[k], axis=0))
    n_small = sum(int(np.prod(shp)) for shp in small_shapes)
    small_rows = -(-(n_small + 128) // (8 * 128)) * 8
    pack = lambda arrs, extra: _pack(list(arrs) + [extra], small_rows)
    zeros_tail = jnp.zeros((128,), F32)
    (small_parts,) = _exchange([(pack(small_grads, loss_part.reshape(-1)), False)], "ag_small")
    sg, sd, sm, sv_ = _adamw(small_parts, pack([weights[k] for k in SMALL], zeros_tail),
                             pack([mom1[k] for k in SMALL], zeros_tail),
                             pack([mom2[k] for k in SMALL], zeros_tail), "adamw_small")
    loss = sg.reshape(-1)[n_small]
    small_out = [dict(zip(SMALL, _unpack(buf, small_shapes))) for buf in (sg, sd, sm, sv_)]

    order = ("norm_mix", "w_in", "gla_gate_up_fwd", "gla_gate_bias_fwd", "gla_gate_up_bwd", "gla_gate_bias_bwd",
             "gla_norm", "rel_bias", "dil_norm", "mem_norm", "w_mem_kv", "mem_out_norm", "w_out", "norm_mlp", "w_up",
             "w_down", "norm_final")
    outs = [loss, dx.reshape(x.shape)]
    for which in range(4):
        for name in order:
            outs.append(big[name][which] if name in big else small_out[which][name])
    return tuple(outs)
```
